```python
import math
import jax, jax.numpy as jnp
from jax import lax
import numpy as np

D_MODEL = 1024
BATCH = 8
SEQ = 4096
DEPTH = 4
DEC_BATCH = 2
DEC_SEQ = 8192
PAST_LEN = 128

N_EVEN = (DEPTH + 1) // 2
N_ODD = DEPTH // 2
MIX_WIDTH = D_MODEL
D_FF = 2816
RMS_EPS = 1e-6

D_A = MIX_WIDTH // 2
S5_GROUP = 16
S5_GROUPS = D_A // S5_GROUP
S5_STATE = 64

D_B = MIX_WIDTH - D_A
RW_HEAD = 64
RW_HEADS = D_B // RW_HEAD
RW_DECAY_LORA = D_MODEL // 32
RW_A_LORA = D_MODEL // 32
RW_GATE_LORA = D_MODEL // 16
RW_IN = 3 * D_B + 2 * RW_DECAY_LORA + RW_A_LORA + RW_GATE_LORA
RW_GN_EPS = 64e-5
AB_IN = D_A + RW_IN

D_C = MIX_WIDTH // 2
LRU_HEADS = 8
LRU_HEAD_DIM = D_C // LRU_HEADS
LRU_CONV = 4
LRU_PAD_LO = LRU_CONV // 2
LRU_PAD_HI = LRU_CONV - 1 - LRU_PAD_LO
LRU_C = 8.0

D_D = MIX_WIDTH - D_C
HY_ORDER = 2
HY_SHORT = 3
HY_POS_EMB = 33
HY_FILTER_HIDDEN = 64
HY_DECAY_PCT_SHORT = 0.3
HY_DECAY_PCT_LONG = 1.5
HY_DECAY_TARGET = 1e-2
CD_IN = 2 * D_C + (HY_ORDER + 1) * D_D

kernel_name = 'hybrid_bidir_s5_rwkv7_rglru_hyena'


def _rmsnorm(x, g):
    x32 = x.astype(jnp.float32)
    y = x32 * lax.rsqrt(jnp.mean(x32 * x32, axis=-1, keepdims=True) + RMS_EPS)
    return (y * g.astype(jnp.float32)).astype(x.dtype)


def _swiglu(x, w_gate, w_up, w_down):
    return (jax.nn.silu(x @ w_gate) * (x @ w_up)) @ w_down


def _depthwise_conv(x, w, b, pad_lo, pad_hi):
    c = x.shape[-1]
    y = lax.conv_general_dilated(x, w[:, None, :].astype(x.dtype), window_strides=(1,),
                                 padding=[(pad_lo, pad_hi)],
                                 dimension_numbers=('NWC', 'WIO', 'NWC'),
                                 feature_group_count=c)
    return y + b.astype(x.dtype)


def _linear_combine(e1, e2):
    a1, b1 = e1
    a2, b2 = e2
    return a1 * a2, a2 * b1 + b2


def _complex_linear_combine(e1, e2):
    ar1, ai1, br1, bi1 = e1
    ar2, ai2, br2, bi2 = e2
    ar = ar2 * ar1 - ai2 * ai1
    ai = ar2 * ai1 + ai2 * ar1
    br = ar2 * br1 - ai2 * bi1 + br2
    bi = ar2 * bi1 + ai2 * br1 + bi2
    return ar, ai, br, bi


def _s5_direction(u, lam_re, lam_im, log_step, b_re, b_im, c_re, c_im, reverse):
    f32 = jnp.float32
    lam_re = jnp.minimum(lam_re.astype(f32), -1e-4)
    lam_im = lam_im.astype(f32)
    step = jnp.exp(log_step.astype(f32))[:, None]
    mag = jnp.exp(lam_re * step)
    ang = lam_im * step
    abar_re = mag * jnp.cos(ang)
    abar_im = mag * jnp.sin(ang)
    den = lam_re * lam_re + lam_im * lam_im
    num_re = abar_re - 1.0
    coef_re = (num_re * lam_re + abar_im * lam_im) / den
    coef_im = (abar_im * lam_re - num_re * lam_im) / den
    b_re = b_re.astype(f32)
    b_im = b_im.astype(f32)
    bbar_re = coef_re[..., None] * b_re - coef_im[..., None] * b_im
    bbar_im = coef_re[..., None] * b_im + coef_im[..., None] * b_re
    bu_re = jnp.einsum('blgc,gnc->blgn', u, bbar_re)
    bu_im = jnp.einsum('blgc,gnc->blgn', u, bbar_im)
    a_re = jnp.broadcast_to(abar_re, bu_re.shape)
    a_im = jnp.broadcast_to(abar_im, bu_re.shape)
    _, _, h_re, h_im = lax.associative_scan(_complex_linear_combine, (a_re, a_im, bu_re, bu_im),
                                            reverse=reverse, axis=1)
    return (jnp.einsum('blgn,gcn->blgc', h_re, c_re.astype(f32))
            - jnp.einsum('blgn,gcn->blgc', h_im, c_im.astype(f32)))


def _s5_mixer(u, lam_re, lam_im, log_step, b_re, b_im, c_re, c_im, d, glu_w, glu_b):
    bsz, seq_len, _ = u.shape
    u32 = u.astype(jnp.float32)
    ug = u32.reshape(bsz, seq_len, S5_GROUPS, S5_GROUP)
    y = _s5_direction(ug, lam_re[0], lam_im[0], log_step[0], b_re[0], b_im[0], c_re[0], c_im[0], False)
    y = y + _s5_direction(ug, lam_re[1], lam_im[1], log_step[1], b_re[1], b_im[1], c_re[1], c_im[1], True)
    y = y.reshape(bsz, seq_len, D_A) + d.astype(jnp.float32) * u32
    y = jax.nn.gelu(y).astype(u.dtype)
    return y * jax.nn.sigmoid(y @ glu_w + glu_b)


def _centred_shift(p):
    prev = jnp.pad(p[:, :-1], ((0, 0), (1, 0), (0, 0)))
    nxt = jnp.pad(p[:, 1:], ((0, 0), (0, 1), (0, 0)))
    return 0.5 * (prev + nxt)


def _rwkv7_step(state, inp):
    r, w, k, v, kk, a = inp
    sa = jnp.einsum('bhvk,bhk->bhv', state, -kk)
    state = (state * w[:, :, None, :] + sa[..., None] * (kk * a)[:, :, None, :]
             + v[..., None] * k[:, :, None, :])
    y = jnp.einsum('bhvk,bhk->bhv', state, r)
    return state, y


def _rwkv7_mixer(p, mu, w0, w_up, a0, a_up, g_up, k_k, k_a, r_k, ln_w, ln_b):
    f32 = jnp.float32
    bsz, seq_len, _ = p.shape
    out_dtype = p.dtype
    p = (p + (_centred_shift(p) - p) * mu).astype(f32)
    s0 = 3 * D_B
    splits = [D_B, 2 * D_B, s0, s0 + RW_DECAY_LORA, s0 + 2 * RW_DECAY_LORA,
              s0 + 2 * RW_DECAY_LORA + RW_A_LORA]
    r, k, v, xw_f, xw_b, xa, xg = jnp.split(p, splits, axis=-1)
    a = jax.nn.sigmoid(a0 + xa @ a_up)
    g = jax.nn.sigmoid(xg) @ g_up

    def heads(t):
        return t.astype(f32).reshape(bsz, seq_len, RW_HEADS, RW_HEAD)

    kk = heads(k * k_k)
    kk = kk / jnp.maximum(jnp.sqrt(jnp.sum(kk * kk, axis=-1, keepdims=True)), 1e-12)
    k = k * (1.0 + (a - 1.0) * k_a)
    r_h, k_h, v_h, a_h = heads(r), heads(k), heads(v), heads(a)

    def tm(t):
        return jnp.swapaxes(t, 0, 1)

    state0 = jnp.zeros((bsz, RW_HEADS, RW_HEAD, RW_HEAD), f32)
    y = None
    for direction, xw in enumerate((xw_f, xw_b)):
        wl = -jax.nn.softplus(-(w0[direction] + jnp.tanh(xw) @ w_up[direction])) - 0.5
        decay = heads(jnp.exp(-jnp.exp(wl)))
        _, yd = lax.scan(_rwkv7_step, state0,
                         (tm(r_h), tm(decay), tm(k_h), tm(v_h), tm(kk), tm(a_h)),
                         reverse=(direction == 1))
        y = yd if y is None else y + yd
    y = jnp.swapaxes(y, 0, 1)
    mean = jnp.mean(y, axis=-1, keepdims=True)
    var = jnp.mean((y - mean) ** 2, axis=-1, keepdims=True)
    y = (y - mean) * lax.rsqrt(var + RW_GN_EPS)
    y = y * ln_w.reshape(RW_HEADS, RW_HEAD) + ln_b.reshape(RW_HEADS, RW_HEAD)
    y = y + jnp.sum(r_h * k_h * r_k, axis=-1, keepdims=True) * v_h
    return (y.reshape(bsz, seq_len, D_B) * g).astype(out_dtype)


def _rglru_mixer(xb, gb, conv_w, conv_b, lam, wa, ba, wx, bx):
    f32 = jnp.float32
    bsz, seq_len, _ = xb.shape
    xc = _depthwise_conv(xb, conv_w, conv_b, LRU_PAD_LO, LRU_PAD_HI).astype(f32)
    xh = xc.reshape(bsz, seq_len, LRU_HEADS, LRU_HEAD_DIM)
    h_sum = None
    for direction in range(2):
        gate_r = jax.nn.sigmoid(jnp.einsum('blhi,hij->blhj', xh, wa[direction].astype(f32))
                                .reshape(bsz, seq_len, D_C) + ba[direction])
        gate_i = jax.nn.sigmoid(jnp.einsum('blhi,hij->blhj', xh, wx[direction].astype(f32))
                                .reshape(bsz, seq_len, D_C) + bx[direction])
        log_a = -LRU_C * gate_r * jax.nn.softplus(-lam[direction].astype(f32))
        a = jnp.exp(log_a)
        mult = jnp.sqrt(-jnp.expm1(2.0 * log_a))
        _, h = lax.associative_scan(_linear_combine, (a, mult * gate_i * xc),
                                    reverse=(direction == 1), axis=1)
        h_sum = h if h_sum is None else h_sum + h
    return (h_sum * jax.nn.gelu(gb.astype(f32))).astype(xb.dtype)


def _hyena_filters(seq_len, f_w1, f_b1, f_w2, f_b2, f_freq, f_w3):
    f32 = jnp.float32
    t = jnp.linspace(0.0, 1.0, seq_len, dtype=f32)[:, None]
    bands = (HY_POS_EMB - 1) // 2
    freqs = jnp.linspace(1e-4, bands - 1, bands, dtype=f32)[None, :]
    wpos = (2.0 * math.pi / seq_len) * jnp.arange(seq_len, dtype=f32)[:, None]
    z = jnp.concatenate([t, jnp.cos(freqs * wpos), -jnp.sin(freqs * wpos)], axis=-1)
    h = jnp.sin(f_freq[0].astype(f32) * (z @ f_w1.astype(f32) + f_b1.astype(f32)))
    h = jnp.sin(f_freq[1].astype(f32) * (h @ f_w2.astype(f32) + f_b2.astype(f32)))
    h = (h @ f_w3.astype(f32)).reshape(seq_len, 2, HY_ORDER, D_D)
    max_decay = math.log(HY_DECAY_TARGET) / HY_DECAY_PCT_SHORT
    min_decay = math.log(HY_DECAY_TARGET) / HY_DECAY_PCT_LONG
    deltas = jnp.abs(jnp.linspace(min_decay, max_decay, D_D, dtype=f32))
    h = h * jnp.exp(-t * deltas)[:, None, None, :]
    h_fwd, h_bwd = h[:, 0], h[:, 1]
    k = jnp.concatenate([h_fwd, jnp.zeros((1, HY_ORDER, D_D), f32), h_bwd[:0:-1]], axis=0)
    k = k / jnp.sum(jnp.abs(k), axis=0, keepdims=True)
    return jnp.fft.rfft(k, axis=0)


def _hyena_mixer(p, conv_w, conv_b, filt_f, bias):
    f32 = jnp.float32
    bsz, seq_len, _ = p.shape
    pc = _depthwise_conv(p, conv_w, conv_b, 1, 1).astype(f32)
    v, x1, x2 = jnp.split(pc, 3, axis=-1)
    z = v
    for order, gate in enumerate((x1, x2)):
        zf = jnp.fft.rfft(z, n=2 * seq_len, axis=1)
        zc = jnp.fft.irfft(zf * filt_f[None, :, order, :], n=2 * seq_len, axis=1)[:, :seq_len]
        z = gate * (zc + bias[order].astype(f32) * z)
    return z.astype(p.dtype)


def setup_inputs(seed: int = 0) -> dict:
    key = jax.random.key(seed)
    ks = iter(jax.random.split(key, 64))
    f32 = jnp.float32

    def nrm(shape, scale):
        return scale * jax.random.normal(next(ks), shape, f32)

    def unif(shape, lo, hi):
        return jax.random.uniform(next(ks), shape, f32, lo, hi)

    NE, NO = N_EVEN, N_ODD
    inp = {}
    inp['x_prompt'] = nrm((BATCH, SEQ, D_MODEL), 1.0)
    inp['x_sample'] = nrm((DEC_BATCH, DEC_SEQ, D_MODEL), 1.0)
    inp['ffn1_norm'] = 1.0 + nrm((DEPTH, D_MODEL), 0.02)
    inp['ffn1_w_gate'] = nrm((DEPTH, D_MODEL, D_FF), D_MODEL ** -0.5)
    inp['ffn1_w_up'] = nrm((DEPTH, D_MODEL, D_FF), D_MODEL ** -0.5)
    inp['ffn1_w_down'] = nrm((DEPTH, D_FF, D_MODEL), D_FF ** -0.5)
    inp['mix_norm'] = 1.0 + nrm((DEPTH, D_MODEL), 0.02)
    inp['ffn2_norm'] = 1.0 + nrm((DEPTH, D_MODEL), 0.02)
    inp['ffn2_w_gate'] = nrm((DEPTH, D_MODEL, D_FF), D_MODEL ** -0.5)
    inp['ffn2_w_up'] = nrm((DEPTH, D_MODEL, D_FF), D_MODEL ** -0.5)
    inp['ffn2_w_down'] = nrm((DEPTH, D_FF, D_MODEL), D_FF ** -0.5)
    inp['ab_w_in'] = nrm((NE, D_MODEL, AB_IN), D_MODEL ** -0.5)
    inp['ab_w_out'] = nrm((NE, MIX_WIDTH, D_MODEL), MIX_WIDTH ** -0.5)
    inp['s5_lambda_re'] = -0.5 + nrm((NE, 2, S5_GROUPS, S5_STATE), 0.01)
    inp['s5_lambda_im'] = math.pi * jnp.arange(S5_STATE, dtype=f32) + nrm((NE, 2, S5_GROUPS, S5_STATE), 0.01)
    inp['s5_log_step'] = unif((NE, 2, S5_GROUPS), math.log(1e-3), math.log(1e-1))
    inp['s5_b_re'] = nrm((NE, 2, S5_GROUPS, S5_STATE, S5_GROUP), (2 * S5_GROUP) ** -0.5)
    inp['s5_b_im'] = nrm((NE, 2, S5_GROUPS, S5_STATE, S5_GROUP), (2 * S5_GROUP) ** -0.5)
    inp['s5_c_re'] = nrm((NE, 2, S5_GROUPS, S5_GROUP, S5_STATE), S5_STATE ** -0.5)
    inp['s5_c_im'] = nrm((NE, 2, S5_GROUPS, S5_GROUP, S5_STATE), S5_STATE ** -0.5)
    inp['s5_d'] = nrm((NE, D_A), 1.0)
    inp['s5_glu_w'] = nrm((NE, D_A, D_A), D_A ** -0.5)
    inp['s5_glu_b'] = nrm((NE, D_A), 0.01)
    inp['rw_mu'] = unif((NE, RW_IN), 0.0, 1.0)
    inp['rw_w0'] = jnp.linspace(-6.0, -1.0, D_B, dtype=f32) + nrm((NE, 2, D_B), 0.1)
    inp['rw_w_up'] = nrm((NE, 2, RW_DECAY_LORA, D_B), 0.1)
    inp['rw_a0'] = nrm((NE, D_B), 0.1)
    inp['rw_a_up'] = nrm((NE, RW_A_LORA, D_B), RW_A_LORA ** -0.5)
    inp['rw_g_up'] = nrm((NE, RW_GATE_LORA, D_B), RW_GATE_LORA ** -0.5)
    inp['rw_k_k'] = 0.85 + nrm((NE, D_B), 0.05)
    inp['rw_k_a'] = 1.0 + nrm((NE, D_B), 0.05)
    inp['rw_r_k'] = -0.04 + nrm((NE, RW_HEADS, RW_HEAD), 0.02)
    inp['rw_ln_w'] = 1.0 + nrm((NE, D_B), 0.02)
    inp['rw_ln_b'] = nrm((NE, D_B), 0.01)
    inp['cd_w_in'] = nrm((NO, D_MODEL, CD_IN), D_MODEL ** -0.5)
    inp['cd_w_out'] = nrm((NO, MIX_WIDTH, D_MODEL), MIX_WIDTH ** -0.5)
    inp['lru_conv_w'] = nrm((NO, LRU_CONV, D_C), LRU_CONV ** -0.5)
    inp['lru_conv_b'] = nrm((NO, D_C), 0.01)
    a_c = unif((NO, 2, D_C), 0.9, 0.999)
    a_base = a_c ** (1.0 / LRU_C)
    inp['lru_lambda'] = jnp.log(a_base) - jnp.log1p(-a_base)
    inp['lru_wa'] = nrm((NO, 2, LRU_HEADS, LRU_HEAD_DIM, LRU_HEAD_DIM), LRU_HEAD_DIM ** -0.5)
    inp['lru_ba'] = nrm((NO, 2, D_C), 0.01)
    inp['lru_wx'] = nrm((NO, 2, LRU_HEADS, LRU_HEAD_DIM, LRU_HEAD_DIM), LRU_HEAD_DIM ** -0.5)
    inp['lru_bx'] = nrm((NO, 2, D_C), 0.01)
    inp['hy_conv_w'] = nrm((NO, HY_SHORT, (HY_ORDER + 1) * D_D), HY_SHORT ** -0.5)
    inp['hy_conv_b'] = nrm((NO, (HY_ORDER + 1) * D_D), 0.01)
    inp['hy_f_w1'] = nrm((NO, HY_POS_EMB, HY_FILTER_HIDDEN), HY_POS_EMB ** -0.5)
    inp['hy_f_b1'] = nrm((NO, HY_FILTER_HIDDEN), 0.1)
    inp['hy_f_w2'] = nrm((NO, HY_FILTER_HIDDEN, HY_FILTER_HIDDEN), HY_FILTER_HIDDEN ** -0.5)
    inp['hy_f_b2'] = nrm((NO, HY_FILTER_HIDDEN), 0.1)
    inp['hy_f_freq'] = 1.0 + nrm((NO, 2, HY_FILTER_HIDDEN), 0.1)
    inp['hy_f_w3'] = nrm((NO, HY_FILTER_HIDDEN, 2 * HY_ORDER * D_D), HY_FILTER_HIDDEN ** -0.5)
    inp['hy_bias'] = nrm((NO, HY_ORDER, D_D), 1.0)
    inp['final_norm'] = 1.0 + nrm((D_MODEL,), 0.02)
    return inp


def reference(x_prompt, x_sample, ffn1_norm, ffn1_w_gate, ffn1_w_up, ffn1_w_down, mix_norm,
              ffn2_norm, ffn2_w_gate, ffn2_w_up, ffn2_w_down, ab_w_in, ab_w_out,
              s5_lambda_re, s5_lambda_im, s5_log_step, s5_b_re, s5_b_im, s5_c_re, s5_c_im,
              s5_d, s5_glu_w, s5_glu_b, rw_mu, rw_w0, rw_w_up, rw_a0, rw_a_up, rw_g_up,
              rw_k_k, rw_k_a, rw_r_k, rw_ln_w, rw_ln_b, cd_w_in, cd_w_out, lru_conv_w,
              lru_conv_b, lru_lambda, lru_wa, lru_ba, lru_wx, lru_bx, hy_conv_w, hy_conv_b,
              hy_f_w1, hy_f_b1, hy_f_w2, hy_f_b2, hy_f_freq, hy_f_w3, hy_bias, final_norm):

    def trunk(x):
        seq_len = x.shape[1]
        for layer in range(DEPTH):
            j = layer // 2
            x = x + 0.5 * _swiglu(_rmsnorm(x, ffn1_norm[layer]), ffn1_w_gate[layer],
                                  ffn1_w_up[layer], ffn1_w_down[layer])
            h = _rmsnorm(x, mix_norm[layer])
            if layer % 2 == 0:
                proj = h @ ab_w_in[j]
                y_a = _s5_mixer(proj[..., :D_A], s5_lambda_re[j], s5_lambda_im[j], s5_log_step[j],
                                s5_b_re[j], s5_b_im[j], s5_c_re[j], s5_c_im[j], s5_d[j],
                                s5_glu_w[j], s5_glu_b[j])
                y_b = _rwkv7_mixer(proj[..., D_A:], rw_mu[j], rw_w0[j], rw_w_up[j], rw_a0[j],
                                   rw_a_up[j], rw_g_up[j], rw_k_k[j], rw_k_a[j], rw_r_k[j],
                                   rw_ln_w[j], rw_ln_b[j])
                x = x + jnp.concatenate([y_a, y_b], axis=-1) @ ab_w_out[j]
            else:
                proj = h @ cd_w_in[j]
                y_c = _rglru_mixer(proj[..., :D_C], proj[..., D_C:2 * D_C], lru_conv_w[j],
                                   lru_conv_b[j], lru_lambda[j], lru_wa[j], lru_ba[j],
                                   lru_wx[j], lru_bx[j])
                filt_f = _hyena_filters(seq_len, hy_f_w1[j], hy_f_b1[j], hy_f_w2[j], hy_f_b2[j],
                                        hy_f_freq[j], hy_f_w3[j])
                y_d = _hyena_mixer(proj[..., 2 * D_C:], hy_conv_w[j], hy_conv_b[j], filt_f, hy_bias[j])
                x = x + jnp.concatenate([y_c, y_d], axis=-1) @ cd_w_out[j]
            x = x + 0.5 * _swiglu(_rmsnorm(x, ffn2_norm[layer]), ffn2_w_gate[layer],
                                  ffn2_w_up[layer], ffn2_w_down[layer])
        return _rmsnorm(x, final_norm)

    y_prompt = trunk(x_prompt)
    y_sample = trunk(x_sample)
    return (y_prompt, y_sample)
```

```python
import functools
import math

import numpy as np
import jax
import jax.numpy as jnp
from jax import lax
from jax.experimental import pallas as pl
from jax.experimental.pallas import tpu as pltpu

f32 = jnp.float32
bf16 = jnp.bfloat16

D_MODEL = 1024
DEPTH = 4
D_FF = 2816
RMS_EPS = 1e-6
D_A = 512
S5_GROUP = 16
S5_GROUPS = 32
S5_STATE = 64
D_B = 512
RW_HEAD = 64
RW_HEADS = 8
RW_DECAY_LORA = 32
RW_A_LORA = 32
RW_GATE_LORA = 64
RW_GN_EPS = 64e-5
D_C = 512
LRU_HEADS = 8
LRU_HEAD_DIM = 64
LRU_CONV = 4
LRU_C = 8.0
D_D = 512
HY_POS_EMB = 33
HY_FILTER_HIDDEN = 64

LANES = 128
SUBLANES = 8
VMEM_LIMIT = 56 * 1024 * 1024
TOKEN_TILE = 512
FF_CHUNK = 1408


def _cparams(*sem):
    return pltpu.CompilerParams(dimension_semantics=tuple(sem), vmem_limit_bytes=VMEM_LIMIT)


def _const_spec(shape):
    nd = len(shape)
    return pl.BlockSpec(shape, lambda *_: (0,) * nd, pipeline_mode=pl.Buffered(1))


def _layer_spec(shape):
    def make(layer):
        nd = len(shape)
        return pl.BlockSpec((None,) + tuple(shape), lambda *_: (layer,) + (0,) * nd,
                            pipeline_mode=pl.Buffered(1))
    return make


def _rms(x, g):
    ms = jnp.mean(x * x, axis=-1, keepdims=True)
    return x * lax.rsqrt(ms + RMS_EPS) * g


def _dot(a, b):
    return jnp.dot(a, b, preferred_element_type=f32)


def _ffn_body(x_ref, g_ref, wg_ref, wu_ref, wd_ref, o_ref):
    x = x_ref[...]
    h = _rms(x, g_ref[...]).astype(bf16)
    acc = None
    for c in range(D_FF // FF_CHUNK):
        sl = slice(c * FF_CHUNK, (c + 1) * FF_CHUNK)
        gate = _dot(h, wg_ref[:, sl])
        up = _dot(h, wu_ref[:, sl])
        act = (gate * jax.nn.sigmoid(gate) * up).astype(bf16)
        part = _dot(act, wd_ref[sl, :])
        acc = part if acc is None else acc + part
    o_ref[...] = x + 0.5 * acc


def _ffn(x, norm, wg, wu, wd, layer):
    t = x.shape[0]
    return pl.pallas_call(
        _ffn_body,
        grid=(t // TOKEN_TILE,),
        in_specs=[
            pl.BlockSpec((TOKEN_TILE, D_MODEL), lambda i: (i, 0)),
            _layer_spec((1, D_MODEL))(layer),
            _layer_spec((D_MODEL, D_FF))(layer),
            _layer_spec((D_MODEL, D_FF))(layer),
            _layer_spec((D_FF, D_MODEL))(layer),
        ],
        out_specs=pl.BlockSpec((TOKEN_TILE, D_MODEL), lambda i: (i, 0)),
        out_shape=jax.ShapeDtypeStruct((t, D_MODEL), f32),
        compiler_params=_cparams("parallel"),
        name="ffn",
    )(x, norm, wg, wu, wd)


def _proj_body(x_ref, g_ref, w_ref, o_ref):
    h = _rms(x_ref[...], g_ref[...]).astype(bf16)
    o_ref[...] = _dot(h, w_ref[...])


def _proj(x, norm, w, layer, j):
    t = x.shape[0]
    p = w.shape[-1]
    return pl.pallas_call(
        _proj_body,
        grid=(t // TOKEN_TILE,),
        in_specs=[
            pl.BlockSpec((TOKEN_TILE, D_MODEL), lambda i: (i, 0)),
            _layer_spec((1, D_MODEL))(layer),
            _layer_spec((D_MODEL, p))(j),
        ],
        out_specs=pl.BlockSpec((TOKEN_TILE, p), lambda i: (i, 0)),
        out_shape=jax.ShapeDtypeStruct((t, p), f32),
        compiler_params=_cparams("parallel"),
        name="proj",
    )(x, norm, w)


def _outproj_body(x_ref, ya_ref, yb_ref, w_ref, o_ref):
    half = w_ref.shape[0] // 2
    acc = _dot(ya_ref[...].astype(bf16), w_ref[:half, :])
    acc = acc + _dot(yb_ref[...].astype(bf16), w_ref[half:, :])
    o_ref[...] = x_ref[...] + acc


def _outproj(x, ya, yb, w, j):
    t = x.shape[0]
    half = ya.shape[-1]
    return pl.pallas_call(
        _outproj_body,
        grid=(t // TOKEN_TILE,),
        in_specs=[
            pl.BlockSpec((TOKEN_TILE, D_MODEL), lambda i: (i, 0)),
            pl.BlockSpec((TOKEN_TILE, half), lambda i: (i, 0)),
            pl.BlockSpec((TOKEN_TILE, half), lambda i: (i, 0)),
            _layer_spec((2 * half, D_MODEL))(j),
        ],
        out_specs=pl.BlockSpec((TOKEN_TILE, D_MODEL), lambda i: (i, 0)),
        out_shape=jax.ShapeDtypeStruct((t, D_MODEL), f32),
        compiler_params=_cparams("parallel"),
        name="outproj",
    )(x, ya, yb, w)


def _final_norm_body(x_ref, g_ref, o_ref):
    o_ref[...] = _rms(x_ref[...], g_ref[...])


def _final_norm(x, g):
    t = x.shape[0]
    return pl.pallas_call(
        _final_norm_body,
        grid=(t // TOKEN_TILE,),
        in_specs=[pl.BlockSpec((TOKEN_TILE, D_MODEL), lambda i: (i, 0)), _const_spec((1, D_MODEL))],
        out_specs=pl.BlockSpec((TOKEN_TILE, D_MODEL), lambda i: (i, 0)),
        out_shape=jax.ShapeDtypeStruct((t, D_MODEL), f32),
        compiler_params=_cparams("parallel"),
        name="final_norm",
    )(x, g)


def _seq_edge(blk, groups, tile, last):
    hit = None
    for off, n_seq, seq_len in groups:
        start, per, n = off // tile, seq_len // tile, n_seq * (seq_len // tile)
        rel = blk - start
        edge = (per - 1) if last else 0
        h = (rel >= 0) & (rel < n) & (lax.rem(jnp.maximum(rel, 0), per) == edge)
        hit = h if hit is None else (hit | h)
    return hit


S5_TILE = 512
S5_OCT = 4
S5_OCT_STATES = 512


def _s5_param_body(lr_ref, li_ref, st_ref, lr2_ref, li2_ref, st2_ref, br_ref, bi_ref,
                   pwr_ref, pwi_ref, bbr_ref, bbi_ref):
    def discretize(lam_re, lam_im, log_step):
        lam_re = jnp.minimum(lam_re, -1e-4)
        step = jnp.exp(log_step)
        mag = jnp.exp(lam_re * step)
        ang = lam_im * step
        return lam_re, lam_im, mag * jnp.cos(ang), mag * jnp.sin(ang)

    _, _, ar, ai = discretize(lr_ref[...], li_ref[...], st_ref[...])
    pr, pi = ar, ai
    for j in range(SUBLANES):
        pwr_ref[j] = pr
        pwi_ref[j] = pi
        pr, pi = pr * ar - pi * ai, pr * ai + pi * ar
    lam_re, lam_im, ar, ai = discretize(lr2_ref[...], li2_ref[...], st2_ref[...])
    den = lam_re * lam_re + lam_im * lam_im
    num_re = ar - 1.0
    coef_re = (num_re * lam_re + ai * lam_im) / den
    coef_im = (ai * lam_re - num_re * lam_im) / den
    b_re, b_im = br_ref[...], bi_ref[...]
    bbr_ref[...] = coef_re * b_re - coef_im * b_im
    bbi_ref[...] = coef_re * b_im + coef_im * b_re


def _s5_params(lam_re, lam_im, log_step, b_re, b_im):
    g2 = 2 * S5_GROUPS
    rows = g2 * S5_GROUP
    lr = lam_re.reshape(g2, S5_STATE)
    li = lam_im.reshape(g2, S5_STATE)
    st = jnp.broadcast_to(log_step.reshape(g2, 1), (g2, S5_STATE))
    rep = lambda a: jnp.broadcast_to(a[:, None, :], (g2, S5_GROUP, S5_STATE)).reshape(rows, S5_STATE)
    bt = lambda b: jnp.swapaxes(b, -1, -2).reshape(rows, S5_STATE)
    full = lambda s: pl.BlockSpec(s, lambda: (0,) * len(s))
    pwr, pwi, bbr, bbi = pl.pallas_call(
        _s5_param_body,
        in_specs=[full((g2, S5_STATE))] * 3 + [full((rows, S5_STATE))] * 5,
        out_specs=[full((SUBLANES, g2, S5_STATE))] * 2 + [full((rows, S5_STATE))] * 2,
        out_shape=[jax.ShapeDtypeStruct((SUBLANES, g2, S5_STATE), f32)] * 2
        + [jax.ShapeDtypeStruct((rows, S5_STATE), f32)] * 2,
        name="s5_params",
    )(lr, li, st, rep(lr), rep(li), rep(st), bt(b_re), bt(b_im))
    n_all = S5_GROUPS * S5_STATE
    pw = lambda p: p.reshape(SUBLANES, 2, n_all)
    bb = lambda b: b.reshape(2, S5_GROUPS, S5_GROUP, S5_STATE)
    return pw(pwr), pw(pwi), bb(bbr), bb(bbi)


def _s5_block_mats(bb_re, bb_im, c_re, c_im):
    eye = jnp.eye(8, dtype=f32)

    def in_mat(b):
        b4 = b.reshape(S5_OCT, 8, S5_GROUP, S5_STATE)
        return jnp.einsum("qgcn,gh->qgchn", b4, eye).reshape(S5_OCT, 8 * S5_GROUP, 8 * S5_STATE)

    def out_mat(c):
        c4 = c.reshape(S5_OCT, 8, S5_GROUP, S5_STATE)
        return jnp.einsum("qgcn,gh->qgnhc", c4, eye).reshape(S5_OCT, 8 * S5_STATE, 8 * S5_GROUP)

    w_in = jnp.concatenate([in_mat(bb_re), in_mat(bb_im)], axis=-1).astype(bf16)
    return w_in, out_mat(c_re).astype(bf16), out_mat(c_im).astype(bf16)


def _s5_scan_body(u_ref, win_ref, wcr_ref, wci_ref, pwr_ref, pwi_ref, o_ref,
                  x_buf, h_buf, carry, *, groups, reverse, n_blk):
    pid = pl.program_id(0)
    blk = (n_blk - 1 - pid) if reverse else pid

    @pl.when(_seq_edge(blk, groups, S5_TILE, last=reverse))
    def _():
        carry[...] = jnp.zeros_like(carry)

    row = lax.broadcasted_iota(jnp.int32, (SUBLANES, S5_OCT_STATES), 0)
    n_tiles = S5_TILE // SUBLANES
    for q in range(S5_OCT):
        lanes = slice(q * S5_OCT_STATES, (q + 1) * S5_OCT_STATES)
        x_buf[...] = _dot(u_ref[:, q * LANES:(q + 1) * LANES].astype(bf16), win_ref[q])

        def power(j):
            shape = (SUBLANES, S5_OCT_STATES)
            return (jnp.broadcast_to(pwr_ref[j:j + 1, lanes], shape),
                    jnp.broadcast_to(pwi_ref[j:j + 1, lanes], shape))

        levels = []
        for d in (1, 2, 4):
            ar, ai = power(d - 1)
            keep = (row < SUBLANES - d) if reverse else (row >= d)
            levels.append((d, jnp.where(keep, ar, 0.0), jnp.where(keep, ai, 0.0)))
        cr = jnp.zeros((SUBLANES, S5_OCT_STATES), f32)
        ci = jnp.zeros((SUBLANES, S5_OCT_STATES), f32)
        for j in range(SUBLANES):
            pr, pi = power(j)
            sel = (row == (SUBLANES - 1 - j)) if reverse else (row == j)
            cr = jnp.where(sel, pr, cr)
            ci = jnp.where(sel, pi, ci)

        def tile_step(i, c):
            c_re, c_im = c
            t = (n_tiles - 1 - i) if reverse else i
            r0 = pl.multiple_of(t * SUBLANES, SUBLANES)
            h_re = x_buf[pl.ds(r0, SUBLANES), :S5_OCT_STATES]
            h_im = x_buf[pl.ds(r0, SUBLANES), S5_OCT_STATES:]
            for d, ar, ai in levels:
                sh = (SUBLANES - d) if reverse else d
                s_re = pltpu.roll(h_re, sh, 0)
                s_im = pltpu.roll(h_im, sh, 0)
                h_re, h_im = h_re + (ar * s_re - ai * s_im), h_im + (ar * s_im + ai * s_re)
            h_re, h_im = h_re + (cr * c_re - ci * c_im), h_im + (cr * c_im + ci * c_re)
            h_buf[pl.ds(r0, SUBLANES), :S5_OCT_STATES] = h_re
            h_buf[pl.ds(r0, SUBLANES), S5_OCT_STATES:] = h_im
            edge = 0 if reverse else SUBLANES - 1
            shape = (SUBLANES, S5_OCT_STATES)
            return (jnp.broadcast_to(h_re[edge:edge + 1, :], shape),
                    jnp.broadcast_to(h_im[edge:edge + 1, :], shape))

        c0 = (carry[:, lanes], carry[:, S5_OCT * S5_OCT_STATES + q * S5_OCT_STATES:
                                     S5_OCT * S5_OCT_STATES + (q + 1) * S5_OCT_STATES])
        c_re, c_im = lax.fori_loop(0, n_tiles, tile_step, c0)
        carry[:, lanes] = c_re
        carry[:, S5_OCT * S5_OCT_STATES + q * S5_OCT_STATES:
              S5_OCT * S5_OCT_STATES + (q + 1) * S5_OCT_STATES] = c_im
        y = _dot(h_buf[:, :S5_OCT_STATES].astype(bf16), wcr_ref[q])
        y = y - _dot(h_buf[:, S5_OCT_STATES:].astype(bf16), wci_ref[q])
        o_ref[:, q * LANES:(q + 1) * LANES] = y


def _s5_scan(proj, w_in, w_cre, w_cim, pw_re, pw_im, groups, reverse):
    t = proj.shape[0]
    n_blk = t // S5_TILE
    n_state = S5_OCT * S5_OCT_STATES
    order = (lambda i: (n_blk - 1 - i, 0)) if reverse else (lambda i: (i, 0))
    body = functools.partial(_s5_scan_body, groups=groups, reverse=reverse, n_blk=n_blk)
    return pl.pallas_call(
        body,
        grid=(n_blk,),
        in_specs=[
            pl.BlockSpec((S5_TILE, D_A), order),
            _const_spec((S5_OCT, LANES, 2 * S5_OCT_STATES)),
            _const_spec((S5_OCT, S5_OCT_STATES, LANES)),
            _const_spec((S5_OCT, S5_OCT_STATES, LANES)),
            _const_spec((SUBLANES, n_state)),
            _const_spec((SUBLANES, n_state)),
        ],
        out_specs=pl.BlockSpec((S5_TILE, D_A), order),
        out_shape=jax.ShapeDtypeStruct((t, D_A), f32),
        scratch_shapes=[
            pltpu.VMEM((S5_TILE, 2 * S5_OCT_STATES), f32),
            pltpu.VMEM((S5_TILE, 2 * S5_OCT_STATES), f32),
            pltpu.VMEM((SUBLANES, 2 * n_state), f32),
        ],
        compiler_params=_cparams("arbitrary"),
        name="s5_scan_rev" if reverse else "s5_scan_fwd",
    )(proj, w_in, w_cre, w_cim, pw_re, pw_im)


def _s5_post_body(u_ref, yf_ref, yb_ref, d_ref, w_ref, b_ref, o_ref):
    y = yf_ref[...] + yb_ref[...] + d_ref[...] * u_ref[...]
    y = jax.nn.gelu(y)
    gate = _dot(y.astype(bf16), w_ref[...]) + b_ref[...]
    o_ref[...] = y * jax.nn.sigmoid(gate)


def _s5_post(proj, y_f, y_b, d, glu_w, glu_b, j):
    t = proj.shape[0]
    tok = lambda i: (i, 0)
    return pl.pallas_call(
        _s5_post_body,
        grid=(t // TOKEN_TILE,),
        in_specs=[
            pl.BlockSpec((TOKEN_TILE, D_A), tok),
            pl.BlockSpec((TOKEN_TILE, D_A), tok),
            pl.BlockSpec((TOKEN_TILE, D_A), tok),
            _layer_spec((1, D_A))(j),
            _layer_spec((D_A, D_A))(j),
            _layer_spec((1, D_A))(j),
        ],
        out_specs=pl.BlockSpec((TOKEN_TILE, D_A), tok),
        out_shape=jax.ShapeDtypeStruct((t, D_A), f32),
        compiler_params=_cparams("parallel"),
        name="s5_post",
    )(proj, y_f, y_b, d, glu_w, glu_b)


def _s5_mixer(proj, prm, j, groups):
    pw_re, pw_im, bb_re, bb_im = _s5_params(prm["lam_re"][j], prm["lam_im"][j], prm["log_step"][j],
                                            prm["b_re"][j], prm["b_im"][j])
    ys = []
    for direction in range(2):
        w_in, w_cre, w_cim = _s5_block_mats(bb_re[direction], bb_im[direction],
                                            prm["c_re"][j, direction], prm["c_im"][j, direction])
        ys.append(_s5_scan(proj, w_in, w_cre, w_cim, pw_re[:, direction], pw_im[:, direction],
                           groups, reverse=(direction == 1)))
    return _s5_post(proj, ys[0], ys[1], prm["d"], prm["glu_w"], prm["glu_b"], j)


RW_TILE = 256
RW_CHUNK = 64
RW_PAIRS = RW_HEADS // 2
RW_P_WIDTH = 2304
RW_LORA_OFF = 2048
RW_LORA_W = 256


def _head_ones():
    a = lax.broadcasted_iota(jnp.int32, (D_B, D_B), 0) // RW_HEAD
    b = lax.broadcasted_iota(jnp.int32, (D_B, D_B), 1) // RW_HEAD
    return jnp.where(a == b, 1.0, 0.0).astype(f32)


def _rw_prep_body(x_ref, xp_ref, xn_ref, mu_ref, lw_ref, w0_ref, a0_ref, kk_ref, ka_ref,
                  r_out, k_out, v_out, kk_out, kka_out, lwf_out, lwb_out, g_out, *, groups):
    i = pl.program_id(0)
    x = x_ref[...]
    prev = jnp.where(_seq_edge(i, groups, RW_TILE, last=False), 0.0, xp_ref[...])
    nxt = jnp.where(_seq_edge(i, groups, RW_TILE, last=True), 0.0, xn_ref[...])
    row = lax.broadcasted_iota(jnp.int32, x.shape, 0)
    shifted = 0.5 * (_shift_rows(x, prev, 1, row) + _shift_rows(x, nxt, -1, row))
    p = x + (shifted - x) * mu_ref[...]
    r = p[:, D_A:D_A + D_B]
    k = p[:, D_A + D_B:D_A + 2 * D_B]
    v = p[:, D_A + 2 * D_B:D_A + 3 * D_B]
    lora = p[:, RW_LORA_OFF:RW_LORA_OFF + RW_LORA_W]
    lora_t = jnp.tanh(lora).astype(bf16)
    a = jax.nn.sigmoid(a0_ref[...] + _dot(lora.astype(bf16), lw_ref[2]))
    g = _dot(jax.nn.sigmoid(lora).astype(bf16), lw_ref[3])
    kx = k * kk_ref[...]
    ss = _dot_hi(kx * kx, _head_ones())
    kk = kx / jnp.maximum(jnp.sqrt(ss), 1e-12)
    r_out[...] = r
    k_out[...] = k * (1.0 + (a - 1.0) * ka_ref[...])
    v_out[...] = v
    kk_out[...] = kk
    kka_out[...] = kk * a
    g_out[...] = g
    for direction, out in enumerate((lwf_out, lwb_out)):
        wl = -jax.nn.softplus(-(w0_ref[direction:direction + 1, :] + _dot(lora_t, lw_ref[direction]))) - 0.5
        out[...] = -jnp.exp(wl)


def _rw_prep(proj, prm, j, groups):
    t = proj.shape[0]
    tok = lambda i: (i, 0)
    prev_spec, next_spec = _halo_specs(RW_TILE, RW_P_WIDTH, 0, t)
    out = jax.ShapeDtypeStruct((t, D_B), f32)
    return pl.pallas_call(
        functools.partial(_rw_prep_body, groups=groups),
        grid=(t // RW_TILE,),
        in_specs=[pl.BlockSpec((RW_TILE, RW_P_WIDTH), tok), prev_spec, next_spec,
                  _layer_spec((1, RW_P_WIDTH))(j), _layer_spec((4, RW_LORA_W, D_B))(j),
                  _layer_spec((2, D_B))(j), _layer_spec((1, D_B))(j), _layer_spec((1, D_B))(j),
                  _layer_spec((1, D_B))(j)],
        out_specs=[pl.BlockSpec((RW_TILE, D_B), tok)] * 8,
        out_shape=[out] * 8,
        compiler_params=_cparams("parallel"),
        name="rw_prep",
    )(proj, proj, proj, prm["mu"], prm["lora_w"], prm["w0"], prm["a0"], prm["k_k"], prm["k_a"])


def _rw_chunk_body(r_ref, k_ref, v_ref, kk_ref, kka_ref, lw_ref, rp_out, y0_out, g_out, h_out, *, reverse):
    c = RW_CHUNK
    ri = lax.broadcasted_iota(jnp.int32, (c, c), 0)
    ci = lax.broadcasted_iota(jnp.int32, (c, c), 1)
    tri = jnp.where((ci >= ri) if reverse else (ci <= ri), 1.0, 0.0).astype(f32)
    edge = 0 if reverse else c - 1
    big_r = lax.broadcasted_iota(jnp.int32, (LANES, LANES), 0)
    big_c = lax.broadcasted_iota(jnp.int32, (LANES, LANES), 1)
    same_head = (big_r // RW_HEAD) == (big_c // RW_HEAD)
    s_idx, j_idx = big_r % RW_HEAD, big_c % RW_HEAD
    strict = same_head & ((j_idx > s_idx) if reverse else (j_idx < s_idx))
    incl = same_head & ((j_idx >= s_idx) if reverse else (j_idx <= s_idx))
    eye = big_r == big_c
    eye_f = jnp.where(eye, 1.0, 0.0).astype(f32)
    head0 = lax.broadcasted_iota(jnp.int32, (c, LANES), 1) < RW_HEAD

    def expand(x):
        return jnp.concatenate([jnp.where(head0, x, 0.0), jnp.where(head0, 0.0, x)], axis=0)

    def collapse(x):
        return x[:c] + x[c:]

    nt = (((1,), (1,)), ((), ()))
    tn = (((0,), (0,)), ((), ()))
    for pair in range(RW_PAIRS):
        sl = slice(pair * LANES, (pair + 1) * LANES)
        r, k, v, kk, kka, lw = (ref[:, sl] for ref in (r_ref, k_ref, v_ref, kk_ref, kka_ref, lw_ref))
        cum = _dot_hi(tri, lw)
        total = cum[edge:edge + 1, :]
        at = -kk * jnp.exp(cum - lw)
        rt = r * jnp.exp(cum)
        e_neg = jnp.exp(-cum)
        kh, bh = k * e_neg, kka * e_neg
        lhs = jnp.concatenate([expand(at), expand(rt)], axis=0)
        rhs = jnp.concatenate([kh, kh, bh, bh], axis=0)
        z = lax.dot_general(lhs, rhs, nt, precision=_HI, preferred_element_type=f32)
        a_ak = jnp.where(strict, z[:LANES, :LANES], 0.0)
        a_ab = jnp.where(strict, z[:LANES, LANES:], 0.0)
        a_rk = jnp.where(incl, z[LANES:, :LANES], 0.0)
        a_rb = jnp.where(incl, z[LANES:, LANES:], 0.0)
        tinv = eye_f + a_ab
        pw = a_ab
        for _ in range(int(math.log2(c)) - 1):
            pw = _dot_hi(pw, pw)
            tinv = _dot_hi(tinv, eye_f + pw)
        v_exp, at_exp = expand(v), expand(at)
        w = _dot_hi(a_ak, v_exp)
        tu = _dot_hi(tinv, jnp.concatenate([at_exp, w], axis=1))
        atp_exp, u0_exp = tu[:, :LANES], tu[:, LANES:]
        y0 = collapse(_dot_hi(a_rk, v_exp) + _dot_hi(a_rb, u0_exp))
        rp = rt + collapse(_dot_hi(a_rb, atp_exp))
        atp, u0 = collapse(atp_exp), collapse(u0_exp)
        tail = jnp.exp(total - cum)
        kp, bp = k * tail, kka * tail
        g_full = lax.dot_general(bp, atp, tn, precision=_HI, preferred_element_type=f32)
        h_full = (lax.dot_general(kp, v, tn, precision=_HI, preferred_element_type=f32)
                  + lax.dot_general(bp, u0, tn, precision=_HI, preferred_element_type=f32))
        pc = jnp.broadcast_to(jnp.exp(total), (LANES, LANES))
        rp_out[:, sl] = rp
        y0_out[:, sl] = y0
        g_out[:, sl] = jnp.where(same_head, g_full, 0.0) + jnp.where(eye, pc, 0.0)
        h_out[:, sl] = jnp.where(same_head, h_full, 0.0)


def _rw_chunk(r, k, v, kk, kka, lw, reverse):
    t = r.shape[0]
    n_chunks = t // RW_CHUNK
    tok = lambda i: (i, 0)
    tok_out = jax.ShapeDtypeStruct((t, D_B), f32)
    mat_out = jax.ShapeDtypeStruct((n_chunks * LANES, D_B), f32)
    return pl.pallas_call(
        functools.partial(_rw_chunk_body, reverse=reverse),
        grid=(n_chunks,),
        in_specs=[pl.BlockSpec((RW_CHUNK, D_B), tok)] * 6,
        out_specs=[pl.BlockSpec((RW_CHUNK, D_B), tok)] * 2 + [pl.BlockSpec((LANES, D_B), tok)] * 2,
        out_shape=[tok_out, tok_out, mat_out, mat_out],
        compiler_params=_cparams("parallel"),
        name="rw_chunk_rev" if reverse else "rw_chunk_fwd",
    )(r, k, v, kk, kka, lw)


def _rw_seq_body(rpf_ref, y0f_ref, gf_ref, hf_ref, rpb_ref, y0b_ref, gb_ref, hb_ref,
                 yf_out, yb_out, st_f, st_b, *, groups, n_chunks):
    i = pl.program_id(0)

    @pl.when(_seq_edge(i, groups, RW_CHUNK, last=False))
    def _():
        st_f[...] = jnp.zeros_like(st_f)

    @pl.when(_seq_edge(n_chunks - 1 - i, groups, RW_CHUNK, last=True))
    def _():
        st_b[...] = jnp.zeros_like(st_b)

    for rp_ref, y0_ref, g_ref, h_ref, y_out, st in ((rpf_ref, y0f_ref, gf_ref, hf_ref, yf_out, st_f),
                                                    (rpb_ref, y0b_ref, gb_ref, hb_ref, yb_out, st_b)):
        for pair in range(RW_PAIRS):
            sl = slice(pair * LANES, (pair + 1) * LANES)
            s0 = st[:, sl]
            y_out[:, sl] = _dot_hi(rp_ref[:, sl], s0) + y0_ref[:, sl]
            st[:, sl] = _dot_hi(g_ref[:, sl], s0) + h_ref[:, sl]


def _rw_seq(coef_f, coef_b, groups):
    rp_f, y0_f, g_f, h_f = coef_f
    rp_b, y0_b, g_b, h_b = coef_b
    t = rp_f.shape[0]
    n_chunks = t // RW_CHUNK
    fwd = lambda i: (i, 0)
    bwd = lambda i: (n_chunks - 1 - i, 0)
    tok = lambda m: pl.BlockSpec((RW_CHUNK, D_B), m)
    mat = lambda m: pl.BlockSpec((LANES, D_B), m)
    out = jax.ShapeDtypeStruct((t, D_B), f32)
    return pl.pallas_call(
        functools.partial(_rw_seq_body, groups=groups, n_chunks=n_chunks),
        grid=(n_chunks,),
        in_specs=[tok(fwd), tok(fwd), mat(fwd), mat(fwd), tok(bwd), tok(bwd), mat(bwd), mat(bwd)],
        out_specs=[tok(fwd), tok(bwd)],
        out_shape=[out, out],
        scratch_shapes=[pltpu.VMEM((LANES, D_B), f32), pltpu.VMEM((LANES, D_B), f32)],
        compiler_params=_cparams("arbitrary"),
        name="rw_seq",
    )(rp_f, y0_f, g_f, h_f, rp_b, y0_b, g_b, h_b)


def _rw_post_body(yf_ref, yb_ref, r_ref, k_ref, v_ref, g_ref, rk_ref, lnw_ref, lnb_ref, o_ref):
    ones = _head_ones()
    inv = 1.0 / RW_HEAD
    y = yf_ref[...] + yb_ref[...]
    mean = _dot_hi(y, ones) * inv
    yc = y - mean
    var = _dot_hi(yc * yc, ones) * inv
    yn = yc * lax.rsqrt(var + RW_GN_EPS) * lnw_ref[...] + lnb_ref[...]
    bonus = _dot_hi(r_ref[...] * k_ref[...] * rk_ref[...], ones) * v_ref[...]
    o_ref[...] = (yn + bonus) * g_ref[...]


def _rw_post(y_f, y_b, r, k, v, g, prm, j):
    t = y_f.shape[0]
    tok = pl.BlockSpec((TOKEN_TILE, D_B), lambda i: (i, 0))
    vec = _layer_spec((1, D_B))(j)
    return pl.pallas_call(
        _rw_post_body,
        grid=(t // TOKEN_TILE,),
        in_specs=[tok] * 6 + [vec] * 3,
        out_specs=tok,
        out_shape=jax.ShapeDtypeStruct((t, D_B), f32),
        compiler_params=_cparams("parallel"),
        name="rw_post",
    )(y_f, y_b, r, k, v, g, prm["r_k"], prm["ln_w"], prm["ln_b"])


def _rwkv7_mixer(proj, prm, j, groups):
    r, k, v, kk, kka, lw_f, lw_b, g = _rw_prep(proj, prm, j, groups)
    coef_f = _rw_chunk(r, k, v, kk, kka, lw_f, reverse=False)
    coef_b = _rw_chunk(r, k, v, kk, kka, lw_b, reverse=True)
    y_f, y_b = _rw_seq(coef_f, coef_b, groups)
    return _rw_post(y_f, y_b, r, k, v, g, prm, j)


def _halo_specs(tile, width, col_blk, n_rows):
    per = tile // SUBLANES
    last8 = n_rows // SUBLANES - 1
    prev = pl.BlockSpec((SUBLANES, width), lambda i: (jnp.maximum(i * per - 1, 0), col_blk))
    nxt = pl.BlockSpec((SUBLANES, width), lambda i: (jnp.minimum((i + 1) * per, last8), col_blk))
    return prev, nxt


def _shift_rows(x, halo, k, row):
    n = x.shape[0]
    if k > 0:
        y = pltpu.roll(x, k, 0)
        for r in range(k):
            y = jnp.where(row == r, halo[SUBLANES - k + r:SUBLANES - k + r + 1, :], y)
        return y
    y = pltpu.roll(x, n - 1, 0)
    return jnp.where(row == n - 1, halo[0:1, :], y)


LRU_TILE = 512
CD_XB_BLK = 3
CD_GB_BLK = 4


def _lru_gate_body(x_ref, xp_ref, xn_ref, cw_ref, cb_ref, lam_ref, w_ref, b_ref,
                   af_ref, bf_ref, ab_ref, bb_ref, *, groups):
    i = pl.program_id(0)
    first = _seq_edge(i, groups, LRU_TILE, last=False)
    last = _seq_edge(i, groups, LRU_TILE, last=True)
    x = x_ref[...]
    prev = jnp.where(first, 0.0, xp_ref[...])
    nxt = jnp.where(last, 0.0, xn_ref[...])
    row = lax.broadcasted_iota(jnp.int32, x.shape, 0)
    xc = (cw_ref[0:1, :] * _shift_rows(x, prev, 2, row) + cw_ref[1:2, :] * _shift_rows(x, prev, 1, row)
          + cw_ref[2:3, :] * x + cw_ref[3:4, :] * _shift_rows(x, nxt, -1, row) + cb_ref[...])
    pre = _dot(xc.astype(bf16), w_ref[...]) + b_ref[...]
    outs = ((af_ref, bf_ref), (ab_ref, bb_ref))
    for direction in range(2):
        base = 2 * direction * D_C
        gate_r = jax.nn.sigmoid(pre[:, base:base + D_C])
        gate_i = jax.nn.sigmoid(pre[:, base + D_C:base + 2 * D_C])
        log_a = -LRU_C * gate_r * jax.nn.softplus(-lam_ref[direction:direction + 1, :])
        t = jnp.tanh(log_a)
        mult = jnp.sqrt(-2.0 * t / (1.0 - t))
        a_ref, b_ref_out = outs[direction]
        a_ref[...] = jnp.exp(log_a)
        b_ref_out[...] = mult * gate_i * xc


def _lru_gates(proj, prm, j, groups):
    t = proj.shape[0]
    tok = lambda i: (i, 0)
    prev_spec, next_spec = _halo_specs(LRU_TILE, D_C, CD_XB_BLK, t)
    out = jax.ShapeDtypeStruct((t, D_C), f32)
    return pl.pallas_call(
        functools.partial(_lru_gate_body, groups=groups),
        grid=(t // LRU_TILE,),
        in_specs=[
            pl.BlockSpec((LRU_TILE, D_C), lambda i: (i, CD_XB_BLK)),
            prev_spec,
            next_spec,
            _layer_spec((LRU_CONV, D_C))(j),
            _layer_spec((1, D_C))(j),
            _layer_spec((2, D_C))(j),
            _layer_spec((D_C, 4 * D_C))(j),
            _layer_spec((1, 4 * D_C))(j),
        ],
        out_specs=[pl.BlockSpec((LRU_TILE, D_C), tok)] * 4,
        out_shape=[out] * 4,
        compiler_params=_cparams("parallel"),
        name="lru_gates",
    )(proj, proj, proj, prm["conv_w"], prm["conv_b"], prm["lam"], prm["w_gates"], prm["b_gates"])


def _lru_scan_body(af_ref, bf_ref, ab_ref, bb_ref, hf_ref, hb_ref, carry_f, carry_b, *, groups, n_blk):
    i = pl.program_id(0)

    @pl.when(_seq_edge(i, groups, LRU_TILE, last=False))
    def _():
        carry_f[...] = jnp.zeros_like(carry_f)

    @pl.when(_seq_edge(n_blk - 1 - i, groups, LRU_TILE, last=True))
    def _():
        carry_b[...] = jnp.zeros_like(carry_b)

    row = lax.broadcasted_iota(jnp.int32, (SUBLANES, D_C), 0)
    n_tiles = LRU_TILE // SUBLANES

    def run(a_ref, b_ref, h_ref, carry, reverse):
        def tile_step(k, c):
            t = (n_tiles - 1 - k) if reverse else k
            r0 = pl.multiple_of(t * SUBLANES, SUBLANES)
            a = a_ref[pl.ds(r0, SUBLANES), :]
            b = b_ref[pl.ds(r0, SUBLANES), :]
            for d in (1, 2, 4):
                sh = (SUBLANES - d) if reverse else d
                keep = (row < SUBLANES - d) if reverse else (row >= d)
                a_s = jnp.where(keep, pltpu.roll(a, sh, 0), 1.0)
                b_s = jnp.where(keep, pltpu.roll(b, sh, 0), 0.0)
                b = b + a * b_s
                a = a * a_s
            h = b + a * c
            h_ref[pl.ds(r0, SUBLANES), :] = h
            edge = 0 if reverse else SUBLANES - 1
            return jnp.broadcast_to(h[edge:edge + 1, :], (SUBLANES, D_C))

        carry[...] = lax.fori_loop(0, n_tiles, tile_step, carry[...])

    run(af_ref, bf_ref, hf_ref, carry_f, False)
    run(ab_ref, bb_ref, hb_ref, carry_b, True)


def _lru_scan(a_f, b_f, a_b, b_b, groups):
    t = a_f.shape[0]
    n_blk = t // LRU_TILE
    fwd = lambda i: (i, 0)
    bwd = lambda i: (n_blk - 1 - i, 0)
    out = jax.ShapeDtypeStruct((t, D_C), f32)
    return pl.pallas_call(
        functools.partial(_lru_scan_body, groups=groups, n_blk=n_blk),
        grid=(n_blk,),
        in_specs=[pl.BlockSpec((LRU_TILE, D_C), fwd), pl.BlockSpec((LRU_TILE, D_C), fwd),
                  pl.BlockSpec((LRU_TILE, D_C), bwd), pl.BlockSpec((LRU_TILE, D_C), bwd)],
        out_specs=[pl.BlockSpec((LRU_TILE, D_C), fwd), pl.BlockSpec((LRU_TILE, D_C), bwd)],
        out_shape=[out, out],
        scratch_shapes=[pltpu.VMEM((SUBLANES, D_C), f32), pltpu.VMEM((SUBLANES, D_C), f32)],
        compiler_params=_cparams("arbitrary"),
        name="lru_scan",
    )(a_f, b_f, a_b, b_b)


HY_N2 = 256
HY_CW = 128
HY_SC_TILE = 512
HY_MLP_TILE = 512
_HI = lax.Precision.HIGHEST


def _dot_hi(a, b):
    return jnp.dot(a, b, precision=_HI, preferred_element_type=f32)


def _hy_dims(seq_len):
    n = 2 * seq_len
    n1 = n // HY_N2
    lb = n1 // 2
    k1 = n1 // 2 + 1
    k1p = -(-k1 // SUBLANES) * SUBLANES
    return n, n1, lb, k1, k1p


def _hy_tables(seq_len):
    n, n1, lb, k1, k1p = _hy_dims(seq_len)
    two_pi = 2.0 * math.pi
    kk = np.arange(k1, dtype=np.float64)[:, None]
    nn = np.arange(lb, dtype=np.float64)[None, :]
    ang1 = two_pi * kk * nn / n1
    f1 = np.zeros((2 * k1p, lb), np.float64)
    f1[:k1] = np.cos(ang1)
    f1[k1p:k1p + k1] = -np.sin(ang1)
    weight = np.where((np.arange(k1) == 0) | (np.arange(k1) == n1 // 2), 1.0, 2.0)[None, :] / n
    f1inv = np.zeros((lb, 2 * k1p), np.float64)
    f1inv[:, :k1] = weight * np.cos(ang1.T)
    f1inv[:, k1p:k1p + k1] = -weight * np.sin(ang1.T)
    idx = np.arange(HY_N2, dtype=np.float64)
    ang2 = two_pi * np.outer(idx, idx) / HY_N2
    f2 = np.concatenate([np.cos(ang2), -np.sin(ang2)], axis=0)
    prod = (jnp.arange(k1, dtype=jnp.int32)[:, None] * jnp.arange(HY_N2, dtype=jnp.int32)[None, :]) % n
    ang = prod.astype(f32) * f32(two_pi / n)
    bc = lambda a: jnp.broadcast_to(a.reshape(k1 * HY_N2, 1), (k1 * HY_N2, HY_CW))
    return dict(f1=jnp.asarray(f1, f32), f1inv=jnp.asarray(f1inv, f32), f2=jnp.asarray(f2, f32),
                tw_re=bc(jnp.cos(ang)), tw_im=bc(-jnp.sin(ang)))


def _hy_forward(z_ref, f1_ref, f2_ref, twr_ref, twi_ref, wr, wi, dims):
    _, _, lb, k1, k1p = dims

    def stage1(n2, carry):
        rows = z_ref[pl.ds(n2, lb, stride=HY_N2), :]
        y = _dot_hi(f1_ref[...], rows)
        wr[pl.ds(n2, k1p, stride=HY_N2), :] = y[:k1p]
        wi[pl.ds(n2, k1p, stride=HY_N2), :] = y[k1p:]
        return carry

    lax.fori_loop(0, HY_N2, stage1, 0)

    def stage2(k, carry):
        r0 = pl.multiple_of(k * HY_N2, HY_N2)
        sl = pl.ds(r0, HY_N2)
        yr, yi, tr, ti = wr[sl, :], wi[sl, :], twr_ref[sl, :], twi_ref[sl, :]
        ar = yr * tr - yi * ti
        ai = yr * ti + yi * tr
        p = _dot_hi(f2_ref[...], ar)
        q = _dot_hi(f2_ref[...], ai)
        wr[sl, :] = p[:HY_N2] - q[HY_N2:]
        wi[sl, :] = p[HY_N2:] + q[:HY_N2]
        return carry

    lax.fori_loop(0, k1, stage2, 0)


def _hy_inverse(hr_ref, hi_ref, f1inv_ref, f2_ref, twr_ref, twi_ref, wr, wi, o_ref, dims):
    _, _, lb, k1, k1p = dims

    def stage2(k, carry):
        r0 = pl.multiple_of(k * HY_N2, HY_N2)
        sl = pl.ds(r0, HY_N2)
        xr, xi, hr, hi = wr[sl, :], wi[sl, :], hr_ref[sl, :], hi_ref[sl, :]
        zr = xr * hr - xi * hi
        zi = xr * hi + xi * hr
        p = _dot_hi(f2_ref[...], zr)
        q = _dot_hi(f2_ref[...], zi)
        vr = p[:HY_N2] + q[HY_N2:]
        vi = q[:HY_N2] - p[HY_N2:]
        tr, ti = twr_ref[sl, :], twi_ref[sl, :]
        wr[sl, :] = vr * tr + vi * ti
        wi[sl, :] = vi * tr - vr * ti
        return carry

    lax.fori_loop(0, k1, stage2, 0)

    def stage1(n2, carry):
        a = wr[pl.ds(n2, k1p, stride=HY_N2), :]
        b = wi[pl.ds(n2, k1p, stride=HY_N2), :]
        x = _dot_hi(f1inv_ref[:, :k1p], a) + _dot_hi(f1inv_ref[:, k1p:], b)
        o_ref[pl.ds(n2, lb, stride=HY_N2), :] = x
        return carry

    lax.fori_loop(0, HY_N2, stage1, 0)


def _hy_shortconv_body(x_ref, xp_ref, xn_ref, w_ref, b_ref, o_ref, *, groups):
    i = pl.program_id(0)
    x = x_ref[...]
    prev = jnp.where(_seq_edge(i, groups, HY_SC_TILE, last=False), 0.0, xp_ref[...])
    nxt = jnp.where(_seq_edge(i, groups, HY_SC_TILE, last=True), 0.0, xn_ref[...])
    row = lax.broadcasted_iota(jnp.int32, x.shape, 0)
    o_ref[...] = (w_ref[0:1, :] * _shift_rows(x, prev, 1, row) + w_ref[1:2, :] * x
                  + w_ref[2:3, :] * _shift_rows(x, nxt, -1, row) + b_ref[...])


def _hy_shortconv(proj, conv_w, conv_b, j, groups):
    t = proj.shape[0]
    width = 3 * D_D
    prev_spec, next_spec = _halo_specs(HY_SC_TILE, width, 0, t)
    return pl.pallas_call(
        functools.partial(_hy_shortconv_body, groups=groups),
        grid=(t // HY_SC_TILE,),
        in_specs=[pl.BlockSpec((HY_SC_TILE, width), lambda i: (i, 0)), prev_spec, next_spec,
                  _layer_spec((3, width))(j), _layer_spec((1, width))(j)],
        out_specs=pl.BlockSpec((HY_SC_TILE, width), lambda i: (i, 0)),
        out_shape=jax.ShapeDtypeStruct((t, width), f32),
        compiler_params=_cparams("parallel"),
        name="hy_shortconv",
    )(proj, proj, proj, conv_w, conv_b)


def _hy_mlp_body(z_ref, w1_ref, b1_ref, w2_ref, b2_ref, fr_ref, w3_ref, dl_ref, o_ref):
    z = z_ref[...]
    h = jnp.sin(fr_ref[0:1, :] * (_dot_hi(z, w1_ref[...]) + b1_ref[...]))
    h = jnp.sin(fr_ref[1:2, :] * (_dot_hi(h, w2_ref[...]) + b2_ref[...]))
    h = _dot_hi(h, w3_ref[...])
    o_ref[...] = h * jnp.exp(-z[:, 0:1] * dl_ref[...])


def _hy_mlp(seq_len, w1, b1, w2, b2, freq, w3):
    t = jnp.linspace(0.0, 1.0, seq_len, dtype=f32)[:, None]
    bands = (HY_POS_EMB - 1) // 2
    freqs = jnp.linspace(1e-4, bands - 1, bands, dtype=f32)[None, :]
    wpos = (2.0 * math.pi / seq_len) * jnp.arange(seq_len, dtype=f32)[:, None]
    z = jnp.concatenate([t, jnp.cos(freqs * wpos), -jnp.sin(freqs * wpos)], axis=-1)
    z = jnp.pad(z, ((0, 0), (0, LANES - HY_POS_EMB)))
    w1 = jnp.pad(w1, ((0, LANES - HY_POS_EMB), (0, 0)))
    max_decay = math.log(1e-2) / 0.3
    min_decay = math.log(1e-2) / 1.5
    deltas = jnp.abs(jnp.linspace(min_decay, max_decay, D_D, dtype=f32))
    deltas = jnp.tile(deltas, 4)[None, :]
    width = 4 * D_D
    hid = HY_FILTER_HIDDEN
    return pl.pallas_call(
        _hy_mlp_body,
        grid=(seq_len // HY_MLP_TILE,),
        in_specs=[pl.BlockSpec((HY_MLP_TILE, LANES), lambda i: (i, 0)),
                  _const_spec((LANES, hid)), _const_spec((1, hid)), _const_spec((hid, hid)),
                  _const_spec((1, hid)), _const_spec((2, hid)), _const_spec((hid, width)),
                  _const_spec((1, width))],
        out_specs=pl.BlockSpec((HY_MLP_TILE, width), lambda i: (i, 0)),
        out_shape=jax.ShapeDtypeStruct((seq_len, width), f32),
        compiler_params=_cparams("parallel"),
        name="hy_mlp",
    )(z, w1, b1[None, :], w2, b2[None, :], freq, w3, deltas)


def _hy_spectrum_body(hf_ref, hb_ref, f1_ref, f2_ref, twr_ref, twi_ref, or_ref, oi_ref,
                      zb, wr, wi, *, dims):
    k1 = dims[3]
    rows = k1 * HY_N2
    hf = hf_ref[...]
    hb = hb_ref[...]
    row = lax.broadcasted_iota(jnp.int32, hb.shape, 0)
    hb0 = jnp.where(row == 0, 0.0, hb)
    scale = 1.0 / (jnp.sum(jnp.abs(hf), axis=0, keepdims=True) + jnp.sum(jnp.abs(hb0), axis=0, keepdims=True))
    _hy_forward(hf_ref, f1_ref, f2_ref, twr_ref, twi_ref, wr, wi, dims)
    or_ref[...] = wr[:rows, :] * scale
    oi_ref[...] = wi[:rows, :] * scale
    zb[...] = hb0
    _hy_forward(zb, f1_ref, f2_ref, twr_ref, twi_ref, wr, wi, dims)
    or_ref[...] = or_ref[...] + wr[:rows, :] * scale
    oi_ref[...] = oi_ref[...] - wi[:rows, :] * scale


def _hy_spectrum(taps, tables, seq_len):
    dims = _hy_dims(seq_len)
    _, _, lb, k1, k1p = dims
    n_ct = D_D // HY_CW
    rows = k1 * HY_N2
    out = jax.ShapeDtypeStruct((2, rows, D_D), f32)
    return pl.pallas_call(
        functools.partial(_hy_spectrum_body, dims=dims),
        grid=(2, n_ct),
        in_specs=[pl.BlockSpec((seq_len, HY_CW), lambda o, c: (0, o * n_ct + c), pipeline_mode=pl.Buffered(1)),
                  pl.BlockSpec((seq_len, HY_CW), lambda o, c: (0, (2 + o) * n_ct + c),
                               pipeline_mode=pl.Buffered(1)),
                  _const_spec((2 * k1p, lb)), _const_spec((2 * HY_N2, HY_N2)),
                  _const_spec((rows, HY_CW)), _const_spec((rows, HY_CW))],
        out_specs=[pl.BlockSpec((None, rows, HY_CW), lambda o, c: (o, 0, c))] * 2,
        out_shape=[out, out],
        scratch_shapes=[pltpu.VMEM((seq_len, HY_CW), f32),
                        pltpu.VMEM((k1p * HY_N2, HY_CW), f32), pltpu.VMEM((k1p * HY_N2, HY_CW), f32)],
        compiler_params=_cparams("arbitrary", "arbitrary"),
        name="hy_spectrum",
    )(taps, taps, tables["f1"], tables["f2"], tables["tw_re"], tables["tw_im"])


def _hy_conv_body(z_ref, g_ref, hr_ref, hi_ref, bias_ref, f1_ref, f1inv_ref, f2_ref, twr_ref, twi_ref,
                  o_ref, wr, wi, *, dims):
    _hy_forward(z_ref, f1_ref, f2_ref, twr_ref, twi_ref, wr, wi, dims)
    _hy_inverse(hr_ref, hi_ref, f1inv_ref, f2_ref, twr_ref, twi_ref, wr, wi, o_ref, dims)
    bias = bias_ref[...]
    n_chunks = z_ref.shape[0] // HY_N2

    def gate(c, carry):
        sl = pl.ds(pl.multiple_of(c * HY_N2, HY_N2), HY_N2)
        o_ref[sl, :] = g_ref[sl, :] * (o_ref[sl, :] + bias * z_ref[sl, :])
        return carry

    lax.fori_loop(0, n_chunks, gate, 0)


def _hy_conv(z, z_off, z_col, gate, gate_off, gate_col, h_re, h_im, bias, tables, order, j, n_seq, seq_len):
    dims = _hy_dims(seq_len)
    _, _, lb, k1, k1p = dims
    n_ct = D_D // HY_CW
    rows = k1 * HY_N2
    z0, g0 = z_off // seq_len, gate_off // seq_len
    one = pl.Buffered(1)
    return pl.pallas_call(
        functools.partial(_hy_conv_body, dims=dims),
        grid=(n_ct, n_seq),
        in_specs=[
            pl.BlockSpec((seq_len, HY_CW), lambda c, b: (z0 + b, z_col * n_ct + c), pipeline_mode=one),
            pl.BlockSpec((seq_len, HY_CW), lambda c, b: (g0 + b, gate_col * n_ct + c), pipeline_mode=one),
            pl.BlockSpec((None, rows, HY_CW), lambda c, b: (order, 0, c), pipeline_mode=one),
            pl.BlockSpec((None, rows, HY_CW), lambda c, b: (order, 0, c), pipeline_mode=one),
            pl.BlockSpec((None, None, 1, HY_CW), lambda c, b: (j, order, 0, c)),
            _const_spec((2 * k1p, lb)), _const_spec((lb, 2 * k1p)), _const_spec((2 * HY_N2, HY_N2)),
            _const_spec((rows, HY_CW)), _const_spec((rows, HY_CW)),
        ],
        out_specs=pl.BlockSpec((seq_len, HY_CW), lambda c, b: (b, c)),
        out_shape=jax.ShapeDtypeStruct((n_seq * seq_len, D_D), f32),
        scratch_shapes=[pltpu.VMEM((k1p * HY_N2, HY_CW), f32), pltpu.VMEM((k1p * HY_N2, HY_CW), f32)],
        compiler_params=_cparams("arbitrary", "arbitrary"),
        name="hy_conv",
    )(z, gate, h_re, h_im, bias, tables["f1"], tables["f1inv"], tables["f2"], tables["tw_re"], tables["tw_im"])


def _hyena_mixer(proj, prm, j, groups):
    pc = _hy_shortconv(proj, prm["conv_w"], prm["conv_b"], j, groups)
    outs = []
    for tok_off, n_seq, seq_len in groups:
        tables = _hy_tables(seq_len)
        taps = _hy_mlp(seq_len, prm["f_w1"][j], prm["f_b1"][j], prm["f_w2"][j], prm["f_b2"][j],
                       prm["f_freq"][j], prm["f_w3"][j])
        h_re, h_im = _hy_spectrum(taps, tables, seq_len)
        z1 = _hy_conv(pc, tok_off, 0, pc, tok_off, 1, h_re, h_im, prm["bias"], tables, 0, j, n_seq, seq_len)
        y = _hy_conv(z1, 0, 0, pc, tok_off, 2, h_re, h_im, prm["bias"], tables, 1, j, n_seq, seq_len)
        outs.append(y)
    return jnp.concatenate(outs, axis=0)


def _outproj_odd_body(x_ref, hf_ref, hb_ref, gb_ref, yd_ref, w_ref, o_ref):
    y_c = (hf_ref[...] + hb_ref[...]) * jax.nn.gelu(gb_ref[...])
    acc = _dot(y_c.astype(bf16), w_ref[:D_C, :])
    acc = acc + _dot(yd_ref[...].astype(bf16), w_ref[D_C:, :])
    o_ref[...] = x_ref[...] + acc


def _outproj_odd(x, h_f, h_b, proj, y_d, w, j):
    t = x.shape[0]
    tok = lambda i: (i, 0)
    return pl.pallas_call(
        _outproj_odd_body,
        grid=(t // TOKEN_TILE,),
        in_specs=[
            pl.BlockSpec((TOKEN_TILE, D_MODEL), tok),
            pl.BlockSpec((TOKEN_TILE, D_C), tok),
            pl.BlockSpec((TOKEN_TILE, D_C), tok),
            pl.BlockSpec((TOKEN_TILE, D_C), lambda i: (i, CD_GB_BLK)),
            pl.BlockSpec((TOKEN_TILE, D_D), tok),
            _layer_spec((D_C + D_D, D_MODEL))(j),
        ],
        out_specs=pl.BlockSpec((TOKEN_TILE, D_MODEL), tok),
        out_shape=jax.ShapeDtypeStruct((t, D_MODEL), f32),
        compiler_params=_cparams("parallel"),
        name="outproj_odd",
    )(x, h_f, h_b, proj, y_d, w)


def _block_diag_heads(w):
    n, h, a, b = w.shape
    return jnp.einsum("nhij,hk->nhikj", w, jnp.eye(h, dtype=w.dtype)).reshape(n, h * a, h * b)


def _pad_rows(w, lo, total):
    return jnp.pad(w, ((0, 0), (lo, total - lo - w.shape[1]), (0, 0)))


def kernel(x_prompt, x_sample, ffn1_norm, ffn1_w_gate, ffn1_w_up, ffn1_w_down, mix_norm, ffn2_norm, ffn2_w_gate, ffn2_w_up, ffn2_w_down, ab_w_in, ab_w_out, s5_lambda_re, s5_lambda_im, s5_log_step, s5_b_re, s5_b_im, s5_c_re, s5_c_im, s5_d, s5_glu_w, s5_glu_b, rw_mu, rw_w0, rw_w_up, rw_a0, rw_a_up, rw_g_up, rw_k_k, rw_k_a, rw_r_k, rw_ln_w, rw_ln_b, cd_w_in, cd_w_out, lru_conv_w, lru_conv_b, lru_lambda, lru_wa, lru_ba, lru_wx, lru_bx, hy_conv_w, hy_conv_b, hy_f_w1, hy_f_b1, hy_f_w2, hy_f_b2, hy_f_freq, hy_f_w3, hy_bias, final_norm):
    n_p, l_p, _ = x_prompt.shape
    n_s, l_s, _ = x_sample.shape
    t_p = n_p * l_p
    groups = ((0, n_p, l_p), (t_p, n_s, l_s))
    x = jnp.concatenate([x_prompt.reshape(t_p, D_MODEL), x_sample.reshape(n_s * l_s, D_MODEL)], axis=0)

    row = lambda a: a[:, None, :]
    cast = lambda a: a.astype(bf16)
    ffn1 = (row(ffn1_norm), cast(ffn1_w_gate), cast(ffn1_w_up), cast(ffn1_w_down))
    ffn2 = (row(ffn2_norm), cast(ffn2_w_gate), cast(ffn2_w_up), cast(ffn2_w_down))
    mix_g = row(mix_norm)

    ab_in = cast(jnp.pad(ab_w_in, ((0, 0), (0, 0), (0, RW_P_WIDTH - ab_w_in.shape[-1]))))
    ab_out = cast(ab_w_out)
    s5 = dict(lam_re=s5_lambda_re, lam_im=s5_lambda_im, log_step=s5_log_step, b_re=s5_b_re, b_im=s5_b_im,
              c_re=s5_c_re, c_im=s5_c_im, d=row(s5_d), glu_w=cast(s5_glu_w), glu_b=row(s5_glu_b))
    lo = RW_DECAY_LORA
    lora_w = jnp.stack([_pad_rows(rw_w_up[:, 0], 0, RW_LORA_W), _pad_rows(rw_w_up[:, 1], lo, RW_LORA_W),
                        _pad_rows(rw_a_up, 2 * lo, RW_LORA_W),
                        _pad_rows(rw_g_up, 2 * lo + RW_A_LORA, RW_LORA_W)], axis=1)
    rw = dict(mu=row(jnp.pad(rw_mu, ((0, 0), (D_A, RW_P_WIDTH - D_A - rw_mu.shape[-1])))),
              lora_w=cast(lora_w), w0=rw_w0, a0=row(rw_a0), k_k=row(rw_k_k), k_a=row(rw_k_a),
              r_k=rw_r_k.reshape(-1, 1, D_B), ln_w=row(rw_ln_w), ln_b=row(rw_ln_b))

    cd_in = cast(jnp.concatenate([cd_w_in[..., 2 * D_C:], cd_w_in[..., :2 * D_C]], axis=-1))
    cd_out = cast(cd_w_out)
    w_gates = jnp.concatenate([_block_diag_heads(lru_wa[:, 0]), _block_diag_heads(lru_wx[:, 0]),
                               _block_diag_heads(lru_wa[:, 1]), _block_diag_heads(lru_wx[:, 1])], axis=-1)
    b_gates = jnp.concatenate([lru_ba[:, 0], lru_bx[:, 0], lru_ba[:, 1], lru_bx[:, 1]], axis=-1)
    lru = dict(conv_w=lru_conv_w, conv_b=row(lru_conv_b), lam=lru_lambda, w_gates=cast(w_gates),
               b_gates=row(b_gates))
    hy = dict(conv_w=hy_conv_w, conv_b=row(hy_conv_b), f_w1=hy_f_w1, f_b1=hy_f_b1, f_w2=hy_f_w2,
              f_b2=hy_f_b2, f_freq=hy_f_freq, f_w3=hy_f_w3, bias=hy_bias[:, :, None, :])

    for layer in range(DEPTH):
        j = layer // 2
        x = _ffn(x, *ffn1, layer)
        if layer % 2 == 0:
            proj = _proj(x, mix_g, ab_in, layer, j)
            y_a = _s5_mixer(proj, s5, j, groups)
            y_b = _rwkv7_mixer(proj, rw, j, groups)
            x = _outproj(x, y_a, y_b, ab_out, j)
        else:
            proj = _proj(x, mix_g, cd_in, layer, j)
            a_f, b_f, a_b, b_b = _lru_gates(proj, lru, j, groups)
            h_f, h_b = _lru_scan(a_f, b_f, a_b, b_b, groups)
            y_d = _hyena_mixer(proj, hy, j, groups)
            x = _outproj_odd(x, h_f, h_b, proj, y_d, cd_out, j)
        x = _ffn(x, *ffn2, layer)
    y = _final_norm(x, final_norm[None, :])
    return (y[:t_p].reshape(n_p, l_p, D_MODEL), y[t_p:].reshape(n_s, l_s, D_MODEL))
```

```python
import functools
import math

import numpy as np
import jax
import jax.numpy as jnp
from jax import lax
from jax.experimental import pallas as pl
from jax.experimental.pallas import tpu as pltpu

f32 = jnp.float32
bf16 = jnp.bfloat16

D_MODEL = 1024
DEPTH = 4
D_FF = 2816
RMS_EPS = 1e-6
D_A = 512
S5_GROUP = 16
S5_GROUPS = 32
S5_STATE = 64
D_B = 512
RW_HEAD = 64
RW_HEADS = 8
RW_DECAY_LORA = 32
RW_A_LORA = 32
RW_GATE_LORA = 64
RW_GN_EPS = 64e-5
D_C = 512
LRU_HEADS = 8
LRU_HEAD_DIM = 64
LRU_CONV = 4
LRU_C = 8.0
D_D = 512
HY_POS_EMB = 33
HY_FILTER_HIDDEN = 64

LANES = 128
SUBLANES = 8
VMEM_LIMIT = 56 * 1024 * 1024
TOKEN_TILE = 512
FF_CHUNK = 1408


def _cparams(*sem):
    return pltpu.CompilerParams(dimension_semantics=tuple(sem), vmem_limit_bytes=VMEM_LIMIT)


def _const_spec(shape):
    nd = len(shape)
    return pl.BlockSpec(shape, lambda *_: (0,) * nd, pipeline_mode=pl.Buffered(1))


def _layer_spec(shape):
    def make(layer):
        nd = len(shape)
        return pl.BlockSpec((None,) + tuple(shape), lambda *_: (layer,) + (0,) * nd,
                            pipeline_mode=pl.Buffered(1))
    return make


def _rms(x, g):
    ms = jnp.mean(x * x, axis=-1, keepdims=True)
    return x * lax.rsqrt(ms + RMS_EPS) * g


def _dot(a, b):
    return jnp.dot(a, b, preferred_element_type=f32)


_HI = lax.Precision.HIGHEST
_NN = (((1,), (0,)), ((), ()))
_NT = (((1,), (1,)), ((), ()))
_TN = (((0,), (0,)), ((), ()))


def _mm(a, b, prec, dims=_NN):
    dg = functools.partial(lax.dot_general, dimension_numbers=dims, preferred_element_type=f32)
    if prec == "hi":
        return dg(a, b, precision=_HI)
    a1, b1 = a.astype(bf16), b.astype(bf16)
    if prec == "b1":
        return dg(a1, b1)
    a2 = (a - a1.astype(f32)).astype(bf16)
    if prec == "x2":
        return dg(a1, b1) + dg(a2, b1)
    b2 = (b - b1.astype(f32)).astype(bf16)
    return dg(a1, b1) + (dg(a1, b2) + dg(a2, b1))


def _dot_hi(a, b):
    return _mm(a, b, "hi")


def _ffn_body(x_ref, g_ref, wg_ref, wu_ref, wd_ref, o_ref):
    x = x_ref[...]
    h = _rms(x, g_ref[...]).astype(bf16)
    acc = None
    for c in range(D_FF // FF_CHUNK):
        sl = slice(c * FF_CHUNK, (c + 1) * FF_CHUNK)
        gate = _dot(h, wg_ref[:, sl])
        up = _dot(h, wu_ref[:, sl])
        act = (gate * jax.nn.sigmoid(gate) * up).astype(bf16)
        part = _dot(act, wd_ref[sl, :])
        acc = part if acc is None else acc + part
    o_ref[...] = x + 0.5 * acc


def _ffn(x, norm, wg, wu, wd, layer):
    t = x.shape[0]
    return pl.pallas_call(
        _ffn_body,
        grid=(t // TOKEN_TILE,),
        in_specs=[
            pl.BlockSpec((TOKEN_TILE, D_MODEL), lambda i: (i, 0)),
            _layer_spec((1, D_MODEL))(layer),
            _layer_spec((D_MODEL, D_FF))(layer),
            _layer_spec((D_MODEL, D_FF))(layer),
            _layer_spec((D_FF, D_MODEL))(layer),
        ],
        out_specs=pl.BlockSpec((TOKEN_TILE, D_MODEL), lambda i: (i, 0)),
        out_shape=jax.ShapeDtypeStruct((t, D_MODEL), f32),
        compiler_params=_cparams("parallel"),
        name="ffn",
    )(x, norm, wg, wu, wd)


def _proj_body(x_ref, g_ref, w_ref, o_ref):
    h = _rms(x_ref[...], g_ref[...]).astype(bf16)
    o_ref[...] = _dot(h, w_ref[...])


def _proj(x, norm, w, layer, j):
    t = x.shape[0]
    p = w.shape[-1]
    return pl.pallas_call(
        _proj_body,
        grid=(t // TOKEN_TILE,),
        in_specs=[
            pl.BlockSpec((TOKEN_TILE, D_MODEL), lambda i: (i, 0)),
            _layer_spec((1, D_MODEL))(layer),
            _layer_spec((D_MODEL, p))(j),
        ],
        out_specs=pl.BlockSpec((TOKEN_TILE, p), lambda i: (i, 0)),
        out_shape=jax.ShapeDtypeStruct((t, p), f32),
        compiler_params=_cparams("parallel"),
        name="proj",
    )(x, norm, w)


def _outproj_body(x_ref, ya_ref, yb_ref, w_ref, o_ref):
    half = w_ref.shape[0] // 2
    acc = _dot(ya_ref[...].astype(bf16), w_ref[:half, :])
    acc = acc + _dot(yb_ref[...].astype(bf16), w_ref[half:, :])
    o_ref[...] = x_ref[...] + acc


def _outproj(x, ya, yb, w, j):
    t = x.shape[0]
    half = ya.shape[-1]
    return pl.pallas_call(
        _outproj_body,
        grid=(t // TOKEN_TILE,),
        in_specs=[
            pl.BlockSpec((TOKEN_TILE, D_MODEL), lambda i: (i, 0)),
            pl.BlockSpec((TOKEN_TILE, half), lambda i: (i, 0)),
            pl.BlockSpec((TOKEN_TILE, half), lambda i: (i, 0)),
            _layer_spec((2 * half, D_MODEL))(j),
        ],
        out_specs=pl.BlockSpec((TOKEN_TILE, D_MODEL), lambda i: (i, 0)),
        out_shape=jax.ShapeDtypeStruct((t, D_MODEL), f32),
        compiler_params=_cparams("parallel"),
        name="outproj",
    )(x, ya, yb, w)


def _final_norm_body(x_ref, g_ref, o_ref):
    o_ref[...] = _rms(x_ref[...], g_ref[...])


def _final_norm(x, g):
    t = x.shape[0]
    return pl.pallas_call(
        _final_norm_body,
        grid=(t // TOKEN_TILE,),
        in_specs=[pl.BlockSpec((TOKEN_TILE, D_MODEL), lambda i: (i, 0)), _const_spec((1, D_MODEL))],
        out_specs=pl.BlockSpec((TOKEN_TILE, D_MODEL), lambda i: (i, 0)),
        out_shape=jax.ShapeDtypeStruct((t, D_MODEL), f32),
        compiler_params=_cparams("parallel"),
        name="final_norm",
    )(x, g)


def _seq_edge(blk, groups, tile, last):
    hit = None
    for off, n_seq, seq_len in groups:
        start, per, n = off // tile, seq_len // tile, n_seq * (seq_len // tile)
        rel = blk - start
        edge = (per - 1) if last else 0
        h = (rel >= 0) & (rel < n) & (lax.rem(jnp.maximum(rel, 0), per) == edge)
        hit = h if hit is None else (hit | h)
    return hit


S5_TILE = 512
S5_OCT = 4
S5_OCT_STATES = 512


def _s5_param_body(lr_ref, li_ref, st_ref, lr2_ref, li2_ref, st2_ref, br_ref, bi_ref,
                   pwr_ref, pwi_ref, bbr_ref, bbi_ref):
    def discretize(lam_re, lam_im, log_step):
        lam_re = jnp.minimum(lam_re, -1e-4)
        step = jnp.exp(log_step)
        mag = jnp.exp(lam_re * step)
        ang = lam_im * step
        return lam_re, lam_im, mag * jnp.cos(ang), mag * jnp.sin(ang)

    _, _, ar, ai = discretize(lr_ref[...], li_ref[...], st_ref[...])
    pr, pi = ar, ai
    for j in range(SUBLANES):
        pwr_ref[j] = pr
        pwi_ref[j] = pi
        pr, pi = pr * ar - pi * ai, pr * ai + pi * ar
    lam_re, lam_im, ar, ai = discretize(lr2_ref[...], li2_ref[...], st2_ref[...])
    den = lam_re * lam_re + lam_im * lam_im
    num_re = ar - 1.0
    coef_re = (num_re * lam_re + ai * lam_im) / den
    coef_im = (ai * lam_re - num_re * lam_im) / den
    b_re, b_im = br_ref[...], bi_ref[...]
    bbr_ref[...] = coef_re * b_re - coef_im * b_im
    bbi_ref[...] = coef_re * b_im + coef_im * b_re


def _s5_params(lam_re, lam_im, log_step, b_re, b_im):
    g2 = 2 * S5_GROUPS
    rows = g2 * S5_GROUP
    lr = lam_re.reshape(g2, S5_STATE)
    li = lam_im.reshape(g2, S5_STATE)
    st = jnp.broadcast_to(log_step.reshape(g2, 1), (g2, S5_STATE))
    rep = lambda a: jnp.broadcast_to(a[:, None, :], (g2, S5_GROUP, S5_STATE)).reshape(rows, S5_STATE)
    bt = lambda b: jnp.swapaxes(b, -1, -2).reshape(rows, S5_STATE)
    full = lambda s: pl.BlockSpec(s, lambda: (0,) * len(s))
    pwr, pwi, bbr, bbi = pl.pallas_call(
        _s5_param_body,
        in_specs=[full((g2, S5_STATE))] * 3 + [full((rows, S5_STATE))] * 5,
        out_specs=[full((SUBLANES, g2, S5_STATE))] * 2 + [full((rows, S5_STATE))] * 2,
        out_shape=[jax.ShapeDtypeStruct((SUBLANES, g2, S5_STATE), f32)] * 2
        + [jax.ShapeDtypeStruct((rows, S5_STATE), f32)] * 2,
        name="s5_params",
    )(lr, li, st, rep(lr), rep(li), rep(st), bt(b_re), bt(b_im))
    n_all = S5_GROUPS * S5_STATE
    pw = lambda p: p.reshape(SUBLANES, 2, n_all)
    bb = lambda b: b.reshape(2, S5_GROUPS, S5_GROUP, S5_STATE)
    return pw(pwr), pw(pwi), bb(bbr), bb(bbi)


def _s5_block_mats(bb_re, bb_im, c_re, c_im):
    eye = jnp.eye(8, dtype=f32)

    def in_mat(b):
        b4 = b.reshape(S5_OCT, 8, S5_GROUP, S5_STATE)
        return jnp.einsum("qgcn,gh->qgchn", b4, eye).reshape(S5_OCT, 8 * S5_GROUP, 8 * S5_STATE)

    def out_mat(c):
        c4 = c.reshape(S5_OCT, 8, S5_GROUP, S5_STATE)
        return jnp.einsum("qgcn,gh->qgnhc", c4, eye).reshape(S5_OCT, 8 * S5_STATE, 8 * S5_GROUP)

    w_in = jnp.concatenate([in_mat(bb_re), in_mat(bb_im)], axis=-1).astype(bf16)
    return w_in, out_mat(c_re).astype(bf16), out_mat(c_im).astype(bf16)


def _s5_scan_body(u_ref, win_ref, wcr_ref, wci_ref, pwr_ref, pwi_ref, o_ref,
                  x_buf, h_buf, carry, *, groups, reverse, n_blk):
    pid = pl.program_id(0)
    blk = (n_blk - 1 - pid) if reverse else pid

    @pl.when(_seq_edge(blk, groups, S5_TILE, last=reverse))
    def _():
        carry[...] = jnp.zeros_like(carry)

    row = lax.broadcasted_iota(jnp.int32, (SUBLANES, S5_OCT_STATES), 0)
    n_tiles = S5_TILE // SUBLANES
    for q in range(S5_OCT):
        lanes = slice(q * S5_OCT_STATES, (q + 1) * S5_OCT_STATES)
        x_buf[...] = _dot(u_ref[:, q * LANES:(q + 1) * LANES].astype(bf16), win_ref[q])

        def power(j):
            shape = (SUBLANES, S5_OCT_STATES)
            return (jnp.broadcast_to(pwr_ref[j:j + 1, lanes], shape),
                    jnp.broadcast_to(pwi_ref[j:j + 1, lanes], shape))

        levels = []
        for d in (1, 2, 4):
            ar, ai = power(d - 1)
            keep = (row < SUBLANES - d) if reverse else (row >= d)
            levels.append((d, jnp.where(keep, ar, 0.0), jnp.where(keep, ai, 0.0)))
        cr = jnp.zeros((SUBLANES, S5_OCT_STATES), f32)
        ci = jnp.zeros((SUBLANES, S5_OCT_STATES), f32)
        for j in range(SUBLANES):
            pr, pi = power(j)
            sel = (row == (SUBLANES - 1 - j)) if reverse else (row == j)
            cr = jnp.where(sel, pr, cr)
            ci = jnp.where(sel, pi, ci)

        def tile_step(i, c):
            c_re, c_im = c
            t = (n_tiles - 1 - i) if reverse else i
            r0 = pl.multiple_of(t * SUBLANES, SUBLANES)
            h_re = x_buf[pl.ds(r0, SUBLANES), :S5_OCT_STATES]
            h_im = x_buf[pl.ds(r0, SUBLANES), S5_OCT_STATES:]
            for d, ar, ai in levels:
                sh = (SUBLANES - d) if reverse else d
                s_re = pltpu.roll(h_re, sh, 0)
                s_im = pltpu.roll(h_im, sh, 0)
                h_re, h_im = h_re + (ar * s_re - ai * s_im), h_im + (ar * s_im + ai * s_re)
            h_re, h_im = h_re + (cr * c_re - ci * c_im), h_im + (cr * c_im + ci * c_re)
            h_buf[pl.ds(r0, SUBLANES), :S5_OCT_STATES] = h_re
            h_buf[pl.ds(r0, SUBLANES), S5_OCT_STATES:] = h_im
            edge = 0 if reverse else SUBLANES - 1
            shape = (SUBLANES, S5_OCT_STATES)
            return (jnp.broadcast_to(h_re[edge:edge + 1, :], shape),
                    jnp.broadcast_to(h_im[edge:edge + 1, :], shape))

        c0 = (carry[:, lanes], carry[:, S5_OCT * S5_OCT_STATES + q * S5_OCT_STATES:
                                     S5_OCT * S5_OCT_STATES + (q + 1) * S5_OCT_STATES])
        c_re, c_im = lax.fori_loop(0, n_tiles, tile_step, c0)
        carry[:, lanes] = c_re
        carry[:, S5_OCT * S5_OCT_STATES + q * S5_OCT_STATES:
              S5_OCT * S5_OCT_STATES + (q + 1) * S5_OCT_STATES] = c_im
        y = _dot(h_buf[:, :S5_OCT_STATES].astype(bf16), wcr_ref[q])
        y = y - _dot(h_buf[:, S5_OCT_STATES:].astype(bf16), wci_ref[q])
        o_ref[:, q * LANES:(q + 1) * LANES] = y


def _s5_scan(proj, w_in, w_cre, w_cim, pw_re, pw_im, groups, reverse):
    t = proj.shape[0]
    n_blk = t // S5_TILE
    n_state = S5_OCT * S5_OCT_STATES
    order = (lambda i: (n_blk - 1 - i, 0)) if reverse else (lambda i: (i, 0))
    body = functools.partial(_s5_scan_body, groups=groups, reverse=reverse, n_blk=n_blk)
    return pl.pallas_call(
        body,
        grid=(n_blk,),
        in_specs=[
            pl.BlockSpec((S5_TILE, D_A), order),
            _const_spec((S5_OCT, LANES, 2 * S5_OCT_STATES)),
            _const_spec((S5_OCT, S5_OCT_STATES, LANES)),
            _const_spec((S5_OCT, S5_OCT_STATES, LANES)),
            _const_spec((SUBLANES, n_state)),
            _const_spec((SUBLANES, n_state)),
        ],
        out_specs=pl.BlockSpec((S5_TILE, D_A), order),
        out_shape=jax.ShapeDtypeStruct((t, D_A), f32),
        scratch_shapes=[
            pltpu.VMEM((S5_TILE, 2 * S5_OCT_STATES), f32),
            pltpu.VMEM((S5_TILE, 2 * S5_OCT_STATES), f32),
            pltpu.VMEM((SUBLANES, 2 * n_state), f32),
        ],
        compiler_params=_cparams("arbitrary"),
        name="s5_scan_rev" if reverse else "s5_scan_fwd",
    )(proj, w_in, w_cre, w_cim, pw_re, pw_im)


def _s5_post_body(u_ref, yf_ref, yb_ref, d_ref, w_ref, b_ref, o_ref):
    y = yf_ref[...] + yb_ref[...] + d_ref[...] * u_ref[...]
    y = jax.nn.gelu(y)
    gate = _dot(y.astype(bf16), w_ref[...]) + b_ref[...]
    o_ref[...] = y * jax.nn.sigmoid(gate)


def _s5_post(proj, y_f, y_b, d, glu_w, glu_b, j):
    t = proj.shape[0]
    tok = lambda i: (i, 0)
    return pl.pallas_call(
        _s5_post_body,
        grid=(t // TOKEN_TILE,),
        in_specs=[
            pl.BlockSpec((TOKEN_TILE, D_A), tok),
            pl.BlockSpec((TOKEN_TILE, D_A), tok),
            pl.BlockSpec((TOKEN_TILE, D_A), tok),
            _layer_spec((1, D_A))(j),
            _layer_spec((D_A, D_A))(j),
            _layer_spec((1, D_A))(j),
        ],
        out_specs=pl.BlockSpec((TOKEN_TILE, D_A), tok),
        out_shape=jax.ShapeDtypeStruct((t, D_A), f32),
        compiler_params=_cparams("parallel"),
        name="s5_post",
    )(proj, y_f, y_b, d, glu_w, glu_b)


def _s5_mixer(proj, prm, j, groups):
    pw_re, pw_im, bb_re, bb_im = _s5_params(prm["lam_re"][j], prm["lam_im"][j], prm["log_step"][j],
                                            prm["b_re"][j], prm["b_im"][j])
    ys = []
    for direction in range(2):
        w_in, w_cre, w_cim = _s5_block_mats(bb_re[direction], bb_im[direction],
                                            prm["c_re"][j, direction], prm["c_im"][j, direction])
        ys.append(_s5_scan(proj, w_in, w_cre, w_cim, pw_re[:, direction], pw_im[:, direction],
                           groups, reverse=(direction == 1)))
    return _s5_post(proj, ys[0], ys[1], prm["d"], prm["glu_w"], prm["glu_b"], j)


RW_TILE = 256
RW_CHUNK = 64
RW_PAIRS = RW_HEADS // 2
RW_CHUNKS_PER_STEP = 2
RW_PREC_A = "b1"
RW_PREC_INV = "b1"
RW_PREC_APPLY = "b1"
RW_PREC_SEQ = "b1"
RW_PREC_SUM = "x2"
RW_P_WIDTH = 2304
RW_LORA_OFF = 2048
RW_LORA_W = 256


def _head_ones():
    a = lax.broadcasted_iota(jnp.int32, (D_B, D_B), 0) // RW_HEAD
    b = lax.broadcasted_iota(jnp.int32, (D_B, D_B), 1) // RW_HEAD
    return jnp.where(a == b, 1.0, 0.0).astype(f32)


def _rw_prep_body(x_ref, xp_ref, xn_ref, mu_ref, lw_ref, w0_ref, a0_ref, kk_ref, ka_ref,
                  r_out, k_out, v_out, kk_out, kka_out, lwf_out, lwb_out, g_out, *, groups):
    i = pl.program_id(0)
    x = x_ref[...]
    prev = jnp.where(_seq_edge(i, groups, RW_TILE, last=False), 0.0, xp_ref[...])
    nxt = jnp.where(_seq_edge(i, groups, RW_TILE, last=True), 0.0, xn_ref[...])
    row = lax.broadcasted_iota(jnp.int32, x.shape, 0)
    shifted = 0.5 * (_shift_rows(x, prev, 1, row) + _shift_rows(x, nxt, -1, row))
    p = x + (shifted - x) * mu_ref[...]
    r = p[:, D_A:D_A + D_B]
    k = p[:, D_A + D_B:D_A + 2 * D_B]
    v = p[:, D_A + 2 * D_B:D_A + 3 * D_B]
    lora = p[:, RW_LORA_OFF:RW_LORA_OFF + RW_LORA_W]
    lora_t = jnp.tanh(lora).astype(bf16)
    a = jax.nn.sigmoid(a0_ref[...] + _dot(lora.astype(bf16), lw_ref[2]))
    g = _dot(jax.nn.sigmoid(lora).astype(bf16), lw_ref[3])
    kx = k * kk_ref[...]
    ss = _mm(kx * kx, _head_ones(), RW_PREC_SUM)
    kk = kx / jnp.maximum(jnp.sqrt(ss), 1e-12)
    r_out[...] = r
    k_out[...] = k * (1.0 + (a - 1.0) * ka_ref[...])
    v_out[...] = v
    kk_out[...] = kk
    kka_out[...] = kk * a
    g_out[...] = g
    for direction, out in enumerate((lwf_out, lwb_out)):
        wl = -jax.nn.softplus(-(w0_ref[direction:direction + 1, :] + _dot(lora_t, lw_ref[direction]))) - 0.5
        out[...] = -jnp.exp(wl)


def _rw_prep(proj, prm, j, groups):
    t = proj.shape[0]
    tok = lambda i: (i, 0)
    prev_spec, next_spec = _halo_specs(RW_TILE, RW_P_WIDTH, 0, t)
    out = jax.ShapeDtypeStruct((t, D_B), f32)
    return pl.pallas_call(
        functools.partial(_rw_prep_body, groups=groups),
        grid=(t // RW_TILE,),
        in_specs=[pl.BlockSpec((RW_TILE, RW_P_WIDTH), tok), prev_spec, next_spec,
                  _layer_spec((1, RW_P_WIDTH))(j), _layer_spec((4, RW_LORA_W, D_B))(j),
                  _layer_spec((2, D_B))(j), _layer_spec((1, D_B))(j), _layer_spec((1, D_B))(j),
                  _layer_spec((1, D_B))(j)],
        out_specs=[pl.BlockSpec((RW_TILE, D_B), tok)] * 8,
        out_shape=[out] * 8,
        compiler_params=_cparams("parallel"),
        name="rw_prep",
    )(proj, proj, proj, prm["mu"], prm["lora_w"], prm["w0"], prm["a0"], prm["k_k"], prm["k_a"])


def _rw_chunk_body(r_ref, k_ref, v_ref, kk_ref, kka_ref, lw_ref, rp_out, y0_out, g_out, h_out, *, reverse):
    c = RW_CHUNK
    ri = lax.broadcasted_iota(jnp.int32, (c, c), 0)
    ci = lax.broadcasted_iota(jnp.int32, (c, c), 1)
    tri = jnp.where((ci >= ri) if reverse else (ci <= ri), 1.0, 0.0).astype(f32)
    edge = 0 if reverse else c - 1
    big_r = lax.broadcasted_iota(jnp.int32, (LANES, LANES), 0)
    big_c = lax.broadcasted_iota(jnp.int32, (LANES, LANES), 1)
    same_head = (big_r // RW_HEAD) == (big_c // RW_HEAD)
    s_idx, j_idx = big_r % RW_HEAD, big_c % RW_HEAD
    strict = same_head & ((j_idx > s_idx) if reverse else (j_idx < s_idx))
    incl = same_head & ((j_idx >= s_idx) if reverse else (j_idx <= s_idx))
    eye = big_r == big_c
    eye_f = jnp.where(eye, 1.0, 0.0).astype(f32)
    head0 = lax.broadcasted_iota(jnp.int32, (c, LANES), 1) < RW_HEAD

    def expand(x):
        return jnp.concatenate([jnp.where(head0, x, 0.0), jnp.where(head0, 0.0, x)], axis=0)

    def collapse(x):
        return x[:c] + x[c:]

    n_sub = r_ref.shape[0] // c
    lanes = [slice(p * LANES, (p + 1) * LANES) for p in range(RW_PAIRS)]
    rows = [slice(ch * c, (ch + 1) * c) for ch in range(n_sub)]
    units = [(ch, p) for ch in range(n_sub) for p in range(RW_PAIRS)]
    pre = []
    for rs in rows:
        lw = lw_ref[rs, :]
        cum = _dot_hi(tri, lw)
        total = cum[edge:edge + 1, :]
        e_neg, tail = jnp.exp(-cum), jnp.exp(total - cum)
        k, kka = k_ref[rs, :], kka_ref[rs, :]
        pre.append(dict(at=-kk_ref[rs, :] * jnp.exp(cum - lw), rt=r_ref[rs, :] * jnp.exp(cum),
                        kh=k * e_neg, bh=kka * e_neg, kp=k * tail, bp=kka * tail, v=v_ref[rs, :],
                        pc=jnp.exp(total)))

    def get(name):
        return [pre[ch][name][:, lanes[p]] for ch, p in units]

    at, rt, kh, bh, kp, bp, v = (get(n) for n in ("at", "rt", "kh", "bh", "kp", "bp", "v"))
    at_exp = [expand(x) for x in at]
    v_exp = [expand(x) for x in v]
    z = [_mm(jnp.concatenate([ae, expand(rr)], axis=0), jnp.concatenate([a, a, b, b], axis=0), RW_PREC_A, _NT)
         for ae, rr, a, b in zip(at_exp, rt, kh, bh)]
    a_ak = [jnp.where(strict, zz[:LANES, :LANES], 0.0) for zz in z]
    a_ab = [jnp.where(strict, zz[:LANES, LANES:], 0.0) for zz in z]
    a_rk = [jnp.where(incl, zz[LANES:, :LANES], 0.0) for zz in z]
    a_rb = [jnp.where(incl, zz[LANES:, LANES:], 0.0) for zz in z]
    tinv = [eye_f + a for a in a_ab]
    pw = a_ab
    for _ in range(int(math.log2(c)) - 1):
        pw = [_mm(q, q, RW_PREC_INV) for q in pw]
        tinv = [_mm(t, eye_f + q, RW_PREC_INV) for t, q in zip(tinv, pw)]
    w = [_mm(a, ve, RW_PREC_APPLY) for a, ve in zip(a_ak, v_exp)]
    tu = [_mm(t, jnp.concatenate([ae, ww], axis=1), RW_PREC_APPLY) for t, ae, ww in zip(tinv, at_exp, w)]
    atp_exp = [x[:, :LANES] for x in tu]
    u0_exp = [x[:, LANES:] for x in tu]
    y0 = [collapse(_mm(ark, ve, RW_PREC_APPLY) + _mm(arb, ue, RW_PREC_APPLY))
          for ark, arb, ve, ue in zip(a_rk, a_rb, v_exp, u0_exp)]
    rp = [rr + collapse(_mm(arb, ae, RW_PREC_APPLY)) for rr, arb, ae in zip(rt, a_rb, atp_exp)]
    atp = [collapse(x) for x in atp_exp]
    u0 = [collapse(x) for x in u0_exp]
    g_full = [_mm(b, a, RW_PREC_APPLY, _TN) for b, a in zip(bp, atp)]
    h_full = [_mm(kd, vv, RW_PREC_APPLY, _TN) + _mm(b, u, RW_PREC_APPLY, _TN)
              for kd, vv, b, u in zip(kp, v, bp, u0)]
    for i, (ch, p) in enumerate(units):
        sl = lanes[p]
        mat_rows = slice(ch * LANES, (ch + 1) * LANES)
        pc = jnp.broadcast_to(pre[ch]["pc"][:, sl], (LANES, LANES))
        rp_out[rows[ch], sl] = rp[i]
        y0_out[rows[ch], sl] = y0[i]
        g_out[mat_rows, sl] = jnp.where(same_head, g_full[i], 0.0) + jnp.where(eye, pc, 0.0)
        h_out[mat_rows, sl] = jnp.where(same_head, h_full[i], 0.0)


def _rw_chunk(r, k, v, kk, kka, lw, reverse):
    t = r.shape[0]
    n_chunks = t // RW_CHUNK
    step_rows = RW_CHUNK * RW_CHUNKS_PER_STEP
    tok = lambda i: (i, 0)
    tok_out = jax.ShapeDtypeStruct((t, D_B), f32)
    mat_out = jax.ShapeDtypeStruct((n_chunks * LANES, D_B), f32)
    return pl.pallas_call(
        functools.partial(_rw_chunk_body, reverse=reverse),
        grid=(t // step_rows,),
        in_specs=[pl.BlockSpec((step_rows, D_B), tok)] * 6,
        out_specs=[pl.BlockSpec((step_rows, D_B), tok)] * 2
        + [pl.BlockSpec((LANES * RW_CHUNKS_PER_STEP, D_B), tok)] * 2,
        out_shape=[tok_out, tok_out, mat_out, mat_out],
        compiler_params=_cparams("parallel"),
        name="rw_chunk_rev" if reverse else "rw_chunk_fwd",
    )(r, k, v, kk, kka, lw)


def _rw_seq_body(rpf_ref, y0f_ref, gf_ref, hf_ref, rpb_ref, y0b_ref, gb_ref, hb_ref,
                 yf_out, yb_out, st_f, st_b, *, groups, n_chunks):
    i = pl.program_id(0)

    @pl.when(_seq_edge(i, groups, RW_CHUNK, last=False))
    def _():
        st_f[...] = jnp.zeros_like(st_f)

    @pl.when(_seq_edge(n_chunks - 1 - i, groups, RW_CHUNK, last=True))
    def _():
        st_b[...] = jnp.zeros_like(st_b)

    for rp_ref, y0_ref, g_ref, h_ref, y_out, st in ((rpf_ref, y0f_ref, gf_ref, hf_ref, yf_out, st_f),
                                                    (rpb_ref, y0b_ref, gb_ref, hb_ref, yb_out, st_b)):
        for pair in range(RW_PAIRS):
            sl = slice(pair * LANES, (pair + 1) * LANES)
            s0 = st[:, sl]
            y_out[:, sl] = _mm(rp_ref[:, sl], s0, RW_PREC_SEQ) + y0_ref[:, sl]
            st[:, sl] = _mm(g_ref[:, sl], s0, RW_PREC_SEQ) + h_ref[:, sl]


def _rw_seq(coef_f, coef_b, groups):
    rp_f, y0_f, g_f, h_f = coef_f
    rp_b, y0_b, g_b, h_b = coef_b
    t = rp_f.shape[0]
    n_chunks = t // RW_CHUNK
    fwd = lambda i: (i, 0)
    bwd = lambda i: (n_chunks - 1 - i, 0)
    tok = lambda m: pl.BlockSpec((RW_CHUNK, D_B), m)
    mat = lambda m: pl.BlockSpec((LANES, D_B), m)
    out = jax.ShapeDtypeStruct((t, D_B), f32)
    return pl.pallas_call(
        functools.partial(_rw_seq_body, groups=groups, n_chunks=n_chunks),
        grid=(n_chunks,),
        in_specs=[tok(fwd), tok(fwd), mat(fwd), mat(fwd), tok(bwd), tok(bwd), mat(bwd), mat(bwd)],
        out_specs=[tok(fwd), tok(bwd)],
        out_shape=[out, out],
        scratch_shapes=[pltpu.VMEM((LANES, D_B), f32), pltpu.VMEM((LANES, D_B), f32)],
        compiler_params=_cparams("arbitrary"),
        name="rw_seq",
    )(rp_f, y0_f, g_f, h_f, rp_b, y0_b, g_b, h_b)


def _rw_post_body(yf_ref, yb_ref, r_ref, k_ref, v_ref, g_ref, rk_ref, lnw_ref, lnb_ref, o_ref):
    ones = _head_ones()
    inv = 1.0 / RW_HEAD
    y = yf_ref[...] + yb_ref[...]
    mean = _mm(y, ones, RW_PREC_SUM) * inv
    yc = y - mean
    var = _mm(yc * yc, ones, RW_PREC_SUM) * inv
    yn = yc * lax.rsqrt(var + RW_GN_EPS) * lnw_ref[...] + lnb_ref[...]
    bonus = _mm(r_ref[...] * k_ref[...] * rk_ref[...], ones, RW_PREC_SUM) * v_ref[...]
    o_ref[...] = (yn + bonus) * g_ref[...]


def _rw_post(y_f, y_b, r, k, v, g, prm, j):
    t = y_f.shape[0]
    tok = pl.BlockSpec((TOKEN_TILE, D_B), lambda i: (i, 0))
    vec = _layer_spec((1, D_B))(j)
    return pl.pallas_call(
        _rw_post_body,
        grid=(t // TOKEN_TILE,),
        in_specs=[tok] * 6 + [vec] * 3,
        out_specs=tok,
        out_shape=jax.ShapeDtypeStruct((t, D_B), f32),
        compiler_params=_cparams("parallel"),
        name="rw_post",
    )(y_f, y_b, r, k, v, g, prm["r_k"], prm["ln_w"], prm["ln_b"])


def _rwkv7_mixer(proj, prm, j, groups):
    r, k, v, kk, kka, lw_f, lw_b, g = _rw_prep(proj, prm, j, groups)
    coef_f = _rw_chunk(r, k, v, kk, kka, lw_f, reverse=False)
    coef_b = _rw_chunk(r, k, v, kk, kka, lw_b, reverse=True)
    y_f, y_b = _rw_seq(coef_f, coef_b, groups)
    return _rw_post(y_f, y_b, r, k, v, g, prm, j)


def _halo_specs(tile, width, col_blk, n_rows):
    per = tile // SUBLANES
    last8 = n_rows // SUBLANES - 1
    prev = pl.BlockSpec((SUBLANES, width), lambda i: (jnp.maximum(i * per - 1, 0), col_blk))
    nxt = pl.BlockSpec((SUBLANES, width), lambda i: (jnp.minimum((i + 1) * per, last8), col_blk))
    return prev, nxt


def _shift_rows(x, halo, k, row):
    n = x.shape[0]
    if k > 0:
        y = pltpu.roll(x, k, 0)
        for r in range(k):
            y = jnp.where(row == r, halo[SUBLANES - k + r:SUBLANES - k + r + 1, :], y)
        return y
    y = pltpu.roll(x, n - 1, 0)
    return jnp.where(row == n - 1, halo[0:1, :], y)


LRU_TILE = 512
CD_XB_BLK = 3
CD_GB_BLK = 4


def _lru_gate_body(x_ref, xp_ref, xn_ref, cw_ref, cb_ref, lam_ref, w_ref, b_ref,
                   af_ref, bf_ref, ab_ref, bb_ref, *, groups):
    i = pl.program_id(0)
    first = _seq_edge(i, groups, LRU_TILE, last=False)
    last = _seq_edge(i, groups, LRU_TILE, last=True)
    x = x_ref[...]
    prev = jnp.where(first, 0.0, xp_ref[...])
    nxt = jnp.where(last, 0.0, xn_ref[...])
    row = lax.broadcasted_iota(jnp.int32, x.shape, 0)
    xc = (cw_ref[0:1, :] * _shift_rows(x, prev, 2, row) + cw_ref[1:2, :] * _shift_rows(x, prev, 1, row)
          + cw_ref[2:3, :] * x + cw_ref[3:4, :] * _shift_rows(x, nxt, -1, row) + cb_ref[...])
    pre = _dot(xc.astype(bf16), w_ref[...]) + b_ref[...]
    outs = ((af_ref, bf_ref), (ab_ref, bb_ref))
    for direction in range(2):
        base = 2 * direction * D_C
        gate_r = jax.nn.sigmoid(pre[:, base:base + D_C])
        gate_i = jax.nn.sigmoid(pre[:, base + D_C:base + 2 * D_C])
        log_a = -LRU_C * gate_r * jax.nn.softplus(-lam_ref[direction:direction + 1, :])
        t = jnp.tanh(log_a)
        mult = jnp.sqrt(-2.0 * t / (1.0 - t))
        a_ref, b_ref_out = outs[direction]
        a_ref[...] = jnp.exp(log_a)
        b_ref_out[...] = mult * gate_i * xc


def _lru_gates(proj, prm, j, groups):
    t = proj.shape[0]
    tok = lambda i: (i, 0)
    prev_spec, next_spec = _halo_specs(LRU_TILE, D_C, CD_XB_BLK, t)
    out = jax.ShapeDtypeStruct((t, D_C), f32)
    return pl.pallas_call(
        functools.partial(_lru_gate_body, groups=groups),
        grid=(t // LRU_TILE,),
        in_specs=[
            pl.BlockSpec((LRU_TILE, D_C), lambda i: (i, CD_XB_BLK)),
            prev_spec,
            next_spec,
            _layer_spec((LRU_CONV, D_C))(j),
            _layer_spec((1, D_C))(j),
            _layer_spec((2, D_C))(j),
            _layer_spec((D_C, 4 * D_C))(j),
            _layer_spec((1, 4 * D_C))(j),
        ],
        out_specs=[pl.BlockSpec((LRU_TILE, D_C), tok)] * 4,
        out_shape=[out] * 4,
        compiler_params=_cparams("parallel"),
        name="lru_gates",
    )(proj, proj, proj, prm["conv_w"], prm["conv_b"], prm["lam"], prm["w_gates"], prm["b_gates"])


def _lru_scan_body(af_ref, bf_ref, ab_ref, bb_ref, hf_ref, hb_ref, carry_f, carry_b, *, groups, n_blk):
    i = pl.program_id(0)

    @pl.when(_seq_edge(i, groups, LRU_TILE, last=False))
    def _():
        carry_f[...] = jnp.zeros_like(carry_f)

    @pl.when(_seq_edge(n_blk - 1 - i, groups, LRU_TILE, last=True))
    def _():
        carry_b[...] = jnp.zeros_like(carry_b)

    row = lax.broadcasted_iota(jnp.int32, (SUBLANES, D_C), 0)
    n_tiles = LRU_TILE // SUBLANES

    def run(a_ref, b_ref, h_ref, carry, reverse):
        def tile_step(k, c):
            t = (n_tiles - 1 - k) if reverse else k
            r0 = pl.multiple_of(t * SUBLANES, SUBLANES)
            a = a_ref[pl.ds(r0, SUBLANES), :]
            b = b_ref[pl.ds(r0, SUBLANES), :]
            for d in (1, 2, 4):
                sh = (SUBLANES - d) if reverse else d
                keep = (row < SUBLANES - d) if reverse else (row >= d)
                a_s = jnp.where(keep, pltpu.roll(a, sh, 0), 1.0)
                b_s = jnp.where(keep, pltpu.roll(b, sh, 0), 0.0)
                b = b + a * b_s
                a = a * a_s
            h = b + a * c
            h_ref[pl.ds(r0, SUBLANES), :] = h
            edge = 0 if reverse else SUBLANES - 1
            return jnp.broadcast_to(h[edge:edge + 1, :], (SUBLANES, D_C))

        carry[...] = lax.fori_loop(0, n_tiles, tile_step, carry[...])

    run(af_ref, bf_ref, hf_ref, carry_f, False)
    run(ab_ref, bb_ref, hb_ref, carry_b, True)


def _lru_scan(a_f, b_f, a_b, b_b, groups):
    t = a_f.shape[0]
    n_blk = t // LRU_TILE
    fwd = lambda i: (i, 0)
    bwd = lambda i: (n_blk - 1 - i, 0)
    out = jax.ShapeDtypeStruct((t, D_C), f32)
    return pl.pallas_call(
        functools.partial(_lru_scan_body, groups=groups, n_blk=n_blk),
        grid=(n_blk,),
        in_specs=[pl.BlockSpec((LRU_TILE, D_C), fwd), pl.BlockSpec((LRU_TILE, D_C), fwd),
                  pl.BlockSpec((LRU_TILE, D_C), bwd), pl.BlockSpec((LRU_TILE, D_C), bwd)],
        out_specs=[pl.BlockSpec((LRU_TILE, D_C), fwd), pl.BlockSpec((LRU_TILE, D_C), bwd)],
        out_shape=[out, out],
        scratch_shapes=[pltpu.VMEM((SUBLANES, D_C), f32), pltpu.VMEM((SUBLANES, D_C), f32)],
        compiler_params=_cparams("arbitrary"),
        name="lru_scan",
    )(a_f, b_f, a_b, b_b)


HY_N2 = 256
HY_CW = 128
HY_SC_TILE = 512
HY_MLP_TILE = 512
HY_PREC_S1 = "b1"
HY_PREC_S2 = "b1"
HY_S1_UNROLL = 2


def _hy_dims(seq_len):
    n = 2 * seq_len
    n1 = n // HY_N2
    lb = n1 // 2
    k1 = n1 // 2 + 1
    k1p = -(-k1 // SUBLANES) * SUBLANES
    return n, n1, lb, k1, k1p


def _hy_tables(seq_len):
    n, n1, lb, k1, k1p = _hy_dims(seq_len)
    two_pi = 2.0 * math.pi
    kk = np.arange(k1, dtype=np.float64)[:, None]
    nn = np.arange(lb, dtype=np.float64)[None, :]
    ang1 = two_pi * kk * nn / n1
    f1 = np.zeros((2 * k1p, lb), np.float64)
    f1[:k1] = np.cos(ang1)
    f1[k1p:k1p + k1] = -np.sin(ang1)
    weight = np.where((np.arange(k1) == 0) | (np.arange(k1) == n1 // 2), 1.0, 2.0)[None, :] / n
    f1inv = np.zeros((lb, 2 * k1p), np.float64)
    f1inv[:, :k1] = weight * np.cos(ang1.T)
    f1inv[:, k1p:k1p + k1] = -weight * np.sin(ang1.T)
    idx = np.arange(HY_N2, dtype=np.float64)
    ang2 = two_pi * np.outer(idx, idx) / HY_N2
    f2 = np.concatenate([np.cos(ang2), -np.sin(ang2)], axis=0)
    prod = (jnp.arange(k1, dtype=jnp.int32)[:, None] * jnp.arange(HY_N2, dtype=jnp.int32)[None, :]) % n
    ang = prod.astype(f32) * f32(two_pi / n)
    bc = lambda a: jnp.broadcast_to(a.reshape(k1 * HY_N2, 1), (k1 * HY_N2, HY_CW))
    eye8 = np.eye(SUBLANES)
    return dict(f1=jnp.asarray(np.kron(f1, eye8), f32), f1inv=jnp.asarray(np.kron(f1inv, eye8), f32),
                f2=jnp.asarray(f2, f32), tw_re=bc(jnp.cos(ang)), tw_im=bc(-jnp.sin(ang)))


def _hy_forward(z_ref, f1_ref, f2_ref, twr_ref, twi_ref, wr, wi, dims):
    _, _, lb, k1, k1p = dims

    def stage1_group(n2):
        tiles = [z_ref[pl.ds(pl.multiple_of(b * HY_N2 + n2, SUBLANES), SUBLANES), :] for b in range(lb)]
        y = _mm(f1_ref[...], jnp.concatenate(tiles, axis=0), HY_PREC_S1)
        for s in range(k1p):
            dst = pl.ds(pl.multiple_of(s * HY_N2 + n2, SUBLANES), SUBLANES)
            wr[dst, :] = y[s * SUBLANES:(s + 1) * SUBLANES]
            wi[dst, :] = y[(k1p + s) * SUBLANES:(k1p + s + 1) * SUBLANES]

    def stage1(g, carry):
        for u in range(HY_S1_UNROLL):
            stage1_group((g * HY_S1_UNROLL + u) * SUBLANES)
        return carry

    lax.fori_loop(0, HY_N2 // (SUBLANES * HY_S1_UNROLL), stage1, 0)

    def stage2(k, carry):
        r0 = pl.multiple_of(k * HY_N2, HY_N2)
        sl = pl.ds(r0, HY_N2)
        yr, yi, tr, ti = wr[sl, :], wi[sl, :], twr_ref[sl, :], twi_ref[sl, :]
        ar = yr * tr - yi * ti
        ai = yr * ti + yi * tr
        p = _mm(f2_ref[...], ar, HY_PREC_S2)
        q = _mm(f2_ref[...], ai, HY_PREC_S2)
        wr[sl, :] = p[:HY_N2] - q[HY_N2:]
        wi[sl, :] = p[HY_N2:] + q[:HY_N2]
        return carry

    lax.fori_loop(0, k1, stage2, 0)


def _hy_inverse(hr_ref, hi_ref, f1inv_ref, f2_ref, twr_ref, twi_ref, wr, wi, o_ref, dims):
    _, _, lb, k1, k1p = dims

    def stage2(k, carry):
        r0 = pl.multiple_of(k * HY_N2, HY_N2)
        sl = pl.ds(r0, HY_N2)
        xr, xi, hr, hi = wr[sl, :], wi[sl, :], hr_ref[sl, :], hi_ref[sl, :]
        zr = xr * hr - xi * hi
        zi = xr * hi + xi * hr
        p = _mm(f2_ref[...], zr, HY_PREC_S2)
        q = _mm(f2_ref[...], zi, HY_PREC_S2)
        vr = p[:HY_N2] + q[HY_N2:]
        vi = q[:HY_N2] - p[HY_N2:]
        tr, ti = twr_ref[sl, :], twi_ref[sl, :]
        wr[sl, :] = vr * tr + vi * ti
        wi[sl, :] = vi * tr - vr * ti
        return carry

    lax.fori_loop(0, k1, stage2, 0)

    def stage1_group(n2):
        src = [pl.ds(pl.multiple_of(s * HY_N2 + n2, SUBLANES), SUBLANES) for s in range(k1p)]
        tiles = [wr[d, :] for d in src] + [wi[d, :] for d in src]
        x = _mm(f1inv_ref[...], jnp.concatenate(tiles, axis=0), HY_PREC_S1)
        for b in range(lb):
            dst = pl.ds(pl.multiple_of(b * HY_N2 + n2, SUBLANES), SUBLANES)
            o_ref[dst, :] = x[b * SUBLANES:(b + 1) * SUBLANES]

    def stage1(g, carry):
        for u in range(HY_S1_UNROLL):
            stage1_group((g * HY_S1_UNROLL + u) * SUBLANES)
        return carry

    lax.fori_loop(0, HY_N2 // (SUBLANES * HY_S1_UNROLL), stage1, 0)


def _hy_shortconv_body(x_ref, xp_ref, xn_ref, w_ref, b_ref, o_ref, *, groups):
    i = pl.program_id(0)
    x = x_ref[...]
    prev = jnp.where(_seq_edge(i, groups, HY_SC_TILE, last=False), 0.0, xp_ref[...])
    nxt = jnp.where(_seq_edge(i, groups, HY_SC_TILE, last=True), 0.0, xn_ref[...])
    row = lax.broadcasted_iota(jnp.int32, x.shape, 0)
    o_ref[...] = (w_ref[0:1, :] * _shift_rows(x, prev, 1, row) + w_ref[1:2, :] * x
                  + w_ref[2:3, :] * _shift_rows(x, nxt, -1, row) + b_ref[...])


def _hy_shortconv(proj, conv_w, conv_b, j, groups):
    t = proj.shape[0]
    width = 3 * D_D
    prev_spec, next_spec = _halo_specs(HY_SC_TILE, width, 0, t)
    return pl.pallas_call(
        functools.partial(_hy_shortconv_body, groups=groups),
        grid=(t // HY_SC_TILE,),
        in_specs=[pl.BlockSpec((HY_SC_TILE, width), lambda i: (i, 0)), prev_spec, next_spec,
                  _layer_spec((3, width))(j), _layer_spec((1, width))(j)],
        out_specs=pl.BlockSpec((HY_SC_TILE, width), lambda i: (i, 0)),
        out_shape=jax.ShapeDtypeStruct((t, width), f32),
        compiler_params=_cparams("parallel"),
        name="hy_shortconv",
    )(proj, proj, proj, conv_w, conv_b)


def _hy_mlp_body(z_ref, w1_ref, b1_ref, w2_ref, b2_ref, fr_ref, w3_ref, dl_ref, o_ref):
    z = z_ref[...]
    h = jnp.sin(fr_ref[0:1, :] * (_dot_hi(z, w1_ref[...]) + b1_ref[...]))
    h = jnp.sin(fr_ref[1:2, :] * (_dot_hi(h, w2_ref[...]) + b2_ref[...]))
    h = _dot_hi(h, w3_ref[...])
    o_ref[...] = h * jnp.exp(-z[:, 0:1] * dl_ref[...])


def _hy_mlp(seq_len, w1, b1, w2, b2, freq, w3):
    t = jnp.linspace(0.0, 1.0, seq_len, dtype=f32)[:, None]
    bands = (HY_POS_EMB - 1) // 2
    freqs = jnp.linspace(1e-4, bands - 1, bands, dtype=f32)[None, :]
    wpos = (2.0 * math.pi / seq_len) * jnp.arange(seq_len, dtype=f32)[:, None]
    z = jnp.concatenate([t, jnp.cos(freqs * wpos), -jnp.sin(freqs * wpos)], axis=-1)
    z = jnp.pad(z, ((0, 0), (0, LANES - HY_POS_EMB)))
    w1 = jnp.pad(w1, ((0, LANES - HY_POS_EMB), (0, 0)))
    max_decay = math.log(1e-2) / 0.3
    min_decay = math.log(1e-2) / 1.5
    deltas = jnp.abs(jnp.linspace(min_decay, max_decay, D_D, dtype=f32))
    deltas = jnp.tile(deltas, 4)[None, :]
    width = 4 * D_D
    hid = HY_FILTER_HIDDEN
    return pl.pallas_call(
        _hy_mlp_body,
        grid=(seq_len // HY_MLP_TILE,),
        in_specs=[pl.BlockSpec((HY_MLP_TILE, LANES), lambda i: (i, 0)),
                  _const_spec((LANES, hid)), _const_spec((1, hid)), _const_spec((hid, hid)),
                  _const_spec((1, hid)), _const_spec((2, hid)), _const_spec((hid, width)),
                  _const_spec((1, width))],
        out_specs=pl.BlockSpec((HY_MLP_TILE, width), lambda i: (i, 0)),
        out_shape=jax.ShapeDtypeStruct((seq_len, width), f32),
        compiler_params=_cparams("parallel"),
        name="hy_mlp",
    )(z, w1, b1[None, :], w2, b2[None, :], freq, w3, deltas)


def _hy_spectrum_body(hf_ref, hb_ref, f1_ref, f2_ref, twr_ref, twi_ref, or_ref, oi_ref,
                      zb, wr, wi, *, dims):
    k1 = dims[3]
    rows = k1 * HY_N2
    hf = hf_ref[...]
    hb = hb_ref[...]
    row = lax.broadcasted_iota(jnp.int32, hb.shape, 0)
    hb0 = jnp.where(row == 0, 0.0, hb)
    scale = 1.0 / (jnp.sum(jnp.abs(hf), axis=0, keepdims=True) + jnp.sum(jnp.abs(hb0), axis=0, keepdims=True))
    _hy_forward(hf_ref, f1_ref, f2_ref, twr_ref, twi_ref, wr, wi, dims)
    or_ref[...] = wr[:rows, :] * scale
    oi_ref[...] = wi[:rows, :] * scale
    zb[...] = hb0
    _hy_forward(zb, f1_ref, f2_ref, twr_ref, twi_ref, wr, wi, dims)
    or_ref[...] = or_ref[...] + wr[:rows, :] * scale
    oi_ref[...] = oi_ref[...] - wi[:rows, :] * scale


def _hy_spectrum(taps, tables, seq_len):
    dims = _hy_dims(seq_len)
    _, _, lb, k1, k1p = dims
    n_ct = D_D // HY_CW
    rows = k1 * HY_N2
    out = jax.ShapeDtypeStruct((2, rows, D_D), f32)
    return pl.pallas_call(
        functools.partial(_hy_spectrum_body, dims=dims),
        grid=(2, n_ct),
        in_specs=[pl.BlockSpec((seq_len, HY_CW), lambda o, c: (0, o * n_ct + c), pipeline_mode=pl.Buffered(1)),
                  pl.BlockSpec((seq_len, HY_CW), lambda o, c: (0, (2 + o) * n_ct + c),
                               pipeline_mode=pl.Buffered(1)),
                  _const_spec((2 * k1p * SUBLANES, lb * SUBLANES)), _const_spec((2 * HY_N2, HY_N2)),
                  _const_spec((rows, HY_CW)), _const_spec((rows, HY_CW))],
        out_specs=[pl.BlockSpec((None, rows, HY_CW), lambda o, c: (o, 0, c))] * 2,
        out_shape=[out, out],
        scratch_shapes=[pltpu.VMEM((seq_len, HY_CW), f32),
                        pltpu.VMEM((k1p * HY_N2, HY_CW), f32), pltpu.VMEM((k1p * HY_N2, HY_CW), f32)],
        compiler_params=_cparams("arbitrary", "arbitrary"),
        name="hy_spectrum",
    )(taps, taps, tables["f1"], tables["f2"], tables["tw_re"], tables["tw_im"])


def _hy_conv_body(z_ref, g_ref, hr_ref, hi_ref, bias_ref, f1_ref, f1inv_ref, f2_ref, twr_ref, twi_ref,
                  o_ref, wr, wi, *, dims):
    _hy_forward(z_ref, f1_ref, f2_ref, twr_ref, twi_ref, wr, wi, dims)
    _hy_inverse(hr_ref, hi_ref, f1inv_ref, f2_ref, twr_ref, twi_ref, wr, wi, o_ref, dims)
    bias = bias_ref[...]
    n_chunks = z_ref.shape[0] // HY_N2

    def gate(c, carry):
        sl = pl.ds(pl.multiple_of(c * HY_N2, HY_N2), HY_N2)
        o_ref[sl, :] = g_ref[sl, :] * (o_ref[sl, :] + bias * z_ref[sl, :])
        return carry

    lax.fori_loop(0, n_chunks, gate, 0)


def _hy_conv(z, z_off, z_col, gate, gate_off, gate_col, h_re, h_im, bias, tables, order, j, n_seq, seq_len):
    dims = _hy_dims(seq_len)
    _, _, lb, k1, k1p = dims
    n_ct = D_D // HY_CW
    rows = k1 * HY_N2
    z0, g0 = z_off // seq_len, gate_off // seq_len
    one = pl.Buffered(1)
    return pl.pallas_call(
        functools.partial(_hy_conv_body, dims=dims),
        grid=(n_ct, n_seq),
        in_specs=[
            pl.BlockSpec((seq_len, HY_CW), lambda c, b: (z0 + b, z_col * n_ct + c), pipeline_mode=one),
            pl.BlockSpec((seq_len, HY_CW), lambda c, b: (g0 + b, gate_col * n_ct + c), pipeline_mode=one),
            pl.BlockSpec((None, rows, HY_CW), lambda c, b: (order, 0, c), pipeline_mode=one),
            pl.BlockSpec((None, rows, HY_CW), lambda c, b: (order, 0, c), pipeline_mode=one),
            pl.BlockSpec((None, None, 1, HY_CW), lambda c, b: (j, order, 0, c)),
            _const_spec((2 * k1p * SUBLANES, lb * SUBLANES)), _const_spec((lb * SUBLANES, 2 * k1p * SUBLANES)), _const_spec((2 * HY_N2, HY_N2)),
            _const_spec((rows, HY_CW)), _const_spec((rows, HY_CW)),
        ],
        out_specs=pl.BlockSpec((seq_len, HY_CW), lambda c, b: (b, c)),
        out_shape=jax.ShapeDtypeStruct((n_seq * seq_len, D_D), f32),
        scratch_shapes=[pltpu.VMEM((k1p * HY_N2, HY_CW), f32), pltpu.VMEM((k1p * HY_N2, HY_CW), f32)],
        compiler_params=_cparams("arbitrary", "arbitrary"),
        name="hy_conv",
    )(z, gate, h_re, h_im, bias, tables["f1"], tables["f1inv"], tables["f2"], tables["tw_re"], tables["tw_im"])


def _hyena_mixer(proj, prm, j, groups):
    pc = _hy_shortconv(proj, prm["conv_w"], prm["conv_b"], j, groups)
    outs = []
    for tok_off, n_seq, seq_len in groups:
        tables = _hy_tables(seq_len)
        taps = _hy_mlp(seq_len, prm["f_w1"][j], prm["f_b1"][j], prm["f_w2"][j], prm["f_b2"][j],
                       prm["f_freq"][j], prm["f_w3"][j])
        h_re, h_im = _hy_spectrum(taps, tables, seq_len)
        z1 = _hy_conv(pc, tok_off, 0, pc, tok_off, 1, h_re, h_im, prm["bias"], tables, 0, j, n_seq, seq_len)
        y = _hy_conv(z1, 0, 0, pc, tok_off, 2, h_re, h_im, prm["bias"], tables, 1, j, n_seq, seq_len)
        outs.append(y)
    return jnp.concatenate(outs, axis=0)


def _outproj_odd_body(x_ref, hf_ref, hb_ref, gb_ref, yd_ref, w_ref, o_ref):
    y_c = (hf_ref[...] + hb_ref[...]) * jax.nn.gelu(gb_ref[...])
    acc = _dot(y_c.astype(bf16), w_ref[:D_C, :])
    acc = acc + _dot(yd_ref[...].astype(bf16), w_ref[D_C:, :])
    o_ref[...] = x_ref[...] + acc


def _outproj_odd(x, h_f, h_b, proj, y_d, w, j):
    t = x.shape[0]
    tok = lambda i: (i, 0)
    return pl.pallas_call(
        _outproj_odd_body,
        grid=(t // TOKEN_TILE,),
        in_specs=[
            pl.BlockSpec((TOKEN_TILE, D_MODEL), tok),
            pl.BlockSpec((TOKEN_TILE, D_C), tok),
            pl.BlockSpec((TOKEN_TILE, D_C), tok),
            pl.BlockSpec((TOKEN_TILE, D_C), lambda i: (i, CD_GB_BLK)),
            pl.BlockSpec((TOKEN_TILE, D_D), tok),
            _layer_spec((D_C + D_D, D_MODEL))(j),
        ],
        out_specs=pl.BlockSpec((TOKEN_TILE, D_MODEL), tok),
        out_shape=jax.ShapeDtypeStruct((t, D_MODEL), f32),
        compiler_params=_cparams("parallel"),
        name="outproj_odd",
    )(x, h_f, h_b, proj, y_d, w)


def _block_diag_heads(w):
    n, h, a, b = w.shape
    return jnp.einsum("nhij,hk->nhikj", w, jnp.eye(h, dtype=w.dtype)).reshape(n, h * a, h * b)


def _pad_rows(w, lo, total):
    return jnp.pad(w, ((0, 0), (lo, total - lo - w.shape[1]), (0, 0)))


def kernel(x_prompt, x_sample, ffn1_norm, ffn1_w_gate, ffn1_w_up, ffn1_w_down, mix_norm, ffn2_norm, ffn2_w_gate, ffn2_w_up, ffn2_w_down, ab_w_in, ab_w_out, s5_lambda_re, s5_lambda_im, s5_log_step, s5_b_re, s5_b_im, s5_c_re, s5_c_im, s5_d, s5_glu_w, s5_glu_b, rw_mu, rw_w0, rw_w_up, rw_a0, rw_a_up, rw_g_up, rw_k_k, rw_k_a, rw_r_k, rw_ln_w, rw_ln_b, cd_w_in, cd_w_out, lru_conv_w, lru_conv_b, lru_lambda, lru_wa, lru_ba, lru_wx, lru_bx, hy_conv_w, hy_conv_b, hy_f_w1, hy_f_b1, hy_f_w2, hy_f_b2, hy_f_freq, hy_f_w3, hy_bias, final_norm):
    n_p, l_p, _ = x_prompt.shape
    n_s, l_s, _ = x_sample.shape
    t_p = n_p * l_p
    groups = ((0, n_p, l_p), (t_p, n_s, l_s))
    x = jnp.concatenate([x_prompt.reshape(t_p, D_MODEL), x_sample.reshape(n_s * l_s, D_MODEL)], axis=0)

    row = lambda a: a[:, None, :]
    cast = lambda a: a.astype(bf16)
    ffn1 = (row(ffn1_norm), cast(ffn1_w_gate), cast(ffn1_w_up), cast(ffn1_w_down))
    ffn2 = (row(ffn2_norm), cast(ffn2_w_gate), cast(ffn2_w_up), cast(ffn2_w_down))
    mix_g = row(mix_norm)

    ab_in = cast(jnp.pad(ab_w_in, ((0, 0), (0, 0), (0, RW_P_WIDTH - ab_w_in.shape[-1]))))
    ab_out = cast(ab_w_out)
    s5 = dict(lam_re=s5_lambda_re, lam_im=s5_lambda_im, log_step=s5_log_step, b_re=s5_b_re, b_im=s5_b_im,
              c_re=s5_c_re, c_im=s5_c_im, d=row(s5_d), glu_w=cast(s5_glu_w), glu_b=row(s5_glu_b))
    lo = RW_DECAY_LORA
    lora_w = jnp.stack([_pad_rows(rw_w_up[:, 0], 0, RW_LORA_W), _pad_rows(rw_w_up[:, 1], lo, RW_LORA_W),
                        _pad_rows(rw_a_up, 2 * lo, RW_LORA_W),
                        _pad_rows(rw_g_up, 2 * lo + RW_A_LORA, RW_LORA_W)], axis=1)
    rw = dict(mu=row(jnp.pad(rw_mu, ((0, 0), (D_A, RW_P_WIDTH - D_A - rw_mu.shape[-1])))),
              lora_w=cast(lora_w), w0=rw_w0, a0=row(rw_a0), k_k=row(rw_k_k), k_a=row(rw_k_a),
              r_k=rw_r_k.reshape(-1, 1, D_B), ln_w=row(rw_ln_w), ln_b=row(rw_ln_b))

    cd_in = cast(jnp.concatenate([cd_w_in[..., 2 * D_C:], cd_w_in[..., :2 * D_C]], axis=-1))
    cd_out = cast(cd_w_out)
    w_gates = jnp.concatenate([_block_diag_heads(lru_wa[:, 0]), _block_diag_heads(lru_wx[:, 0]),
                               _block_diag_heads(lru_wa[:, 1]), _block_diag_heads(lru_wx[:, 1])], axis=-1)
    b_gates = jnp.concatenate([lru_ba[:, 0], lru_bx[:, 0], lru_ba[:, 1], lru_bx[:, 1]], axis=-1)
    lru = dict(conv_w=lru_conv_w, conv_b=row(lru_conv_b), lam=lru_lambda, w_gates=cast(w_gates),
               b_gates=row(b_gates))
    hy = dict(conv_w=hy_conv_w, conv_b=row(hy_conv_b), f_w1=hy_f_w1, f_b1=hy_f_b1, f_w2=hy_f_w2,
              f_b2=hy_f_b2, f_freq=hy_f_freq, f_w3=hy_f_w3, bias=hy_bias[:, :, None, :])

    for layer in range(DEPTH):
        j = layer // 2
        x = _ffn(x, *ffn1, layer)
        if layer % 2 == 0:
            proj = _proj(x, mix_g, ab_in, layer, j)
            y_a = _s5_mixer(proj, s5, j, groups)
            y_b = _rwkv7_mixer(proj, rw, j, groups)
            x = _outproj(x, y_a, y_b, ab_out, j)
        else:
            proj = _proj(x, mix_g, cd_in, layer, j)
            a_f, b_f, a_b, b_b = _lru_gates(proj, lru, j, groups)
            h_f, h_b = _lru_scan(a_f, b_f, a_b, b_b, groups)
            y_d = _hyena_mixer(proj, hy, j, groups)
            x = _outproj_odd(x, h_f, h_b, proj, y_d, cd_out, j)
        x = _ffn(x, *ffn2, layer)
    y = _final_norm(x, final_norm[None, :])
    return (y[:t_p].reshape(n_p, l_p, D_MODEL), y[t_p:].reshape(n_s, l_s, D_MODEL))
```

```python
import functools
import math

import numpy as np
import jax
import jax.numpy as jnp
from jax import lax
from jax.experimental import pallas as pl
from jax.experimental.pallas import tpu as pltpu

f32 = jnp.float32
bf16 = jnp.bfloat16

D_MODEL = 1024
DEPTH = 4
D_FF = 2816
RMS_EPS = 1e-6
D_A = 512
S5_GROUP = 16
S5_GROUPS = 32
S5_STATE = 64
D_B = 512
RW_HEAD = 64
RW_HEADS = 8
RW_DECAY_LORA = 32
RW_A_LORA = 32
RW_GATE_LORA = 64
RW_GN_EPS = 64e-5
D_C = 512
LRU_HEADS = 8
LRU_HEAD_DIM = 64
LRU_CONV = 4
LRU_C = 8.0
D_D = 512
HY_POS_EMB = 33
HY_FILTER_HIDDEN = 64

LANES = 128
SUBLANES = 8
VMEM_LIMIT = 56 * 1024 * 1024
TOKEN_TILE = 512
FF_CHUNK = 1408


def _cparams(*sem):
    return pltpu.CompilerParams(dimension_semantics=tuple(sem), vmem_limit_bytes=VMEM_LIMIT)


def _const_spec(shape):
    nd = len(shape)
    return pl.BlockSpec(shape, lambda *_: (0,) * nd, pipeline_mode=pl.Buffered(1))


def _layer_spec(shape):
    def make(layer):
        nd = len(shape)
        return pl.BlockSpec((None,) + tuple(shape), lambda *_: (layer,) + (0,) * nd,
                            pipeline_mode=pl.Buffered(1))
    return make


def _rms(x, g):
    ms = jnp.mean(x * x, axis=-1, keepdims=True)
    return x * lax.rsqrt(ms + RMS_EPS) * g


def _dot(a, b):
    return jnp.dot(a, b, preferred_element_type=f32)


_HI = lax.Precision.HIGHEST
_NN = (((1,), (0,)), ((), ()))
_NT = (((1,), (1,)), ((), ()))
_TN = (((0,), (0,)), ((), ()))


def _mm(a, b, prec, dims=_NN):
    dg = functools.partial(lax.dot_general, dimension_numbers=dims, preferred_element_type=f32)
    if prec == "hi":
        return dg(a, b, precision=_HI)
    a1, b1 = a.astype(bf16), b.astype(bf16)
    if prec == "b1":
        return dg(a1, b1)
    a2 = (a - a1.astype(f32)).astype(bf16)
    if prec == "x2":
        return dg(a1, b1) + dg(a2, b1)
    b2 = (b - b1.astype(f32)).astype(bf16)
    return dg(a1, b1) + (dg(a1, b2) + dg(a2, b1))


def _dot_hi(a, b):
    return _mm(a, b, "hi")


def _ffn_body(x_ref, g_ref, wg_ref, wu_ref, wd_ref, o_ref):
    x = x_ref[...]
    h = _rms(x, g_ref[...]).astype(bf16)
    acc = None
    for c in range(D_FF // FF_CHUNK):
        sl = slice(c * FF_CHUNK, (c + 1) * FF_CHUNK)
        gate = _dot(h, wg_ref[:, sl])
        up = _dot(h, wu_ref[:, sl])
        act = (gate * jax.nn.sigmoid(gate) * up).astype(bf16)
        part = _dot(act, wd_ref[sl, :])
        acc = part if acc is None else acc + part
    o_ref[...] = x + 0.5 * acc


def _ffn(x, norm, wg, wu, wd, layer):
    t = x.shape[0]
    return pl.pallas_call(
        _ffn_body,
        grid=(t // TOKEN_TILE,),
        in_specs=[
            pl.BlockSpec((TOKEN_TILE, D_MODEL), lambda i: (i, 0)),
            _layer_spec((1, D_MODEL))(layer),
            _layer_spec((D_MODEL, D_FF))(layer),
            _layer_spec((D_MODEL, D_FF))(layer),
            _layer_spec((D_FF, D_MODEL))(layer),
        ],
        out_specs=pl.BlockSpec((TOKEN_TILE, D_MODEL), lambda i: (i, 0)),
        out_shape=jax.ShapeDtypeStruct((t, D_MODEL), f32),
        compiler_params=_cparams("parallel"),
        name="ffn",
    )(x, norm, wg, wu, wd)


def _proj_body(x_ref, g_ref, w_ref, o_ref):
    h = _rms(x_ref[...], g_ref[...]).astype(bf16)
    o_ref[...] = _dot(h, w_ref[...])


def _proj(x, norm, w, layer, j):
    t = x.shape[0]
    p = w.shape[-1]
    return pl.pallas_call(
        _proj_body,
        grid=(t // TOKEN_TILE,),
        in_specs=[
            pl.BlockSpec((TOKEN_TILE, D_MODEL), lambda i: (i, 0)),
            _layer_spec((1, D_MODEL))(layer),
            _layer_spec((D_MODEL, p))(j),
        ],
        out_specs=pl.BlockSpec((TOKEN_TILE, p), lambda i: (i, 0)),
        out_shape=jax.ShapeDtypeStruct((t, p), f32),
        compiler_params=_cparams("parallel"),
        name="proj",
    )(x, norm, w)


def _outproj_body(x_ref, ya_ref, yb_ref, w_ref, o_ref):
    half = w_ref.shape[0] // 2
    acc = _dot(ya_ref[...].astype(bf16), w_ref[:half, :])
    acc = acc + _dot(yb_ref[...].astype(bf16), w_ref[half:, :])
    o_ref[...] = x_ref[...] + acc


def _outproj(x, ya, yb, w, j):
    t = x.shape[0]
    half = ya.shape[-1]
    return pl.pallas_call(
        _outproj_body,
        grid=(t // TOKEN_TILE,),
        in_specs=[
            pl.BlockSpec((TOKEN_TILE, D_MODEL), lambda i: (i, 0)),
            pl.BlockSpec((TOKEN_TILE, half), lambda i: (i, 0)),
            pl.BlockSpec((TOKEN_TILE, half), lambda i: (i, 0)),
            _layer_spec((2 * half, D_MODEL))(j),
        ],
        out_specs=pl.BlockSpec((TOKEN_TILE, D_MODEL), lambda i: (i, 0)),
        out_shape=jax.ShapeDtypeStruct((t, D_MODEL), f32),
        compiler_params=_cparams("parallel"),
        name="outproj",
    )(x, ya, yb, w)


def _final_norm_body(x_ref, g_ref, o_ref):
    o_ref[...] = _rms(x_ref[...], g_ref[...])


def _final_norm(x, g):
    t = x.shape[0]
    return pl.pallas_call(
        _final_norm_body,
        grid=(t // TOKEN_TILE,),
        in_specs=[pl.BlockSpec((TOKEN_TILE, D_MODEL), lambda i: (i, 0)), _const_spec((1, D_MODEL))],
        out_specs=pl.BlockSpec((TOKEN_TILE, D_MODEL), lambda i: (i, 0)),
        out_shape=jax.ShapeDtypeStruct((t, D_MODEL), f32),
        compiler_params=_cparams("parallel"),
        name="final_norm",
    )(x, g)


def _seq_edge(blk, groups, tile, last):
    hit = None
    for off, n_seq, seq_len in groups:
        start, per, n = off // tile, seq_len // tile, n_seq * (seq_len // tile)
        rel = blk - start
        edge = (per - 1) if last else 0
        h = (rel >= 0) & (rel < n) & (lax.rem(jnp.maximum(rel, 0), per) == edge)
        hit = h if hit is None else (hit | h)
    return hit


S5_TILE = 512
S5_OCT = 4
S5_OCT_STATES = 512


def _s5_param_body(lr_ref, li_ref, st_ref, lr2_ref, li2_ref, st2_ref, br_ref, bi_ref,
                   pwr_ref, pwi_ref, bbr_ref, bbi_ref):
    def discretize(lam_re, lam_im, log_step):
        lam_re = jnp.minimum(lam_re, -1e-4)
        step = jnp.exp(log_step)
        mag = jnp.exp(lam_re * step)
        ang = lam_im * step
        return lam_re, lam_im, mag * jnp.cos(ang), mag * jnp.sin(ang)

    _, _, ar, ai = discretize(lr_ref[...], li_ref[...], st_ref[...])
    pr, pi = ar, ai
    for j in range(SUBLANES):
        pwr_ref[j] = pr
        pwi_ref[j] = pi
        pr, pi = pr * ar - pi * ai, pr * ai + pi * ar
    lam_re, lam_im, ar, ai = discretize(lr2_ref[...], li2_ref[...], st2_ref[...])
    den = lam_re * lam_re + lam_im * lam_im
    num_re = ar - 1.0
    coef_re = (num_re * lam_re + ai * lam_im) / den
    coef_im = (ai * lam_re - num_re * lam_im) / den
    b_re, b_im = br_ref[...], bi_ref[...]
    bbr_ref[...] = coef_re * b_re - coef_im * b_im
    bbi_ref[...] = coef_re * b_im + coef_im * b_re


def _s5_params(lam_re, lam_im, log_step, b_re, b_im):
    g2 = 2 * S5_GROUPS
    rows = g2 * S5_GROUP
    lr = lam_re.reshape(g2, S5_STATE)
    li = lam_im.reshape(g2, S5_STATE)
    st = jnp.broadcast_to(log_step.reshape(g2, 1), (g2, S5_STATE))
    rep = lambda a: jnp.broadcast_to(a[:, None, :], (g2, S5_GROUP, S5_STATE)).reshape(rows, S5_STATE)
    bt = lambda b: jnp.swapaxes(b, -1, -2).reshape(rows, S5_STATE)
    full = lambda s: pl.BlockSpec(s, lambda: (0,) * len(s))
    pwr, pwi, bbr, bbi = pl.pallas_call(
        _s5_param_body,
        in_specs=[full((g2, S5_STATE))] * 3 + [full((rows, S5_STATE))] * 5,
        out_specs=[full((SUBLANES, g2, S5_STATE))] * 2 + [full((rows, S5_STATE))] * 2,
        out_shape=[jax.ShapeDtypeStruct((SUBLANES, g2, S5_STATE), f32)] * 2
        + [jax.ShapeDtypeStruct((rows, S5_STATE), f32)] * 2,
        name="s5_params",
    )(lr, li, st, rep(lr), rep(li), rep(st), bt(b_re), bt(b_im))
    n_all = S5_GROUPS * S5_STATE
    pw = lambda p: p.reshape(SUBLANES, 2, n_all)
    bb = lambda b: b.reshape(2, S5_GROUPS, S5_GROUP, S5_STATE)
    return pw(pwr), pw(pwi), bb(bbr), bb(bbi)


def _s5_block_mats(bb_re, bb_im, c_re, c_im):
    eye = jnp.eye(8, dtype=f32)

    def in_mat(b):
        b4 = b.reshape(S5_OCT, 8, S5_GROUP, S5_STATE)
        return jnp.einsum("qgcn,gh->qgchn", b4, eye).reshape(S5_OCT, 8 * S5_GROUP, 8 * S5_STATE)

    def out_mat(c):
        c4 = c.reshape(S5_OCT, 8, S5_GROUP, S5_STATE)
        return jnp.einsum("qgcn,gh->qgnhc", c4, eye).reshape(S5_OCT, 8 * S5_STATE, 8 * S5_GROUP)

    w_in = jnp.concatenate([in_mat(bb_re), in_mat(bb_im)], axis=-1).astype(bf16)
    return w_in, out_mat(c_re).astype(bf16), out_mat(c_im).astype(bf16)


def _s5_scan_body(u_ref, win_ref, wcr_ref, wci_ref, pwr_ref, pwi_ref, o_ref,
                  x_buf, h_buf, carry, *, groups, reverse, n_blk):
    pid = pl.program_id(0)
    blk = (n_blk - 1 - pid) if reverse else pid

    @pl.when(_seq_edge(blk, groups, S5_TILE, last=reverse))
    def _():
        carry[...] = jnp.zeros_like(carry)

    row = lax.broadcasted_iota(jnp.int32, (SUBLANES, S5_OCT_STATES), 0)
    n_tiles = S5_TILE // SUBLANES
    for q in range(S5_OCT):
        lanes = slice(q * S5_OCT_STATES, (q + 1) * S5_OCT_STATES)
        x_buf[...] = _dot(u_ref[:, q * LANES:(q + 1) * LANES].astype(bf16), win_ref[q])

        def power(j):
            shape = (SUBLANES, S5_OCT_STATES)
            return (jnp.broadcast_to(pwr_ref[j:j + 1, lanes], shape),
                    jnp.broadcast_to(pwi_ref[j:j + 1, lanes], shape))

        levels = []
        for d in (1, 2, 4):
            ar, ai = power(d - 1)
            keep = (row < SUBLANES - d) if reverse else (row >= d)
            levels.append((d, jnp.where(keep, ar, 0.0), jnp.where(keep, ai, 0.0)))
        cr = jnp.zeros((SUBLANES, S5_OCT_STATES), f32)
        ci = jnp.zeros((SUBLANES, S5_OCT_STATES), f32)
        for j in range(SUBLANES):
            pr, pi = power(j)
            sel = (row == (SUBLANES - 1 - j)) if reverse else (row == j)
            cr = jnp.where(sel, pr, cr)
            ci = jnp.where(sel, pi, ci)

        def tile_step(i, c):
            c_re, c_im = c
            t = (n_tiles - 1 - i) if reverse else i
            r0 = pl.multiple_of(t * SUBLANES, SUBLANES)
            h_re = x_buf[pl.ds(r0, SUBLANES), :S5_OCT_STATES]
            h_im = x_buf[pl.ds(r0, SUBLANES), S5_OCT_STATES:]
            for d, ar, ai in levels:
                sh = (SUBLANES - d) if reverse else d
                s_re = pltpu.roll(h_re, sh, 0)
                s_im = pltpu.roll(h_im, sh, 0)
                h_re, h_im = h_re + (ar * s_re - ai * s_im), h_im + (ar * s_im + ai * s_re)
            h_re, h_im = h_re + (cr * c_re - ci * c_im), h_im + (cr * c_im + ci * c_re)
            h_buf[pl.ds(r0, SUBLANES), :S5_OCT_STATES] = h_re
            h_buf[pl.ds(r0, SUBLANES), S5_OCT_STATES:] = h_im
            edge = 0 if reverse else SUBLANES - 1
            shape = (SUBLANES, S5_OCT_STATES)
            return (jnp.broadcast_to(h_re[edge:edge + 1, :], shape),
                    jnp.broadcast_to(h_im[edge:edge + 1, :], shape))

        c0 = (carry[:, lanes], carry[:, S5_OCT * S5_OCT_STATES + q * S5_OCT_STATES:
                                     S5_OCT * S5_OCT_STATES + (q + 1) * S5_OCT_STATES])
        c_re, c_im = lax.fori_loop(0, n_tiles, tile_step, c0)
        carry[:, lanes] = c_re
        carry[:, S5_OCT * S5_OCT_STATES + q * S5_OCT_STATES:
              S5_OCT * S5_OCT_STATES + (q + 1) * S5_OCT_STATES] = c_im
        y = _dot(h_buf[:, :S5_OCT_STATES].astype(bf16), wcr_ref[q])
        y = y - _dot(h_buf[:, S5_OCT_STATES:].astype(bf16), wci_ref[q])
        o_ref[:, q * LANES:(q + 1) * LANES] = y


def _s5_scan(proj, w_in, w_cre, w_cim, pw_re, pw_im, groups, reverse):
    t = proj.shape[0]
    n_blk = t // S5_TILE
    n_state = S5_OCT * S5_OCT_STATES
    order = (lambda i: (n_blk - 1 - i, 0)) if reverse else (lambda i: (i, 0))
    body = functools.partial(_s5_scan_body, groups=groups, reverse=reverse, n_blk=n_blk)
    return pl.pallas_call(
        body,
        grid=(n_blk,),
        in_specs=[
            pl.BlockSpec((S5_TILE, D_A), order),
            _const_spec((S5_OCT, LANES, 2 * S5_OCT_STATES)),
            _const_spec((S5_OCT, S5_OCT_STATES, LANES)),
            _const_spec((S5_OCT, S5_OCT_STATES, LANES)),
            _const_spec((SUBLANES, n_state)),
            _const_spec((SUBLANES, n_state)),
        ],
        out_specs=pl.BlockSpec((S5_TILE, D_A), order),
        out_shape=jax.ShapeDtypeStruct((t, D_A), f32),
        scratch_shapes=[
            pltpu.VMEM((S5_TILE, 2 * S5_OCT_STATES), f32),
            pltpu.VMEM((S5_TILE, 2 * S5_OCT_STATES), f32),
            pltpu.VMEM((SUBLANES, 2 * n_state), f32),
        ],
        compiler_params=_cparams("arbitrary"),
        name="s5_scan_rev" if reverse else "s5_scan_fwd",
    )(proj, w_in, w_cre, w_cim, pw_re, pw_im)


def _s5_post_body(u_ref, yf_ref, yb_ref, d_ref, w_ref, b_ref, o_ref):
    y = yf_ref[...] + yb_ref[...] + d_ref[...] * u_ref[...]
    y = jax.nn.gelu(y)
    gate = _dot(y.astype(bf16), w_ref[...]) + b_ref[...]
    o_ref[...] = y * jax.nn.sigmoid(gate)


def _s5_post(proj, y_f, y_b, d, glu_w, glu_b, j):
    t = proj.shape[0]
    tok = lambda i: (i, 0)
    return pl.pallas_call(
        _s5_post_body,
        grid=(t // TOKEN_TILE,),
        in_specs=[
            pl.BlockSpec((TOKEN_TILE, D_A), tok),
            pl.BlockSpec((TOKEN_TILE, D_A), tok),
            pl.BlockSpec((TOKEN_TILE, D_A), tok),
            _layer_spec((1, D_A))(j),
            _layer_spec((D_A, D_A))(j),
            _layer_spec((1, D_A))(j),
        ],
        out_specs=pl.BlockSpec((TOKEN_TILE, D_A), tok),
        out_shape=jax.ShapeDtypeStruct((t, D_A), f32),
        compiler_params=_cparams("parallel"),
        name="s5_post",
    )(proj, y_f, y_b, d, glu_w, glu_b)


def _s5_mixer(proj, prm, j, groups):
    pw_re, pw_im, bb_re, bb_im = _s5_params(prm["lam_re"][j], prm["lam_im"][j], prm["log_step"][j],
                                            prm["b_re"][j], prm["b_im"][j])
    ys = []
    for direction in range(2):
        w_in, w_cre, w_cim = _s5_block_mats(bb_re[direction], bb_im[direction],
                                            prm["c_re"][j, direction], prm["c_im"][j, direction])
        ys.append(_s5_scan(proj, w_in, w_cre, w_cim, pw_re[:, direction], pw_im[:, direction],
                           groups, reverse=(direction == 1)))
    return _s5_post(proj, ys[0], ys[1], prm["d"], prm["glu_w"], prm["glu_b"], j)


RW_TILE = 256
RW_CHUNK = 64
RW_PAIRS = RW_HEADS // 2
RW_CHUNKS_PER_STEP = 2
RW_PREC_A = "b1"
RW_PREC_INV = "b1"
RW_PREC_APPLY = "b1"
RW_PREC_SEQ = "b1"
RW_PREC_SUM = "x2"
RW_P_WIDTH = 2304
RW_LORA_OFF = 2048
RW_LORA_W = 256


def _head_ones():
    a = lax.broadcasted_iota(jnp.int32, (D_B, D_B), 0) // RW_HEAD
    b = lax.broadcasted_iota(jnp.int32, (D_B, D_B), 1) // RW_HEAD
    return jnp.where(a == b, 1.0, 0.0).astype(f32)


def _rw_prep_body(x_ref, xp_ref, xn_ref, mu_ref, lw_ref, w0_ref, a0_ref, kk_ref, ka_ref,
                  r_out, k_out, v_out, kk_out, kka_out, lwf_out, lwb_out, g_out, *, groups):
    i = pl.program_id(0)
    x = x_ref[:, D_A:]
    prev = jnp.where(_seq_edge(i, groups, RW_TILE, last=False), 0.0, xp_ref[:, D_A:])
    nxt = jnp.where(_seq_edge(i, groups, RW_TILE, last=True), 0.0, xn_ref[:, D_A:])
    row = lax.broadcasted_iota(jnp.int32, x.shape, 0)
    shifted = 0.5 * (_shift_rows(x, prev, 1, row) + _shift_rows(x, nxt, -1, row))
    p = x + (shifted - x) * mu_ref[:, D_A:]
    r = p[:, :D_B]
    k = p[:, D_B:2 * D_B]
    v = p[:, 2 * D_B:3 * D_B]
    lora = p[:, RW_LORA_OFF - D_A:RW_LORA_OFF - D_A + RW_LORA_W]
    lora_t = jnp.tanh(lora).astype(bf16)
    a = jax.nn.sigmoid(a0_ref[...] + _dot(lora.astype(bf16), lw_ref[2]))
    g = _dot(jax.nn.sigmoid(lora).astype(bf16), lw_ref[3])
    kx = k * kk_ref[...]
    ss = _mm(kx * kx, _head_ones(), RW_PREC_SUM)
    kk = kx / jnp.maximum(jnp.sqrt(ss), 1e-12)
    r_out[...] = r
    k_out[...] = k * (1.0 + (a - 1.0) * ka_ref[...])
    v_out[...] = v
    kk_out[...] = kk
    kka_out[...] = kk * a
    g_out[...] = g
    for direction, out in enumerate((lwf_out, lwb_out)):
        wl = -jax.nn.softplus(-(w0_ref[direction:direction + 1, :] + _dot(lora_t, lw_ref[direction]))) - 0.5
        out[...] = -jnp.exp(wl)


def _rw_prep(proj, prm, j, groups):
    t = proj.shape[0]
    tok = lambda i: (i, 0)
    prev_spec, next_spec = _halo_specs(RW_TILE, RW_P_WIDTH, 0, t)
    out = jax.ShapeDtypeStruct((t, D_B), f32)
    return pl.pallas_call(
        functools.partial(_rw_prep_body, groups=groups),
        grid=(t // RW_TILE,),
        in_specs=[pl.BlockSpec((RW_TILE, RW_P_WIDTH), tok), prev_spec, next_spec,
                  _layer_spec((1, RW_P_WIDTH))(j), _layer_spec((4, RW_LORA_W, D_B))(j),
                  _layer_spec((2, D_B))(j), _layer_spec((1, D_B))(j), _layer_spec((1, D_B))(j),
                  _layer_spec((1, D_B))(j)],
        out_specs=[pl.BlockSpec((RW_TILE, D_B), tok)] * 8,
        out_shape=[out] * 8,
        compiler_params=_cparams("parallel"),
        name="rw_prep",
    )(proj, proj, proj, prm["mu"], prm["lora_w"], prm["w0"], prm["a0"], prm["k_k"], prm["k_a"])


def _rw_chunk_body(r_ref, k_ref, v_ref, kk_ref, kka_ref, lw_ref, y_out, st, *, reverse, groups, n_steps):
    c = RW_CHUNK
    pid = pl.program_id(0)

    @pl.when(pid == 0)
    def _():
        st[...] = jnp.zeros_like(st)

    ri = lax.broadcasted_iota(jnp.int32, (c, c), 0)
    ci = lax.broadcasted_iota(jnp.int32, (c, c), 1)
    tri = jnp.where((ci >= ri) if reverse else (ci <= ri), 1.0, 0.0).astype(f32)
    edge = 0 if reverse else c - 1
    big_r = lax.broadcasted_iota(jnp.int32, (LANES, LANES), 0)
    big_c = lax.broadcasted_iota(jnp.int32, (LANES, LANES), 1)
    same_head = (big_r // RW_HEAD) == (big_c // RW_HEAD)
    s_idx, j_idx = big_r % RW_HEAD, big_c % RW_HEAD
    strict = same_head & ((j_idx > s_idx) if reverse else (j_idx < s_idx))
    incl = same_head & ((j_idx >= s_idx) if reverse else (j_idx <= s_idx))
    eye = big_r == big_c
    eye_f = jnp.where(eye, 1.0, 0.0).astype(f32)
    head0 = lax.broadcasted_iota(jnp.int32, (c, LANES), 1) < RW_HEAD

    def expand(x):
        return jnp.concatenate([jnp.where(head0, x, 0.0), jnp.where(head0, 0.0, x)], axis=0)

    def collapse(x):
        return x[:c] + x[c:]

    n_sub = r_ref.shape[0] // c
    lanes = [slice(p * LANES, (p + 1) * LANES) for p in range(RW_PAIRS)]
    rows = [slice(ch * c, (ch + 1) * c) for ch in range(n_sub)]
    units = [(ch, p) for ch in range(n_sub) for p in range(RW_PAIRS)]
    pre = []
    for rs in rows:
        lw = lw_ref[rs, :]
        cum = _dot_hi(tri, lw)
        total = cum[edge:edge + 1, :]
        e_neg, tail = jnp.exp(-cum), jnp.exp(total - cum)
        k, kka = k_ref[rs, :], kka_ref[rs, :]
        pre.append(dict(at=-kk_ref[rs, :] * jnp.exp(cum - lw), rt=r_ref[rs, :] * jnp.exp(cum),
                        kh=k * e_neg, bh=kka * e_neg, kp=k * tail, bp=kka * tail, v=v_ref[rs, :],
                        pc=jnp.exp(total)))

    def get(name):
        return [pre[ch][name][:, lanes[p]] for ch, p in units]

    at, rt, kh, bh, kp, bp, v = (get(n) for n in ("at", "rt", "kh", "bh", "kp", "bp", "v"))
    at_exp = [expand(x) for x in at]
    v_exp = [expand(x) for x in v]
    z = [_mm(jnp.concatenate([ae, expand(rr)], axis=0), jnp.concatenate([a, a, b, b], axis=0), RW_PREC_A, _NT)
         for ae, rr, a, b in zip(at_exp, rt, kh, bh)]
    a_ak = [jnp.where(strict, zz[:LANES, :LANES], 0.0) for zz in z]
    a_ab = [jnp.where(strict, zz[:LANES, LANES:], 0.0) for zz in z]
    a_rk = [jnp.where(incl, zz[LANES:, :LANES], 0.0) for zz in z]
    a_rb = [jnp.where(incl, zz[LANES:, LANES:], 0.0) for zz in z]
    tinv = [eye_f + a for a in a_ab]
    pw = a_ab
    for _ in range(int(math.log2(c)) - 1):
        pw = [_mm(q, q, RW_PREC_INV) for q in pw]
        tinv = [_mm(t, eye_f + q, RW_PREC_INV) for t, q in zip(tinv, pw)]
    w = [_mm(a, ve, RW_PREC_APPLY) for a, ve in zip(a_ak, v_exp)]
    tu = [_mm(t, jnp.concatenate([ae, ww], axis=1), RW_PREC_APPLY) for t, ae, ww in zip(tinv, at_exp, w)]
    atp_exp = [x[:, :LANES] for x in tu]
    u0_exp = [x[:, LANES:] for x in tu]
    y0 = [collapse(_mm(ark, ve, RW_PREC_APPLY) + _mm(arb, ue, RW_PREC_APPLY))
          for ark, arb, ve, ue in zip(a_rk, a_rb, v_exp, u0_exp)]
    rp = [rr + collapse(_mm(arb, ae, RW_PREC_APPLY)) for rr, arb, ae in zip(rt, a_rb, atp_exp)]
    atp = [collapse(x) for x in atp_exp]
    u0 = [collapse(x) for x in u0_exp]
    g_full = [_mm(b, a, RW_PREC_APPLY, _TN) for b, a in zip(bp, atp)]
    h_full = [_mm(kd, vv, RW_PREC_APPLY, _TN) + _mm(b, u, RW_PREC_APPLY, _TN)
              for kd, vv, b, u in zip(kp, v, bp, u0)]
    blk = (n_steps - 1 - pid) if reverse else pid
    state = [st[:, sl] for sl in lanes]
    for ch in (range(n_sub - 1, -1, -1) if reverse else range(n_sub)):
        fresh = _seq_edge(blk * n_sub + ch, groups, c, last=reverse)
        for p in range(RW_PAIRS):
            i, sl = ch * RW_PAIRS + p, lanes[p]
            pc = jnp.broadcast_to(pre[ch]["pc"][:, sl], (LANES, LANES))
            g_bd = jnp.where(same_head, g_full[i], 0.0) + jnp.where(eye, pc, 0.0)
            h_bd = jnp.where(same_head, h_full[i], 0.0)
            s0 = jnp.where(fresh, 0.0, state[p])
            y_out[rows[ch], sl] = _mm(rp[i], s0, RW_PREC_SEQ) + y0[i]
            state[p] = _mm(g_bd, s0, RW_PREC_SEQ) + h_bd
    for p in range(RW_PAIRS):
        st[:, lanes[p]] = state[p]


def _rw_chunk(r, k, v, kk, kka, lw, groups, reverse):
    t = r.shape[0]
    step_rows = RW_CHUNK * RW_CHUNKS_PER_STEP
    n_steps = t // step_rows
    order = (lambda i: (n_steps - 1 - i, 0)) if reverse else (lambda i: (i, 0))
    return pl.pallas_call(
        functools.partial(_rw_chunk_body, reverse=reverse, groups=groups, n_steps=n_steps),
        grid=(n_steps,),
        in_specs=[pl.BlockSpec((step_rows, D_B), order)] * 6,
        out_specs=pl.BlockSpec((step_rows, D_B), order),
        out_shape=jax.ShapeDtypeStruct((t, D_B), f32),
        scratch_shapes=[pltpu.VMEM((LANES, D_B), f32)],
        compiler_params=_cparams("arbitrary"),
        name="rw_chunk_rev" if reverse else "rw_chunk_fwd",
    )(r, k, v, kk, kka, lw)


def _rw_post_body(yf_ref, yb_ref, r_ref, k_ref, v_ref, g_ref, rk_ref, lnw_ref, lnb_ref, o_ref):
    ones = _head_ones()
    inv = 1.0 / RW_HEAD
    y = yf_ref[...] + yb_ref[...]
    mean = _mm(y, ones, RW_PREC_SUM) * inv
    yc = y - mean
    var = _mm(yc * yc, ones, RW_PREC_SUM) * inv
    yn = yc * lax.rsqrt(var + RW_GN_EPS) * lnw_ref[...] + lnb_ref[...]
    bonus = _mm(r_ref[...] * k_ref[...] * rk_ref[...], ones, RW_PREC_SUM) * v_ref[...]
    o_ref[...] = (yn + bonus) * g_ref[...]


def _rw_post(y_f, y_b, r, k, v, g, prm, j):
    t = y_f.shape[0]
    tok = pl.BlockSpec((TOKEN_TILE, D_B), lambda i: (i, 0))
    vec = _layer_spec((1, D_B))(j)
    return pl.pallas_call(
        _rw_post_body,
        grid=(t // TOKEN_TILE,),
        in_specs=[tok] * 6 + [vec] * 3,
        out_specs=tok,
        out_shape=jax.ShapeDtypeStruct((t, D_B), f32),
        compiler_params=_cparams("parallel"),
        name="rw_post",
    )(y_f, y_b, r, k, v, g, prm["r_k"], prm["ln_w"], prm["ln_b"])


def _rwkv7_mixer(proj, prm, j, groups):
    r, k, v, kk, kka, lw_f, lw_b, g = _rw_prep(proj, prm, j, groups)
    y_f = _rw_chunk(r, k, v, kk, kka, lw_f, groups, reverse=False)
    y_b = _rw_chunk(r, k, v, kk, kka, lw_b, groups, reverse=True)
    return _rw_post(y_f, y_b, r, k, v, g, prm, j)


def _halo_specs(tile, width, col_blk, n_rows):
    per = tile // SUBLANES
    last8 = n_rows // SUBLANES - 1
    prev = pl.BlockSpec((SUBLANES, width), lambda i: (jnp.maximum(i * per - 1, 0), col_blk))
    nxt = pl.BlockSpec((SUBLANES, width), lambda i: (jnp.minimum((i + 1) * per, last8), col_blk))
    return prev, nxt


def _shift_rows(x, halo, k, row):
    n = x.shape[0]
    if k > 0:
        y = pltpu.roll(x, k, 0)
        for r in range(k):
            y = jnp.where(row == r, halo[SUBLANES - k + r:SUBLANES - k + r + 1, :], y)
        return y
    y = pltpu.roll(x, n - 1, 0)
    return jnp.where(row == n - 1, halo[0:1, :], y)


LRU_TILE = 512
CD_XB_BLK = 3
CD_GB_BLK = 4


def _lru_gate_body(x_ref, xp_ref, xn_ref, cw_ref, cb_ref, lam_ref, w_ref, b_ref,
                   af_ref, bf_ref, ab_ref, bb_ref, *, groups):
    i = pl.program_id(0)
    first = _seq_edge(i, groups, LRU_TILE, last=False)
    last = _seq_edge(i, groups, LRU_TILE, last=True)
    x = x_ref[...]
    prev = jnp.where(first, 0.0, xp_ref[...])
    nxt = jnp.where(last, 0.0, xn_ref[...])
    row = lax.broadcasted_iota(jnp.int32, x.shape, 0)
    xc = (cw_ref[0:1, :] * _shift_rows(x, prev, 2, row) + cw_ref[1:2, :] * _shift_rows(x, prev, 1, row)
          + cw_ref[2:3, :] * x + cw_ref[3:4, :] * _shift_rows(x, nxt, -1, row) + cb_ref[...])
    pre = _dot(xc.astype(bf16), w_ref[...]) + b_ref[...]
    outs = ((af_ref, bf_ref), (ab_ref, bb_ref))
    for direction in range(2):
        base = 2 * direction * D_C
        gate_r = jax.nn.sigmoid(pre[:, base:base + D_C])
        gate_i = jax.nn.sigmoid(pre[:, base + D_C:base + 2 * D_C])
        log_a = -LRU_C * gate_r * jax.nn.softplus(-lam_ref[direction:direction + 1, :])
        t = jnp.tanh(log_a)
        mult = jnp.sqrt(-2.0 * t / (1.0 - t))
        a_ref, b_ref_out = outs[direction]
        a_ref[...] = jnp.exp(log_a)
        b_ref_out[...] = mult * gate_i * xc


def _lru_gates(proj, prm, j, groups):
    t = proj.shape[0]
    tok = lambda i: (i, 0)
    prev_spec, next_spec = _halo_specs(LRU_TILE, D_C, CD_XB_BLK, t)
    out = jax.ShapeDtypeStruct((t, D_C), f32)
    return pl.pallas_call(
        functools.partial(_lru_gate_body, groups=groups),
        grid=(t // LRU_TILE,),
        in_specs=[
            pl.BlockSpec((LRU_TILE, D_C), lambda i: (i, CD_XB_BLK)),
            prev_spec,
            next_spec,
            _layer_spec((LRU_CONV, D_C))(j),
            _layer_spec((1, D_C))(j),
            _layer_spec((2, D_C))(j),
            _layer_spec((D_C, 4 * D_C))(j),
            _layer_spec((1, 4 * D_C))(j),
        ],
        out_specs=[pl.BlockSpec((LRU_TILE, D_C), tok)] * 4,
        out_shape=[out] * 4,
        compiler_params=_cparams("parallel"),
        name="lru_gates",
    )(proj, proj, proj, prm["conv_w"], prm["conv_b"], prm["lam"], prm["w_gates"], prm["b_gates"])


def _lru_scan_body(af_ref, bf_ref, ab_ref, bb_ref, hf_ref, hb_ref, carry_f, carry_b, *, groups, n_blk):
    i = pl.program_id(0)

    @pl.when(_seq_edge(i, groups, LRU_TILE, last=False))
    def _():
        carry_f[...] = jnp.zeros_like(carry_f)

    @pl.when(_seq_edge(n_blk - 1 - i, groups, LRU_TILE, last=True))
    def _():
        carry_b[...] = jnp.zeros_like(carry_b)

    row = lax.broadcasted_iota(jnp.int32, (SUBLANES, D_C), 0)
    n_tiles = LRU_TILE // SUBLANES

    def run(a_ref, b_ref, h_ref, carry, reverse):
        def tile_step(k, c):
            t = (n_tiles - 1 - k) if reverse else k
            r0 = pl.multiple_of(t * SUBLANES, SUBLANES)
            a = a_ref[pl.ds(r0, SUBLANES), :]
            b = b_ref[pl.ds(r0, SUBLANES), :]
            for d in (1, 2, 4):
                sh = (SUBLANES - d) if reverse else d
                keep = (row < SUBLANES - d) if reverse else (row >= d)
                a_s = jnp.where(keep, pltpu.roll(a, sh, 0), 1.0)
                b_s = jnp.where(keep, pltpu.roll(b, sh, 0), 0.0)
                b = b + a * b_s
                a = a * a_s
            h = b + a * c
            h_ref[pl.ds(r0, SUBLANES), :] = h
            edge = 0 if reverse else SUBLANES - 1
            return jnp.broadcast_to(h[edge:edge + 1, :], (SUBLANES, D_C))

        carry[...] = lax.fori_loop(0, n_tiles, tile_step, carry[...])

    run(af_ref, bf_ref, hf_ref, carry_f, False)
    run(ab_ref, bb_ref, hb_ref, carry_b, True)


def _lru_scan(a_f, b_f, a_b, b_b, groups):
    t = a_f.shape[0]
    n_blk = t // LRU_TILE
    fwd = lambda i: (i, 0)
    bwd = lambda i: (n_blk - 1 - i, 0)
    out = jax.ShapeDtypeStruct((t, D_C), f32)
    return pl.pallas_call(
        functools.partial(_lru_scan_body, groups=groups, n_blk=n_blk),
        grid=(n_blk,),
        in_specs=[pl.BlockSpec((LRU_TILE, D_C), fwd), pl.BlockSpec((LRU_TILE, D_C), fwd),
                  pl.BlockSpec((LRU_TILE, D_C), bwd), pl.BlockSpec((LRU_TILE, D_C), bwd)],
        out_specs=[pl.BlockSpec((LRU_TILE, D_C), fwd), pl.BlockSpec((LRU_TILE, D_C), bwd)],
        out_shape=[out, out],
        scratch_shapes=[pltpu.VMEM((SUBLANES, D_C), f32), pltpu.VMEM((SUBLANES, D_C), f32)],
        compiler_params=_cparams("arbitrary"),
        name="lru_scan",
    )(a_f, b_f, a_b, b_b)


HY_N2 = 256
HY_CW = 128
HY_SC_TILE = 512
HY_MLP_TILE = 512
HY_PREC_S1 = "b1"
HY_PREC_S2 = "b1"
HY_S1_UNROLL = 2


def _hy_dims(seq_len):
    n = 2 * seq_len
    n1 = n // HY_N2
    lb = n1 // 2
    k1 = n1 // 2 + 1
    k1p = -(-k1 // SUBLANES) * SUBLANES
    return n, n1, lb, k1, k1p


def _hy_tables(seq_len):
    n, n1, lb, k1, k1p = _hy_dims(seq_len)
    two_pi = 2.0 * math.pi
    kk = np.arange(k1, dtype=np.float64)[:, None]
    nn = np.arange(lb, dtype=np.float64)[None, :]
    ang1 = two_pi * kk * nn / n1
    f1 = np.zeros((2 * k1p, lb), np.float64)
    f1[:k1] = np.cos(ang1)
    f1[k1p:k1p + k1] = -np.sin(ang1)
    weight = np.where((np.arange(k1) == 0) | (np.arange(k1) == n1 // 2), 1.0, 2.0)[None, :] / n
    f1inv = np.zeros((lb, 2 * k1p), np.float64)
    f1inv[:, :k1] = weight * np.cos(ang1.T)
    f1inv[:, k1p:k1p + k1] = -weight * np.sin(ang1.T)
    idx = np.arange(HY_N2, dtype=np.float64)
    ang2 = two_pi * np.outer(idx, idx) / HY_N2
    f2 = np.concatenate([np.cos(ang2), -np.sin(ang2)], axis=0)
    prod = (jnp.arange(k1, dtype=jnp.int32)[:, None] * jnp.arange(HY_N2, dtype=jnp.int32)[None, :]) % n
    ang = prod.astype(f32) * f32(two_pi / n)
    bc = lambda a: jnp.broadcast_to(a.reshape(k1 * HY_N2, 1), (k1 * HY_N2, HY_CW))
    eye8 = np.eye(SUBLANES)
    return dict(f1=jnp.asarray(np.kron(f1, eye8), f32), f1inv=jnp.asarray(np.kron(f1inv, eye8), f32),
                f2=jnp.asarray(f2, f32), tw_re=bc(jnp.cos(ang)), tw_im=bc(-jnp.sin(ang)))


def _hy_forward(z_ref, f1_ref, f2_ref, twr_ref, twi_ref, wr, wi, dims):
    _, _, lb, k1, k1p = dims

    def stage1_group(n2):
        tiles = [z_ref[pl.ds(pl.multiple_of(b * HY_N2 + n2, SUBLANES), SUBLANES), :] for b in range(lb)]
        y = _mm(f1_ref[...], jnp.concatenate(tiles, axis=0), HY_PREC_S1)
        for s in range(k1p):
            dst = pl.ds(pl.multiple_of(s * HY_N2 + n2, SUBLANES), SUBLANES)
            wr[dst, :] = y[s * SUBLANES:(s + 1) * SUBLANES]
            wi[dst, :] = y[(k1p + s) * SUBLANES:(k1p + s + 1) * SUBLANES]

    def stage1(g, carry):
        for u in range(HY_S1_UNROLL):
            stage1_group((g * HY_S1_UNROLL + u) * SUBLANES)
        return carry

    lax.fori_loop(0, HY_N2 // (SUBLANES * HY_S1_UNROLL), stage1, 0)

    def stage2(k, carry):
        r0 = pl.multiple_of(k * HY_N2, HY_N2)
        sl = pl.ds(r0, HY_N2)
        yr, yi, tr, ti = wr[sl, :], wi[sl, :], twr_ref[sl, :], twi_ref[sl, :]
        ar = yr * tr - yi * ti
        ai = yr * ti + yi * tr
        pq = _mm(f2_ref[...], jnp.concatenate([ar, ai], axis=1), HY_PREC_S2)
        p, q = pq[:, :HY_CW], pq[:, HY_CW:]
        wr[sl, :] = p[:HY_N2] - q[HY_N2:]
        wi[sl, :] = p[HY_N2:] + q[:HY_N2]
        return carry

    lax.fori_loop(0, k1, stage2, 0)


def _hy_inverse(hr_ref, hi_ref, f1inv_ref, f2_ref, twr_ref, twi_ref, wr, wi, o_ref, dims):
    _, _, lb, k1, k1p = dims

    def stage2(k, carry):
        r0 = pl.multiple_of(k * HY_N2, HY_N2)
        sl = pl.ds(r0, HY_N2)
        xr, xi, hr, hi = wr[sl, :], wi[sl, :], hr_ref[sl, :], hi_ref[sl, :]
        zr = xr * hr - xi * hi
        zi = xr * hi + xi * hr
        pq = _mm(f2_ref[...], jnp.concatenate([zr, zi], axis=1), HY_PREC_S2)
        p, q = pq[:, :HY_CW], pq[:, HY_CW:]
        vr = p[:HY_N2] + q[HY_N2:]
        vi = q[:HY_N2] - p[HY_N2:]
        tr, ti = twr_ref[sl, :], twi_ref[sl, :]
        wr[sl, :] = vr * tr + vi * ti
        wi[sl, :] = vi * tr - vr * ti
        return carry

    lax.fori_loop(0, k1, stage2, 0)

    def stage1_group(n2):
        src = [pl.ds(pl.multiple_of(s * HY_N2 + n2, SUBLANES), SUBLANES) for s in range(k1p)]
        tiles = [wr[d, :] for d in src] + [wi[d, :] for d in src]
        x = _mm(f1inv_ref[...], jnp.concatenate(tiles, axis=0), HY_PREC_S1)
        for b in range(lb):
            dst = pl.ds(pl.multiple_of(b * HY_N2 + n2, SUBLANES), SUBLANES)
            o_ref[dst, :] = x[b * SUBLANES:(b + 1) * SUBLANES]

    def stage1(g, carry):
        for u in range(HY_S1_UNROLL):
            stage1_group((g * HY_S1_UNROLL + u) * SUBLANES)
        return carry

    lax.fori_loop(0, HY_N2 // (SUBLANES * HY_S1_UNROLL), stage1, 0)


def _hy_shortconv_body(x_ref, xp_ref, xn_ref, w_ref, b_ref, o_ref, *, groups):
    i = pl.program_id(0)
    x = x_ref[...]
    prev = jnp.where(_seq_edge(i, groups, HY_SC_TILE, last=False), 0.0, xp_ref[...])
    nxt = jnp.where(_seq_edge(i, groups, HY_SC_TILE, last=True), 0.0, xn_ref[...])
    row = lax.broadcasted_iota(jnp.int32, x.shape, 0)
    o_ref[...] = (w_ref[0:1, :] * _shift_rows(x, prev, 1, row) + w_ref[1:2, :] * x
                  + w_ref[2:3, :] * _shift_rows(x, nxt, -1, row) + b_ref[...])


def _hy_shortconv(proj, conv_w, conv_b, j, groups):
    t = proj.shape[0]
    width = 3 * D_D
    prev_spec, next_spec = _halo_specs(HY_SC_TILE, width, 0, t)
    return pl.pallas_call(
        functools.partial(_hy_shortconv_body, groups=groups),
        grid=(t // HY_SC_TILE,),
        in_specs=[pl.BlockSpec((HY_SC_TILE, width), lambda i: (i, 0)), prev_spec, next_spec,
                  _layer_spec((3, width))(j), _layer_spec((1, width))(j)],
        out_specs=pl.BlockSpec((HY_SC_TILE, width), lambda i: (i, 0)),
        out_shape=jax.ShapeDtypeStruct((t, width), f32),
        compiler_params=_cparams("parallel"),
        name="hy_shortconv",
    )(proj, proj, proj, conv_w, conv_b)


def _hy_mlp_body(z_ref, w1_ref, b1_ref, w2_ref, b2_ref, fr_ref, w3_ref, dl_ref, o_ref):
    z = z_ref[...]
    h = jnp.sin(fr_ref[0:1, :] * (_dot_hi(z, w1_ref[...]) + b1_ref[...]))
    h = jnp.sin(fr_ref[1:2, :] * (_dot_hi(h, w2_ref[...]) + b2_ref[...]))
    h = _dot_hi(h, w3_ref[...])
    o_ref[...] = h * jnp.exp(-z[:, 0:1] * dl_ref[...])


def _hy_mlp(seq_len, w1, b1, w2, b2, freq, w3):
    t = jnp.linspace(0.0, 1.0, seq_len, dtype=f32)[:, None]
    bands = (HY_POS_EMB - 1) // 2
    freqs = jnp.linspace(1e-4, bands - 1, bands, dtype=f32)[None, :]
    wpos = (2.0 * math.pi / seq_len) * jnp.arange(seq_len, dtype=f32)[:, None]
    z = jnp.concatenate([t, jnp.cos(freqs * wpos), -jnp.sin(freqs * wpos)], axis=-1)
    z = jnp.pad(z, ((0, 0), (0, LANES - HY_POS_EMB)))
    w1 = jnp.pad(w1, ((0, LANES - HY_POS_EMB), (0, 0)))
    max_decay = math.log(1e-2) / 0.3
    min_decay = math.log(1e-2) / 1.5
    deltas = jnp.abs(jnp.linspace(min_decay, max_decay, D_D, dtype=f32))
    deltas = jnp.tile(deltas, 4)[None, :]
    width = 4 * D_D
    hid = HY_FILTER_HIDDEN
    return pl.pallas_call(
        _hy_mlp_body,
        grid=(seq_len // HY_MLP_TILE,),
        in_specs=[pl.BlockSpec((HY_MLP_TILE, LANES), lambda i: (i, 0)),
                  _const_spec((LANES, hid)), _const_spec((1, hid)), _const_spec((hid, hid)),
                  _const_spec((1, hid)), _const_spec((2, hid)), _const_spec((hid, width)),
                  _const_spec((1, width))],
        out_specs=pl.BlockSpec((HY_MLP_TILE, width), lambda i: (i, 0)),
        out_shape=jax.ShapeDtypeStruct((seq_len, width), f32),
        compiler_params=_cparams("parallel"),
        name="hy_mlp",
    )(z, w1, b1[None, :], w2, b2[None, :], freq, w3, deltas)


def _hy_spectrum_body(hf_ref, hb_ref, f1_ref, f2_ref, twr_ref, twi_ref, or_ref, oi_ref,
                      zb, wr, wi, *, dims):
    k1 = dims[3]
    rows = k1 * HY_N2
    hf = hf_ref[...]
    hb = hb_ref[...]
    row = lax.broadcasted_iota(jnp.int32, hb.shape, 0)
    hb0 = jnp.where(row == 0, 0.0, hb)
    scale = 1.0 / (jnp.sum(jnp.abs(hf), axis=0, keepdims=True) + jnp.sum(jnp.abs(hb0), axis=0, keepdims=True))
    _hy_forward(hf_ref, f1_ref, f2_ref, twr_ref, twi_ref, wr, wi, dims)
    or_ref[...] = wr[:rows, :] * scale
    oi_ref[...] = wi[:rows, :] * scale
    zb[...] = hb0
    _hy_forward(zb, f1_ref, f2_ref, twr_ref, twi_ref, wr, wi, dims)
    or_ref[...] = or_ref[...] + wr[:rows, :] * scale
    oi_ref[...] = oi_ref[...] - wi[:rows, :] * scale


def _hy_spectrum(taps, tables, seq_len):
    dims = _hy_dims(seq_len)
    _, _, lb, k1, k1p = dims
    n_ct = D_D // HY_CW
    rows = k1 * HY_N2
    out = jax.ShapeDtypeStruct((2, rows, D_D), f32)
    return pl.pallas_call(
        functools.partial(_hy_spectrum_body, dims=dims),
        grid=(2, n_ct),
        in_specs=[pl.BlockSpec((seq_len, HY_CW), lambda o, c: (0, o * n_ct + c), pipeline_mode=pl.Buffered(1)),
                  pl.BlockSpec((seq_len, HY_CW), lambda o, c: (0, (2 + o) * n_ct + c),
                               pipeline_mode=pl.Buffered(1)),
                  _const_spec((2 * k1p * SUBLANES, lb * SUBLANES)), _const_spec((2 * HY_N2, HY_N2)),
                  _const_spec((rows, HY_CW)), _const_spec((rows, HY_CW))],
        out_specs=[pl.BlockSpec((None, rows, HY_CW), lambda o, c: (o, 0, c))] * 2,
        out_shape=[out, out],
        scratch_shapes=[pltpu.VMEM((seq_len, HY_CW), f32),
                        pltpu.VMEM((k1p * HY_N2, HY_CW), f32), pltpu.VMEM((k1p * HY_N2, HY_CW), f32)],
        compiler_params=_cparams("arbitrary", "arbitrary"),
        name="hy_spectrum",
    )(taps, taps, tables["f1"], tables["f2"], tables["tw_re"], tables["tw_im"])


def _hy_conv_body(z_ref, g_ref, hr_ref, hi_ref, bias_ref, f1_ref, f1inv_ref, f2_ref, twr_ref, twi_ref,
                  o_ref, wr, wi, *, dims):
    _hy_forward(z_ref, f1_ref, f2_ref, twr_ref, twi_ref, wr, wi, dims)
    _hy_inverse(hr_ref, hi_ref, f1inv_ref, f2_ref, twr_ref, twi_ref, wr, wi, o_ref, dims)
    bias = bias_ref[...]
    n_chunks = z_ref.shape[0] // HY_N2

    def gate(c, carry):
        sl = pl.ds(pl.multiple_of(c * HY_N2, HY_N2), HY_N2)
        o_ref[sl, :] = g_ref[sl, :] * (o_ref[sl, :] + bias * z_ref[sl, :])
        return carry

    lax.fori_loop(0, n_chunks, gate, 0)


def _hy_conv(z, z_off, z_col, gate, gate_off, gate_col, h_re, h_im, bias, tables, order, j, n_seq, seq_len):
    dims = _hy_dims(seq_len)
    _, _, lb, k1, k1p = dims
    n_ct = D_D // HY_CW
    rows = k1 * HY_N2
    z0, g0 = z_off // seq_len, gate_off // seq_len
    one = pl.Buffered(1)
    return pl.pallas_call(
        functools.partial(_hy_conv_body, dims=dims),
        grid=(n_ct, n_seq),
        in_specs=[
            pl.BlockSpec((seq_len, HY_CW), lambda c, b: (z0 + b, z_col * n_ct + c), pipeline_mode=one),
            pl.BlockSpec((seq_len, HY_CW), lambda c, b: (g0 + b, gate_col * n_ct + c), pipeline_mode=one),
            pl.BlockSpec((None, rows, HY_CW), lambda c, b: (order, 0, c), pipeline_mode=one),
            pl.BlockSpec((None, rows, HY_CW), lambda c, b: (order, 0, c), pipeline_mode=one),
            pl.BlockSpec((None, None, 1, HY_CW), lambda c, b: (j, order, 0, c)),
            _const_spec((2 * k1p * SUBLANES, lb * SUBLANES)), _const_spec((lb * SUBLANES, 2 * k1p * SUBLANES)), _const_spec((2 * HY_N2, HY_N2)),
            _const_spec((rows, HY_CW)), _const_spec((rows, HY_CW)),
        ],
        out_specs=pl.BlockSpec((seq_len, HY_CW), lambda c, b: (b, c)),
        out_shape=jax.ShapeDtypeStruct((n_seq * seq_len, D_D), f32),
        scratch_shapes=[pltpu.VMEM((k1p * HY_N2, HY_CW), f32), pltpu.VMEM((k1p * HY_N2, HY_CW), f32)],
        compiler_params=_cparams("arbitrary", "arbitrary"),
        name="hy_conv",
    )(z, gate, h_re, h_im, bias, tables["f1"], tables["f1inv"], tables["f2"], tables["tw_re"], tables["tw_im"])


def _hyena_mixer(proj, prm, j, groups):
    pc = _hy_shortconv(proj, prm["conv_w"], prm["conv_b"], j, groups)
    outs = []
    for tok_off, n_seq, seq_len in groups:
        tables = _hy_tables(seq_len)
        taps = _hy_mlp(seq_len, prm["f_w1"][j], prm["f_b1"][j], prm["f_w2"][j], prm["f_b2"][j],
                       prm["f_freq"][j], prm["f_w3"][j])
        h_re, h_im = _hy_spectrum(taps, tables, seq_len)
        z1 = _hy_conv(pc, tok_off, 0, pc, tok_off, 1, h_re, h_im, prm["bias"], tables, 0, j, n_seq, seq_len)
        y = _hy_conv(z1, 0, 0, pc, tok_off, 2, h_re, h_im, prm["bias"], tables, 1, j, n_seq, seq_len)
        outs.append(y)
    return jnp.concatenate(outs, axis=0)


def _outproj_odd_body(x_ref, hf_ref, hb_ref, gb_ref, yd_ref, w_ref, o_ref):
    y_c = (hf_ref[...] + hb_ref[...]) * jax.nn.gelu(gb_ref[...])
    acc = _dot(y_c.astype(bf16), w_ref[:D_C, :])
    acc = acc + _dot(yd_ref[...].astype(bf16), w_ref[D_C:, :])
    o_ref[...] = x_ref[...] + acc


def _outproj_odd(x, h_f, h_b, proj, y_d, w, j):
    t = x.shape[0]
    tok = lambda i: (i, 0)
    return pl.pallas_call(
        _outproj_odd_body,
        grid=(t // TOKEN_TILE,),
        in_specs=[
            pl.BlockSpec((TOKEN_TILE, D_MODEL), tok),
            pl.BlockSpec((TOKEN_TILE, D_C), tok),
            pl.BlockSpec((TOKEN_TILE, D_C), tok),
            pl.BlockSpec((TOKEN_TILE, D_C), lambda i: (i, CD_GB_BLK)),
            pl.BlockSpec((TOKEN_TILE, D_D), tok),
            _layer_spec((D_C + D_D, D_MODEL))(j),
        ],
        out_specs=pl.BlockSpec((TOKEN_TILE, D_MODEL), tok),
        out_shape=jax.ShapeDtypeStruct((t, D_MODEL), f32),
        compiler_params=_cparams("parallel"),
        name="outproj_odd",
    )(x, h_f, h_b, proj, y_d, w)


def _block_diag_heads(w):
    n, h, a, b = w.shape
    return jnp.einsum("nhij,hk->nhikj", w, jnp.eye(h, dtype=w.dtype)).reshape(n, h * a, h * b)


def _pad_rows(w, lo, total):
    return jnp.pad(w, ((0, 0), (lo, total - lo - w.shape[1]), (0, 0)))


def kernel(x_prompt, x_sample, ffn1_norm, ffn1_w_gate, ffn1_w_up, ffn1_w_down, mix_norm, ffn2_norm, ffn2_w_gate, ffn2_w_up, ffn2_w_down, ab_w_in, ab_w_out, s5_lambda_re, s5_lambda_im, s5_log_step, s5_b_re, s5_b_im, s5_c_re, s5_c_im, s5_d, s5_glu_w, s5_glu_b, rw_mu, rw_w0, rw_w_up, rw_a0, rw_a_up, rw_g_up, rw_k_k, rw_k_a, rw_r_k, rw_ln_w, rw_ln_b, cd_w_in, cd_w_out, lru_conv_w, lru_conv_b, lru_lambda, lru_wa, lru_ba, lru_wx, lru_bx, hy_conv_w, hy_conv_b, hy_f_w1, hy_f_b1, hy_f_w2, hy_f_b2, hy_f_freq, hy_f_w3, hy_bias, final_norm):
    n_p, l_p, _ = x_prompt.shape
    n_s, l_s, _ = x_sample.shape
    t_p = n_p * l_p
    groups = ((0, n_p, l_p), (t_p, n_s, l_s))
    x = jnp.concatenate([x_prompt.reshape(t_p, D_MODEL), x_sample.reshape(n_s * l_s, D_MODEL)], axis=0)

    row = lambda a: a[:, None, :]
    cast = lambda a: a.astype(bf16)
    ffn1 = (row(ffn1_norm), cast(ffn1_w_gate), cast(ffn1_w_up), cast(ffn1_w_down))
    ffn2 = (row(ffn2_norm), cast(ffn2_w_gate), cast(ffn2_w_up), cast(ffn2_w_down))
    mix_g = row(mix_norm)

    ab_in = cast(jnp.pad(ab_w_in, ((0, 0), (0, 0), (0, RW_P_WIDTH - ab_w_in.shape[-1]))))
    ab_out = cast(ab_w_out)
    s5 = dict(lam_re=s5_lambda_re, lam_im=s5_lambda_im, log_step=s5_log_step, b_re=s5_b_re, b_im=s5_b_im,
              c_re=s5_c_re, c_im=s5_c_im, d=row(s5_d), glu_w=cast(s5_glu_w), glu_b=row(s5_glu_b))
    lo = RW_DECAY_LORA
    lora_w = jnp.stack([_pad_rows(rw_w_up[:, 0], 0, RW_LORA_W), _pad_rows(rw_w_up[:, 1], lo, RW_LORA_W),
                        _pad_rows(rw_a_up, 2 * lo, RW_LORA_W),
                        _pad_rows(rw_g_up, 2 * lo + RW_A_LORA, RW_LORA_W)], axis=1)
    rw = dict(mu=row(jnp.pad(rw_mu, ((0, 0), (D_A, RW_P_WIDTH - D_A - rw_mu.shape[-1])))),
              lora_w=cast(lora_w), w0=rw_w0, a0=row(rw_a0), k_k=row(rw_k_k), k_a=row(rw_k_a),
              r_k=rw_r_k.reshape(-1, 1, D_B), ln_w=row(rw_ln_w), ln_b=row(rw_ln_b))

    cd_in = cast(jnp.concatenate([cd_w_in[..., 2 * D_C:], cd_w_in[..., :2 * D_C]], axis=-1))
    cd_out = cast(cd_w_out)
    w_gates = jnp.concatenate([_block_diag_heads(lru_wa[:, 0]), _block_diag_heads(lru_wx[:, 0]),
                               _block_diag_heads(lru_wa[:, 1]), _block_diag_heads(lru_wx[:, 1])], axis=-1)
    b_gates = jnp.concatenate([lru_ba[:, 0], lru_bx[:, 0], lru_ba[:, 1], lru_bx[:, 1]], axis=-1)
    lru = dict(conv_w=lru_conv_w, conv_b=row(lru_conv_b), lam=lru_lambda, w_gates=cast(w_gates),
               b_gates=row(b_gates))
    hy = dict(conv_w=hy_conv_w, conv_b=row(hy_conv_b), f_w1=hy_f_w1, f_b1=hy_f_b1, f_w2=hy_f_w2,
              f_b2=hy_f_b2, f_freq=hy_f_freq, f_w3=hy_f_w3, bias=hy_bias[:, :, None, :])

    for layer in range(DEPTH):
        j = layer // 2
        x = _ffn(x, *ffn1, layer)
        if layer % 2 == 0:
            proj = _proj(x, mix_g, ab_in, layer, j)
            y_a = _s5_mixer(proj, s5, j, groups)
            y_b = _rwkv7_mixer(proj, rw, j, groups)
            x = _outproj(x, y_a, y_b, ab_out, j)
        else:
            proj = _proj(x, mix_g, cd_in, layer, j)
            a_f, b_f, a_b, b_b = _lru_gates(proj, lru, j, groups)
            h_f, h_b = _lru_scan(a_f, b_f, a_b, b_b, groups)
            y_d = _hyena_mixer(proj, hy, j, groups)
            x = _outproj_odd(x, h_f, h_b, proj, y_d, cd_out, j)
        x = _ffn(x, *ffn2, layer)
    y = _final_norm(x, final_norm[None, :])
    return (y[:t_p].reshape(n_p, l_p, D_MODEL), y[t_p:].reshape(n_s, l_s, D_MODEL))
```

```python
import functools
import math

import numpy as np
import jax
import jax.numpy as jnp
from jax import lax
from jax.experimental import pallas as pl
from jax.experimental.pallas import tpu as pltpu

f32 = jnp.float32
bf16 = jnp.bfloat16

D_MODEL = 1024
DEPTH = 4
D_FF = 2816
RMS_EPS = 1e-6
D_A = 512
S5_GROUP = 16
S5_GROUPS = 32
S5_STATE = 64
D_B = 512
RW_HEAD = 64
RW_HEADS = 8
RW_DECAY_LORA = 32
RW_A_LORA = 32
RW_GATE_LORA = 64
RW_GN_EPS = 64e-5
D_C = 512
LRU_HEADS = 8
LRU_HEAD_DIM = 64
LRU_CONV = 4
LRU_C = 8.0
D_D = 512
HY_POS_EMB = 33
HY_FILTER_HIDDEN = 64

LANES = 128
SUBLANES = 8
VMEM_LIMIT = 56 * 1024 * 1024
TOKEN_TILE = 512
FF_CHUNK = 1408


def _cparams(*sem):
    return pltpu.CompilerParams(dimension_semantics=tuple(sem), vmem_limit_bytes=VMEM_LIMIT)


def _const_spec(shape):
    nd = len(shape)
    return pl.BlockSpec(shape, lambda *_: (0,) * nd, pipeline_mode=pl.Buffered(1))


def _layer_spec(shape):
    def make(layer):
        nd = len(shape)
        return pl.BlockSpec((None,) + tuple(shape), lambda *_: (layer,) + (0,) * nd,
                            pipeline_mode=pl.Buffered(1))
    return make


def _rms(x, g):
    ms = jnp.mean(x * x, axis=-1, keepdims=True)
    return x * lax.rsqrt(ms + RMS_EPS) * g


def _dot(a, b):
    return jnp.dot(a, b, preferred_element_type=f32)


_HI = lax.Precision.HIGHEST
_NN = (((1,), (0,)), ((), ()))
_NT = (((1,), (1,)), ((), ()))
_TN = (((0,), (0,)), ((), ()))


def _mm(a, b, prec, dims=_NN):
    dg = functools.partial(lax.dot_general, dimension_numbers=dims, preferred_element_type=f32)
    if prec == "hi":
        return dg(a, b, precision=_HI)
    a1, b1 = a.astype(bf16), b.astype(bf16)
    if prec == "b1":
        return dg(a1, b1)
    a2 = (a - a1.astype(f32)).astype(bf16)
    if prec == "x2":
        return dg(a1, b1) + dg(a2, b1)
    b2 = (b - b1.astype(f32)).astype(bf16)
    return dg(a1, b1) + (dg(a1, b2) + dg(a2, b1))


def _dot_hi(a, b):
    return _mm(a, b, "hi")


def _ffn_body(x_ref, g_ref, wg_ref, wu_ref, wd_ref, o_ref):
    x = x_ref[...]
    h = _rms(x, g_ref[...]).astype(bf16)
    acc = None
    for c in range(D_FF // FF_CHUNK):
        sl = slice(c * FF_CHUNK, (c + 1) * FF_CHUNK)
        gate = _dot(h, wg_ref[:, sl])
        up = _dot(h, wu_ref[:, sl])
        act = (gate * jax.nn.sigmoid(gate) * up).astype(bf16)
        part = _dot(act, wd_ref[sl, :])
        acc = part if acc is None else acc + part
    o_ref[...] = x + 0.5 * acc


def _ffn(x, norm, wg, wu, wd, layer):
    t = x.shape[0]
    return pl.pallas_call(
        _ffn_body,
        grid=(t // TOKEN_TILE,),
        in_specs=[
            pl.BlockSpec((TOKEN_TILE, D_MODEL), lambda i: (i, 0)),
            _layer_spec((1, D_MODEL))(layer),
            _layer_spec((D_MODEL, D_FF))(layer),
            _layer_spec((D_MODEL, D_FF))(layer),
            _layer_spec((D_FF, D_MODEL))(layer),
        ],
        out_specs=pl.BlockSpec((TOKEN_TILE, D_MODEL), lambda i: (i, 0)),
        out_shape=jax.ShapeDtypeStruct((t, D_MODEL), f32),
        compiler_params=_cparams("parallel"),
        name="ffn",
    )(x, norm, wg, wu, wd)


def _proj_body(x_ref, g_ref, w_ref, o_ref):
    h = _rms(x_ref[...], g_ref[...]).astype(bf16)
    o_ref[...] = _dot(h, w_ref[...])


def _proj(x, norm, w, layer, j):
    t = x.shape[0]
    p = w.shape[-1]
    return pl.pallas_call(
        _proj_body,
        grid=(t // TOKEN_TILE,),
        in_specs=[
            pl.BlockSpec((TOKEN_TILE, D_MODEL), lambda i: (i, 0)),
            _layer_spec((1, D_MODEL))(layer),
            _layer_spec((D_MODEL, p))(j),
        ],
        out_specs=pl.BlockSpec((TOKEN_TILE, p), lambda i: (i, 0)),
        out_shape=jax.ShapeDtypeStruct((t, p), f32),
        compiler_params=_cparams("parallel"),
        name="proj",
    )(x, norm, w)


def _outproj_body(x_ref, ya_ref, yb_ref, w_ref, o_ref):
    half = w_ref.shape[0] // 2
    acc = _dot(ya_ref[...].astype(bf16), w_ref[:half, :])
    acc = acc + _dot(yb_ref[...].astype(bf16), w_ref[half:, :])
    o_ref[...] = x_ref[...] + acc


def _outproj(x, ya, yb, w, j):
    t = x.shape[0]
    half = ya.shape[-1]
    return pl.pallas_call(
        _outproj_body,
        grid=(t // TOKEN_TILE,),
        in_specs=[
            pl.BlockSpec((TOKEN_TILE, D_MODEL), lambda i: (i, 0)),
            pl.BlockSpec((TOKEN_TILE, half), lambda i: (i, 0)),
            pl.BlockSpec((TOKEN_TILE, half), lambda i: (i, 0)),
            _layer_spec((2 * half, D_MODEL))(j),
        ],
        out_specs=pl.BlockSpec((TOKEN_TILE, D_MODEL), lambda i: (i, 0)),
        out_shape=jax.ShapeDtypeStruct((t, D_MODEL), f32),
        compiler_params=_cparams("parallel"),
        name="outproj",
    )(x, ya, yb, w)


def _final_norm_body(x_ref, g_ref, o_ref):
    o_ref[...] = _rms(x_ref[...], g_ref[...])


def _final_norm(x, g):
    t = x.shape[0]
    return pl.pallas_call(
        _final_norm_body,
        grid=(t // TOKEN_TILE,),
        in_specs=[pl.BlockSpec((TOKEN_TILE, D_MODEL), lambda i: (i, 0)), _const_spec((1, D_MODEL))],
        out_specs=pl.BlockSpec((TOKEN_TILE, D_MODEL), lambda i: (i, 0)),
        out_shape=jax.ShapeDtypeStruct((t, D_MODEL), f32),
        compiler_params=_cparams("parallel"),
        name="final_norm",
    )(x, g)


def _seq_edge(blk, groups, tile, last):
    hit = None
    for off, n_seq, seq_len in groups:
        start, per, n = off // tile, seq_len // tile, n_seq * (seq_len // tile)
        rel = blk - start
        edge = (per - 1) if last else 0
        h = (rel >= 0) & (rel < n) & (lax.rem(jnp.maximum(rel, 0), per) == edge)
        hit = h if hit is None else (hit | h)
    return hit


S5_BLOCK = 32
S5_ROW = S5_BLOCK * S5_GROUP
S5_ZW = 4 * LANES


def _s5_weight_body(lr_ref, li_ref, st_ref, btr_ref, bti_ref, cr_ref, ci_ref,
                    kt_ref, er_ref, ei_ref, wor_ref, woi_ref, aqr_ref, aqi_ref):
    backward = pl.program_id(0) >= S5_GROUPS
    lam_re = jnp.minimum(lr_ref[...], -1e-4)
    lam_im = li_ref[...]
    step = jnp.exp(st_ref[...])
    zr, zi = lam_re * step, lam_im * step
    mag = jnp.exp(zr)
    a_re, a_im = mag * jnp.cos(zi), mag * jnp.sin(zi)
    den = lam_re * lam_re + lam_im * lam_im
    num_re = a_re - 1.0
    coef_re = (num_re * lam_re + a_im * lam_im) / den
    coef_im = (a_im * lam_re - num_re * lam_im) / den
    bt_re, bt_im = btr_ref[...], bti_ref[...]
    bb_re = coef_re * bt_re - coef_im * bt_im
    bb_im = coef_re * bt_im + coef_im * bt_re

    def power(ell):
        m = jnp.exp(ell * zr)
        return m * jnp.cos(ell * zi), m * jnp.sin(ell * zi)

    tile = lambda a: jnp.concatenate([a] * S5_BLOCK, axis=0)
    lag = (lax.broadcasted_iota(jnp.int32, (S5_ROW, S5_STATE), 0) // S5_GROUP).astype(f32)
    p_re, p_im = power(lag)
    tb_re, tb_im = tile(bb_re), tile(bb_im)
    e_re = p_re * tb_re - p_im * tb_im
    e_im = p_re * tb_im + p_im * tb_re
    er_ref[...] = e_re
    ei_ref[...] = e_im
    c_re, c_im = cr_ref[...], ci_ref[...]
    kt_ref[...] = _mm(e_re, c_re, "hi", _NT) - _mm(e_im, c_im, "hi", _NT)
    lag_out = jnp.where(backward, S5_BLOCK - lag, lag + 1.0)
    q_re, q_im = power(lag_out)
    tc_re, tc_im = tile(c_re), tile(c_im)
    wor_ref[...] = tc_re * q_re - tc_im * q_im
    woi_ref[...] = -(tc_re * q_im + tc_im * q_re)
    jj = (lax.broadcasted_iota(jnp.int32, (SUBLANES, S5_STATE), 0) + 1).astype(f32) * S5_BLOCK
    aqr_ref[...], aqi_ref[...] = power(jj)


def _s5_weights(lam_re, lam_im, log_step, b_re, b_im, c_re, c_im):
    dg = 2 * S5_GROUPS
    vec = lambda a: a.reshape(dg, 1, S5_STATE)
    st = jnp.broadcast_to(log_step.reshape(dg, 1, 1), (dg, 1, S5_STATE))
    bt = lambda b: jnp.swapaxes(b, -1, -2).reshape(dg, S5_GROUP, S5_STATE)
    cc = lambda c: c.reshape(dg, S5_GROUP, S5_STATE)
    spec = lambda r, c: pl.BlockSpec((None, r, c), lambda i: (i, 0, 0))
    out = lambda r, c: jax.ShapeDtypeStruct((dg, r, c), f32)
    return pl.pallas_call(
        _s5_weight_body,
        grid=(dg,),
        in_specs=[spec(1, S5_STATE)] * 3 + [spec(S5_GROUP, S5_STATE)] * 4,
        out_specs=[spec(S5_ROW, S5_GROUP)] + [spec(S5_ROW, S5_STATE)] * 4 + [spec(SUBLANES, S5_STATE)] * 2,
        out_shape=[out(S5_ROW, S5_GROUP)] + [out(S5_ROW, S5_STATE)] * 4 + [out(SUBLANES, S5_STATE)] * 2,
        compiler_params=_cparams("parallel"),
        name="s5_weights",
    )(vec(lam_re), vec(lam_im), st, bt(b_re), bt(b_im), cc(c_re), cc(c_im))


def _s5_assemble(kt, e_re, e_im, wo_re, wo_im, aq_re, aq_im):
    g, t, c, n = S5_GROUPS, S5_BLOCK, S5_GROUP, S5_STATE
    k = kt.reshape(2, g, t, c, c)
    lag = jnp.arange(t)[None, :] - jnp.arange(t)[:, None]
    fwd = jnp.where((lag >= 0)[None, :, :, None, None], k[0][:, jnp.clip(lag, 0, t - 1)], 0.0)
    bwd = jnp.where((lag <= 0)[None, :, :, None, None], k[1][:, jnp.clip(-lag, 0, t - 1)], 0.0)
    toep = (fwd + bwd).transpose(0, 1, 3, 2, 4).reshape(g, t * c, t * c)
    wide = lambda a: jnp.pad(a, [(0, 0)] * (a.ndim - 1) + [(0, LANES - n)])
    e = lambda a: wide(a.reshape(2, g, t, c, n))
    inj = jnp.concatenate([e(e_re)[0][:, ::-1], e(e_im)[0][:, ::-1], e(e_re)[1], e(e_im)[1]], axis=-1)
    inj = inj.reshape(g, t * c, S5_ZW)
    w = lambda a: wide(a.reshape(2, g, t * c, n))
    out_t = jnp.concatenate([w(wo_re)[0], w(wo_im)[0], w(wo_re)[1], w(wo_im)[1]], axis=-1)
    a = lambda x: wide(x.reshape(2, g, SUBLANES, n))
    tab1 = jnp.concatenate([a(aq_re)[0], a(aq_re)[1]], axis=-1)
    tab2 = jnp.concatenate([a(aq_im)[0], a(aq_im)[1]], axis=-1)
    return toep.astype(bf16), inj.astype(bf16), out_t.astype(bf16), tab1, tab2


def _s5_conv_body(x_ref, m_ref, inj_ref, out_ref, tr_ref, ti_ref, o_ref, z_buf, h_buf, *, groups):
    n_tiles = x_ref.shape[0] // SUBLANES
    tile_tokens = SUBLANES * S5_BLOCK
    xb = x_ref[...].astype(bf16)
    z_buf[...] = _dot(xb, inj_ref[...])
    row = lax.broadcasted_iota(jnp.int32, (SUBLANES, LANES), 0)
    shape = (SUBLANES, LANES)

    def tables(reverse):
        lanes = slice(LANES, 2 * LANES) if reverse else slice(0, LANES)
        full = lambda r: (jnp.broadcast_to(tr_ref[r:r + 1, lanes], shape),
                          jnp.broadcast_to(ti_ref[r:r + 1, lanes], shape))
        levels = []
        for d in (1, 2, 4):
            ar, ai = full(d - 1)
            keep = (row < SUBLANES - d) if reverse else (row >= d)
            levels.append((d, jnp.where(keep, ar, 0.0), jnp.where(keep, ai, 0.0)))
        cr, ci = jnp.zeros(shape, f32), jnp.zeros(shape, f32)
        for r in range(SUBLANES):
            ar, ai = full(r)
            sel = (row == (SUBLANES - 1 - r)) if reverse else (row == r)
            cr, ci = jnp.where(sel, ar, cr), jnp.where(sel, ai, ci)
        return levels, cr, ci

    tabs = (tables(False), tables(True))

    def scan_tile(t, carry, reverse):
        levels, cr, ci = tabs[int(reverse)]
        base = 2 * LANES * int(reverse)
        re_l, im_l = slice(base, base + LANES), slice(base + LANES, base + 2 * LANES)
        fresh = _seq_edge(t, groups, tile_tokens, last=reverse)
        c_re, c_im = (jnp.where(fresh, 0.0, c) for c in carry)
        rows = pl.ds(pl.multiple_of(t * SUBLANES, SUBLANES), SUBLANES)
        h_re, h_im = z_buf[rows, re_l], z_buf[rows, im_l]
        for d, ar, ai in levels:
            sh = (SUBLANES - d) if reverse else d
            s_re, s_im = pltpu.roll(h_re, sh, 0), pltpu.roll(h_im, sh, 0)
            h_re, h_im = h_re + (ar * s_re - ai * s_im), h_im + (ar * s_im + ai * s_re)
        h_re, h_im = h_re + (cr * c_re - ci * c_im), h_im + (cr * c_im + ci * c_re)
        edge_in = SUBLANES - 1 if reverse else 0
        sh = (SUBLANES - 1) if reverse else 1
        h_buf[rows, re_l] = jnp.where(row == edge_in, c_re, pltpu.roll(h_re, sh, 0))
        h_buf[rows, im_l] = jnp.where(row == edge_in, c_im, pltpu.roll(h_im, sh, 0))
        edge_out = 0 if reverse else SUBLANES - 1
        return (jnp.broadcast_to(h_re[edge_out:edge_out + 1, :], shape),
                jnp.broadcast_to(h_im[edge_out:edge_out + 1, :], shape))

    def tile_pair(i, carry):
        return (scan_tile(i, carry[0], False), scan_tile(n_tiles - 1 - i, carry[1], True))

    zero = (jnp.zeros(shape, f32), jnp.zeros(shape, f32))
    lax.fori_loop(0, n_tiles, tile_pair, (zero, zero))

    y = _dot(xb, m_ref[...])
    y = y + lax.dot_general(h_buf[...].astype(bf16), out_ref[...], _NT, preferred_element_type=f32)
    o_ref[...] = y


def _s5_conv(x3, toep, inj, out_t, tab1, tab2, groups):
    g, n_rows, _ = x3.shape
    per = lambda r, c: pl.BlockSpec((None, r, c), lambda i: (i, 0, 0))
    return pl.pallas_call(
        functools.partial(_s5_conv_body, groups=groups),
        grid=(g,),
        in_specs=[per(n_rows, S5_ROW), per(S5_ROW, S5_ROW), per(S5_ROW, S5_ZW), per(S5_ROW, S5_ZW),
                  per(SUBLANES, 2 * LANES), per(SUBLANES, 2 * LANES)],
        out_specs=per(n_rows, S5_ROW),
        out_shape=jax.ShapeDtypeStruct((g, n_rows, S5_ROW), f32),
        scratch_shapes=[pltpu.VMEM((n_rows, S5_ZW), f32), pltpu.VMEM((n_rows, S5_ZW), f32)],
        compiler_params=_cparams("parallel"),
        name="s5_conv",
    )(x3, toep, inj, out_t, tab1, tab2)


def _s5_post_body(u_ref, y_ref, d_ref, w_ref, b_ref, o_ref):
    y = y_ref[...] + d_ref[...] * u_ref[...]
    y = jax.nn.gelu(y)
    gate = _dot(y.astype(bf16), w_ref[...]) + b_ref[...]
    o_ref[...] = y * jax.nn.sigmoid(gate)


def _s5_post(proj, y, d, glu_w, glu_b, j):
    t = proj.shape[0]
    tok = lambda i: (i, 0)
    return pl.pallas_call(
        _s5_post_body,
        grid=(t // TOKEN_TILE,),
        in_specs=[
            pl.BlockSpec((TOKEN_TILE, D_A), tok),
            pl.BlockSpec((TOKEN_TILE, D_A), tok),
            _layer_spec((1, D_A))(j),
            _layer_spec((D_A, D_A))(j),
            _layer_spec((1, D_A))(j),
        ],
        out_specs=pl.BlockSpec((TOKEN_TILE, D_A), tok),
        out_shape=jax.ShapeDtypeStruct((t, D_A), f32),
        compiler_params=_cparams("parallel"),
        name="s5_post",
    )(proj, y, d, glu_w, glu_b)


def _s5_mixer(proj, prm, j, groups):
    t = proj.shape[0]
    n_rows = t // S5_BLOCK
    w = _s5_weights(prm["lam_re"][j], prm["lam_im"][j], prm["log_step"][j], prm["b_re"][j], prm["b_im"][j],
                    prm["c_re"][j], prm["c_im"][j])
    toep, inj, out_t, tab1, tab2 = _s5_assemble(*w)
    x3 = proj[:, :D_A].reshape(n_rows, S5_BLOCK, S5_GROUPS, S5_GROUP).transpose(2, 0, 1, 3)
    y3 = _s5_conv(x3.reshape(S5_GROUPS, n_rows, S5_ROW), toep, inj, out_t, tab1, tab2, groups)
    y = y3.reshape(S5_GROUPS, n_rows, S5_BLOCK, S5_GROUP).transpose(1, 2, 0, 3).reshape(t, D_A)
    return _s5_post(proj, y, prm["d"], prm["glu_w"], prm["glu_b"], j)


RW_TILE = 256
RW_CHUNK = 64
RW_PAIRS = RW_HEADS // 2
RW_CHUNKS_PER_STEP = 2
RW_PREC_A = "b1"
RW_PREC_INV = "b1"
RW_PREC_APPLY = "b1"
RW_PREC_SEQ = "b1"
RW_PREC_SUM = "x2"
RW_P_WIDTH = 2304
RW_LORA_OFF = 2048
RW_LORA_W = 256


def _head_ones():
    a = lax.broadcasted_iota(jnp.int32, (D_B, D_B), 0) // RW_HEAD
    b = lax.broadcasted_iota(jnp.int32, (D_B, D_B), 1) // RW_HEAD
    return jnp.where(a == b, 1.0, 0.0).astype(f32)


def _rw_prep_body(x_ref, xp_ref, xn_ref, mu_ref, lw_ref, w0_ref, a0_ref, kk_ref, ka_ref,
                  r_out, k_out, v_out, kk_out, kka_out, lwf_out, lwb_out, g_out, *, groups):
    i = pl.program_id(0)
    x = x_ref[:, D_A:]
    prev = jnp.where(_seq_edge(i, groups, RW_TILE, last=False), 0.0, xp_ref[:, D_A:])
    nxt = jnp.where(_seq_edge(i, groups, RW_TILE, last=True), 0.0, xn_ref[:, D_A:])
    row = lax.broadcasted_iota(jnp.int32, x.shape, 0)
    shifted = 0.5 * (_shift_rows(x, prev, 1, row) + _shift_rows(x, nxt, -1, row))
    p = x + (shifted - x) * mu_ref[:, D_A:]
    r = p[:, :D_B]
    k = p[:, D_B:2 * D_B]
    v = p[:, 2 * D_B:3 * D_B]
    lora = p[:, RW_LORA_OFF - D_A:RW_LORA_OFF - D_A + RW_LORA_W]
    lora_t = jnp.tanh(lora).astype(bf16)
    a = jax.nn.sigmoid(a0_ref[...] + _dot(lora.astype(bf16), lw_ref[2]))
    g = _dot(jax.nn.sigmoid(lora).astype(bf16), lw_ref[3])
    kx = k * kk_ref[...]
    ss = _mm(kx * kx, _head_ones(), RW_PREC_SUM)
    kk = kx / jnp.maximum(jnp.sqrt(ss), 1e-12)
    r_out[...] = r
    k_out[...] = k * (1.0 + (a - 1.0) * ka_ref[...])
    v_out[...] = v
    kk_out[...] = kk
    kka_out[...] = kk * a
    g_out[...] = g
    for direction, out in enumerate((lwf_out, lwb_out)):
        wl = -jax.nn.softplus(-(w0_ref[direction:direction + 1, :] + _dot(lora_t, lw_ref[direction]))) - 0.5
        out[...] = -jnp.exp(wl)


def _rw_prep(proj, prm, j, groups):
    t = proj.shape[0]
    tok = lambda i: (i, 0)
    prev_spec, next_spec = _halo_specs(RW_TILE, RW_P_WIDTH, 0, t)
    out = jax.ShapeDtypeStruct((t, D_B), f32)
    return pl.pallas_call(
        functools.partial(_rw_prep_body, groups=groups),
        grid=(t // RW_TILE,),
        in_specs=[pl.BlockSpec((RW_TILE, RW_P_WIDTH), tok), prev_spec, next_spec,
                  _layer_spec((1, RW_P_WIDTH))(j), _layer_spec((4, RW_LORA_W, D_B))(j),
                  _layer_spec((2, D_B))(j), _layer_spec((1, D_B))(j), _layer_spec((1, D_B))(j),
                  _layer_spec((1, D_B))(j)],
        out_specs=[pl.BlockSpec((RW_TILE, D_B), tok)] * 8,
        out_shape=[out] * 8,
        compiler_params=_cparams("parallel"),
        name="rw_prep",
    )(proj, proj, proj, prm["mu"], prm["lora_w"], prm["w0"], prm["a0"], prm["k_k"], prm["k_a"])


def _rw_chunk_body(r_ref, k_ref, v_ref, kk_ref, kka_ref, lw_ref, y_out, st, *, reverse, groups, n_steps):
    c = RW_CHUNK
    pid = pl.program_id(0)

    @pl.when(pid == 0)
    def _():
        st[...] = jnp.zeros_like(st)

    ri = lax.broadcasted_iota(jnp.int32, (c, c), 0)
    ci = lax.broadcasted_iota(jnp.int32, (c, c), 1)
    tri = jnp.where((ci >= ri) if reverse else (ci <= ri), 1.0, 0.0).astype(f32)
    edge = 0 if reverse else c - 1
    big_r = lax.broadcasted_iota(jnp.int32, (LANES, LANES), 0)
    big_c = lax.broadcasted_iota(jnp.int32, (LANES, LANES), 1)
    same_head = (big_r // RW_HEAD) == (big_c // RW_HEAD)
    s_idx, j_idx = big_r % RW_HEAD, big_c % RW_HEAD
    strict = same_head & ((j_idx > s_idx) if reverse else (j_idx < s_idx))
    incl = same_head & ((j_idx >= s_idx) if reverse else (j_idx <= s_idx))
    eye = big_r == big_c
    eye_f = jnp.where(eye, 1.0, 0.0).astype(f32)
    head0 = lax.broadcasted_iota(jnp.int32, (c, LANES), 1) < RW_HEAD

    def expand(x):
        return jnp.concatenate([jnp.where(head0, x, 0.0), jnp.where(head0, 0.0, x)], axis=0)

    def collapse(x):
        return x[:c] + x[c:]

    n_sub = r_ref.shape[0] // c
    lanes = [slice(p * LANES, (p + 1) * LANES) for p in range(RW_PAIRS)]
    rows = [slice(ch * c, (ch + 1) * c) for ch in range(n_sub)]
    units = [(ch, p) for ch in range(n_sub) for p in range(RW_PAIRS)]
    pre = []
    for rs in rows:
        lw = lw_ref[rs, :]
        cum = _dot_hi(tri, lw)
        total = cum[edge:edge + 1, :]
        e_neg, tail = jnp.exp(-cum), jnp.exp(total - cum)
        k, kka = k_ref[rs, :], kka_ref[rs, :]
        pre.append(dict(at=-kk_ref[rs, :] * jnp.exp(cum - lw), rt=r_ref[rs, :] * jnp.exp(cum),
                        kh=k * e_neg, bh=kka * e_neg, kp=k * tail, bp=kka * tail, v=v_ref[rs, :],
                        pc=jnp.exp(total)))

    def get(name):
        return [pre[ch][name][:, lanes[p]] for ch, p in units]

    at, rt, kh, bh, kp, bp, v = (get(n) for n in ("at", "rt", "kh", "bh", "kp", "bp", "v"))
    at_exp = [expand(x) for x in at]
    v_exp = [expand(x) for x in v]
    z = [_mm(jnp.concatenate([ae, expand(rr)], axis=0), jnp.concatenate([a, a, b, b], axis=0), RW_PREC_A, _NT)
         for ae, rr, a, b in zip(at_exp, rt, kh, bh)]
    a_ak = [jnp.where(strict, zz[:LANES, :LANES], 0.0) for zz in z]
    a_ab = [jnp.where(strict, zz[:LANES, LANES:], 0.0) for zz in z]
    a_rk = [jnp.where(incl, zz[LANES:, :LANES], 0.0) for zz in z]
    a_rb = [jnp.where(incl, zz[LANES:, LANES:], 0.0) for zz in z]
    tinv = [eye_f + a for a in a_ab]
    pw = a_ab
    for _ in range(int(math.log2(c)) - 1):
        pw = [_mm(q, q, RW_PREC_INV) for q in pw]
        tinv = [_mm(t, eye_f + q, RW_PREC_INV) for t, q in zip(tinv, pw)]
    w = [_mm(a, ve, RW_PREC_APPLY) for a, ve in zip(a_ak, v_exp)]
    tu = [_mm(t, jnp.concatenate([ae, ww], axis=1), RW_PREC_APPLY) for t, ae, ww in zip(tinv, at_exp, w)]
    atp_exp = [x[:, :LANES] for x in tu]
    u0_exp = [x[:, LANES:] for x in tu]
    y0 = [collapse(_mm(ark, ve, RW_PREC_APPLY) + _mm(arb, ue, RW_PREC_APPLY))
          for ark, arb, ve, ue in zip(a_rk, a_rb, v_exp, u0_exp)]
    rp = [rr + collapse(_mm(arb, ae, RW_PREC_APPLY)) for rr, arb, ae in zip(rt, a_rb, atp_exp)]
    atp = [collapse(x) for x in atp_exp]
    u0 = [collapse(x) for x in u0_exp]
    g_full = [_mm(b, a, RW_PREC_APPLY, _TN) for b, a in zip(bp, atp)]
    h_full = [_mm(kd, vv, RW_PREC_APPLY, _TN) + _mm(b, u, RW_PREC_APPLY, _TN)
              for kd, vv, b, u in zip(kp, v, bp, u0)]
    blk = (n_steps - 1 - pid) if reverse else pid
    state = [st[:, sl] for sl in lanes]
    for ch in (range(n_sub - 1, -1, -1) if reverse else range(n_sub)):
        fresh = _seq_edge(blk * n_sub + ch, groups, c, last=reverse)
        for p in range(RW_PAIRS):
            i, sl = ch * RW_PAIRS + p, lanes[p]
            pc = jnp.broadcast_to(pre[ch]["pc"][:, sl], (LANES, LANES))
            g_bd = jnp.where(same_head, g_full[i], 0.0) + jnp.where(eye, pc, 0.0)
            h_bd = jnp.where(same_head, h_full[i], 0.0)
            s0 = jnp.where(fresh, 0.0, state[p])
            y_out[rows[ch], sl] = _mm(rp[i], s0, RW_PREC_SEQ) + y0[i]
            state[p] = _mm(g_bd, s0, RW_PREC_SEQ) + h_bd
    for p in range(RW_PAIRS):
        st[:, lanes[p]] = state[p]


def _rw_chunk(r, k, v, kk, kka, lw, groups, reverse):
    t = r.shape[0]
    step_rows = RW_CHUNK * RW_CHUNKS_PER_STEP
    n_steps = t // step_rows
    order = (lambda i: (n_steps - 1 - i, 0)) if reverse else (lambda i: (i, 0))
    return pl.pallas_call(
        functools.partial(_rw_chunk_body, reverse=reverse, groups=groups, n_steps=n_steps),
        grid=(n_steps,),
        in_specs=[pl.BlockSpec((step_rows, D_B), order)] * 6,
        out_specs=pl.BlockSpec((step_rows, D_B), order),
        out_shape=jax.ShapeDtypeStruct((t, D_B), f32),
        scratch_shapes=[pltpu.VMEM((LANES, D_B), f32)],
        compiler_params=_cparams("arbitrary"),
        name="rw_chunk_rev" if reverse else "rw_chunk_fwd",
    )(r, k, v, kk, kka, lw)


def _rw_post_body(yf_ref, yb_ref, r_ref, k_ref, v_ref, g_ref, rk_ref, lnw_ref, lnb_ref, o_ref):
    ones = _head_ones()
    inv = 1.0 / RW_HEAD
    y = yf_ref[...] + yb_ref[...]
    mean = _mm(y, ones, RW_PREC_SUM) * inv
    yc = y - mean
    var = _mm(yc * yc, ones, RW_PREC_SUM) * inv
    yn = yc * lax.rsqrt(var + RW_GN_EPS) * lnw_ref[...] + lnb_ref[...]
    bonus = _mm(r_ref[...] * k_ref[...] * rk_ref[...], ones, RW_PREC_SUM) * v_ref[...]
    o_ref[...] = (yn + bonus) * g_ref[...]


def _rw_post(y_f, y_b, r, k, v, g, prm, j):
    t = y_f.shape[0]
    tok = pl.BlockSpec((TOKEN_TILE, D_B), lambda i: (i, 0))
    vec = _layer_spec((1, D_B))(j)
    return pl.pallas_call(
        _rw_post_body,
        grid=(t // TOKEN_TILE,),
        in_specs=[tok] * 6 + [vec] * 3,
        out_specs=tok,
        out_shape=jax.ShapeDtypeStruct((t, D_B), f32),
        compiler_params=_cparams("parallel"),
        name="rw_post",
    )(y_f, y_b, r, k, v, g, prm["r_k"], prm["ln_w"], prm["ln_b"])


def _rwkv7_mixer(proj, prm, j, groups):
    r, k, v, kk, kka, lw_f, lw_b, g = _rw_prep(proj, prm, j, groups)
    y_f = _rw_chunk(r, k, v, kk, kka, lw_f, groups, reverse=False)
    y_b = _rw_chunk(r, k, v, kk, kka, lw_b, groups, reverse=True)
    return _rw_post(y_f, y_b, r, k, v, g, prm, j)


def _halo_specs(tile, width, col_blk, n_rows):
    per = tile // SUBLANES
    last8 = n_rows // SUBLANES - 1
    prev = pl.BlockSpec((SUBLANES, width), lambda i: (jnp.maximum(i * per - 1, 0), col_blk))
    nxt = pl.BlockSpec((SUBLANES, width), lambda i: (jnp.minimum((i + 1) * per, last8), col_blk))
    return prev, nxt


def _shift_rows(x, halo, k, row):
    n = x.shape[0]
    if k > 0:
        y = pltpu.roll(x, k, 0)
        for r in range(k):
            y = jnp.where(row == r, halo[SUBLANES - k + r:SUBLANES - k + r + 1, :], y)
        return y
    y = pltpu.roll(x, n - 1, 0)
    return jnp.where(row == n - 1, halo[0:1, :], y)


LRU_TILE = 512
CD_XB_BLK = 3
CD_GB_BLK = 4


def _lru_gate_body(x_ref, xp_ref, xn_ref, cw_ref, cb_ref, lam_ref, w_ref, b_ref,
                   af_ref, bf_ref, ab_ref, bb_ref, *, groups):
    i = pl.program_id(0)
    first = _seq_edge(i, groups, LRU_TILE, last=False)
    last = _seq_edge(i, groups, LRU_TILE, last=True)
    x = x_ref[...]
    prev = jnp.where(first, 0.0, xp_ref[...])
    nxt = jnp.where(last, 0.0, xn_ref[...])
    row = lax.broadcasted_iota(jnp.int32, x.shape, 0)
    xc = (cw_ref[0:1, :] * _shift_rows(x, prev, 2, row) + cw_ref[1:2, :] * _shift_rows(x, prev, 1, row)
          + cw_ref[2:3, :] * x + cw_ref[3:4, :] * _shift_rows(x, nxt, -1, row) + cb_ref[...])
    pre = _dot(xc.astype(bf16), w_ref[...]) + b_ref[...]
    outs = ((af_ref, bf_ref), (ab_ref, bb_ref))
    for direction in range(2):
        base = 2 * direction * D_C
        gate_r = jax.nn.sigmoid(pre[:, base:base + D_C])
        gate_i = jax.nn.sigmoid(pre[:, base + D_C:base + 2 * D_C])
        log_a = -LRU_C * gate_r * jax.nn.softplus(-lam_ref[direction:direction + 1, :])
        t = jnp.tanh(log_a)
        mult = jnp.sqrt(-2.0 * t / (1.0 - t))
        a_ref, b_ref_out = outs[direction]
        a_ref[...] = jnp.exp(log_a)
        b_ref_out[...] = mult * gate_i * xc


def _lru_gates(proj, prm, j, groups):
    t = proj.shape[0]
    tok = lambda i: (i, 0)
    prev_spec, next_spec = _halo_specs(LRU_TILE, D_C, CD_XB_BLK, t)
    out = jax.ShapeDtypeStruct((t, D_C), f32)
    return pl.pallas_call(
        functools.partial(_lru_gate_body, groups=groups),
        grid=(t // LRU_TILE,),
        in_specs=[
            pl.BlockSpec((LRU_TILE, D_C), lambda i: (i, CD_XB_BLK)),
            prev_spec,
            next_spec,
            _layer_spec((LRU_CONV, D_C))(j),
            _layer_spec((1, D_C))(j),
            _layer_spec((2, D_C))(j),
            _layer_spec((D_C, 4 * D_C))(j),
            _layer_spec((1, 4 * D_C))(j),
        ],
        out_specs=[pl.BlockSpec((LRU_TILE, D_C), tok)] * 4,
        out_shape=[out] * 4,
        compiler_params=_cparams("parallel"),
        name="lru_gates",
    )(proj, proj, proj, prm["conv_w"], prm["conv_b"], prm["lam"], prm["w_gates"], prm["b_gates"])


def _lru_scan_body(af_ref, bf_ref, ab_ref, bb_ref, hf_ref, hb_ref, carry_f, carry_b, *, groups, n_blk):
    i = pl.program_id(0)

    @pl.when(_seq_edge(i, groups, LRU_TILE, last=False))
    def _():
        carry_f[...] = jnp.zeros_like(carry_f)

    @pl.when(_seq_edge(n_blk - 1 - i, groups, LRU_TILE, last=True))
    def _():
        carry_b[...] = jnp.zeros_like(carry_b)

    row = lax.broadcasted_iota(jnp.int32, (SUBLANES, D_C), 0)
    n_tiles = LRU_TILE // SUBLANES

    def run(a_ref, b_ref, h_ref, carry, reverse):
        def tile_step(k, c):
            t = (n_tiles - 1 - k) if reverse else k
            r0 = pl.multiple_of(t * SUBLANES, SUBLANES)
            a = a_ref[pl.ds(r0, SUBLANES), :]
            b = b_ref[pl.ds(r0, SUBLANES), :]
            for d in (1, 2, 4):
                sh = (SUBLANES - d) if reverse else d
                keep = (row < SUBLANES - d) if reverse else (row >= d)
                a_s = jnp.where(keep, pltpu.roll(a, sh, 0), 1.0)
                b_s = jnp.where(keep, pltpu.roll(b, sh, 0), 0.0)
                b = b + a * b_s
                a = a * a_s
            h = b + a * c
            h_ref[pl.ds(r0, SUBLANES), :] = h
            edge = 0 if reverse else SUBLANES - 1
            return jnp.broadcast_to(h[edge:edge + 1, :], (SUBLANES, D_C))

        carry[...] = lax.fori_loop(0, n_tiles, tile_step, carry[...])

    run(af_ref, bf_ref, hf_ref, carry_f, False)
    run(ab_ref, bb_ref, hb_ref, carry_b, True)


def _lru_scan(a_f, b_f, a_b, b_b, groups):
    t = a_f.shape[0]
    n_blk = t // LRU_TILE
    fwd = lambda i: (i, 0)
    bwd = lambda i: (n_blk - 1 - i, 0)
    out = jax.ShapeDtypeStruct((t, D_C), f32)
    return pl.pallas_call(
        functools.partial(_lru_scan_body, groups=groups, n_blk=n_blk),
        grid=(n_blk,),
        in_specs=[pl.BlockSpec((LRU_TILE, D_C), fwd), pl.BlockSpec((LRU_TILE, D_C), fwd),
                  pl.BlockSpec((LRU_TILE, D_C), bwd), pl.BlockSpec((LRU_TILE, D_C), bwd)],
        out_specs=[pl.BlockSpec((LRU_TILE, D_C), fwd), pl.BlockSpec((LRU_TILE, D_C), bwd)],
        out_shape=[out, out],
        scratch_shapes=[pltpu.VMEM((SUBLANES, D_C), f32), pltpu.VMEM((SUBLANES, D_C), f32)],
        compiler_params=_cparams("arbitrary"),
        name="lru_scan",
    )(a_f, b_f, a_b, b_b)


HY_N2 = 256
HY_CW = 128
HY_SC_TILE = 512
HY_MLP_TILE = 512
HY_PREC_S1 = "b1"
HY_PREC_S2 = "b1"
HY_S1_UNROLL = 2


def _hy_dims(seq_len):
    n = 2 * seq_len
    n1 = n // HY_N2
    lb = n1 // 2
    k1 = n1 // 2 + 1
    k1p = -(-k1 // SUBLANES) * SUBLANES
    return n, n1, lb, k1, k1p


def _hy_tables(seq_len):
    n, n1, lb, k1, k1p = _hy_dims(seq_len)
    two_pi = 2.0 * math.pi
    kk = np.arange(k1, dtype=np.float64)[:, None]
    nn = np.arange(lb, dtype=np.float64)[None, :]
    ang1 = two_pi * kk * nn / n1
    f1 = np.zeros((2 * k1p, lb), np.float64)
    f1[:k1] = np.cos(ang1)
    f1[k1p:k1p + k1] = -np.sin(ang1)
    weight = np.where((np.arange(k1) == 0) | (np.arange(k1) == n1 // 2), 1.0, 2.0)[None, :] / n
    f1inv = np.zeros((lb, 2 * k1p), np.float64)
    f1inv[:, :k1] = weight * np.cos(ang1.T)
    f1inv[:, k1p:k1p + k1] = -weight * np.sin(ang1.T)
    idx = np.arange(HY_N2, dtype=np.float64)
    ang2 = two_pi * np.outer(idx, idx) / HY_N2
    f2 = np.concatenate([np.cos(ang2), -np.sin(ang2)], axis=0)
    prod = (jnp.arange(k1, dtype=jnp.int32)[:, None] * jnp.arange(HY_N2, dtype=jnp.int32)[None, :]) % n
    ang = prod.astype(f32) * f32(two_pi / n)
    bc = lambda a: jnp.broadcast_to(a.reshape(k1 * HY_N2, 1), (k1 * HY_N2, HY_CW))
    eye8 = np.eye(SUBLANES)
    return dict(f1=jnp.asarray(np.kron(f1, eye8), f32), f1inv=jnp.asarray(np.kron(f1inv, eye8), f32),
                f2=jnp.asarray(f2, f32), tw_re=bc(jnp.cos(ang)), tw_im=bc(-jnp.sin(ang)))


def _hy_slab(k):
    return pl.ds(k * HY_N2 if isinstance(k, int) else pl.multiple_of(k * HY_N2, HY_N2), HY_N2)


def _hy_slab_loop(slab, k1):
    def pair(i, carry):
        slab(2 * i, carry)
        slab(2 * i + 1, carry)
        return carry

    lax.fori_loop(0, k1 // 2, pair, 0)
    if k1 % 2:
        slab(k1 - 1, 0)


def _hy_forward(z_ref, f1_ref, f2_ref, twr_ref, twi_ref, wr, wi, dims):
    _, _, lb, k1, k1p = dims

    def stage1_group(n2):
        tiles = [z_ref[pl.ds(pl.multiple_of(b * HY_N2 + n2, SUBLANES), SUBLANES), :] for b in range(lb)]
        y = _mm(f1_ref[...], jnp.concatenate(tiles, axis=0), HY_PREC_S1)
        for s in range(k1p):
            dst = pl.ds(pl.multiple_of(s * HY_N2 + n2, SUBLANES), SUBLANES)
            wr[dst, :] = y[s * SUBLANES:(s + 1) * SUBLANES]
            wi[dst, :] = y[(k1p + s) * SUBLANES:(k1p + s + 1) * SUBLANES]

    def stage1(g, carry):
        for u in range(HY_S1_UNROLL):
            stage1_group((g * HY_S1_UNROLL + u) * SUBLANES)
        return carry

    lax.fori_loop(0, HY_N2 // (SUBLANES * HY_S1_UNROLL), stage1, 0)

    def stage2(k, carry):
        sl = _hy_slab(k)
        yr, yi, tr, ti = wr[sl, :], wi[sl, :], twr_ref[sl, :], twi_ref[sl, :]
        ar = yr * tr - yi * ti
        ai = yr * ti + yi * tr
        pq = _mm(f2_ref[...], jnp.concatenate([ar, ai], axis=1), HY_PREC_S2)
        p, q = pq[:, :HY_CW], pq[:, HY_CW:]
        wr[sl, :] = p[:HY_N2] - q[HY_N2:]
        wi[sl, :] = p[HY_N2:] + q[:HY_N2]
        return carry

    _hy_slab_loop(stage2, k1)


def _hy_inverse(hr_ref, hi_ref, f1inv_ref, f2_ref, twr_ref, twi_ref, wr, wi, o_ref, dims):
    _, _, lb, k1, k1p = dims

    def stage2(k, carry):
        sl = _hy_slab(k)
        xr, xi, hr, hi = wr[sl, :], wi[sl, :], hr_ref[sl, :], hi_ref[sl, :]
        zr = xr * hr - xi * hi
        zi = xr * hi + xi * hr
        pq = _mm(f2_ref[...], jnp.concatenate([zr, zi], axis=1), HY_PREC_S2)
        p, q = pq[:, :HY_CW], pq[:, HY_CW:]
        vr = p[:HY_N2] + q[HY_N2:]
        vi = q[:HY_N2] - p[HY_N2:]
        tr, ti = twr_ref[sl, :], twi_ref[sl, :]
        wr[sl, :] = vr * tr + vi * ti
        wi[sl, :] = vi * tr - vr * ti
        return carry

    _hy_slab_loop(stage2, k1)

    def stage1_group(n2):
        src = [pl.ds(pl.multiple_of(s * HY_N2 + n2, SUBLANES), SUBLANES) for s in range(k1p)]
        tiles = [wr[d, :] for d in src] + [wi[d, :] for d in src]
        x = _mm(f1inv_ref[...], jnp.concatenate(tiles, axis=0), HY_PREC_S1)
        for b in range(lb):
            dst = pl.ds(pl.multiple_of(b * HY_N2 + n2, SUBLANES), SUBLANES)
            o_ref[dst, :] = x[b * SUBLANES:(b + 1) * SUBLANES]

    def stage1(g, carry):
        for u in range(HY_S1_UNROLL):
            stage1_group((g * HY_S1_UNROLL + u) * SUBLANES)
        return carry

    lax.fori_loop(0, HY_N2 // (SUBLANES * HY_S1_UNROLL), stage1, 0)


def _hy_shortconv_body(x_ref, xp_ref, xn_ref, w_ref, b_ref, o_ref, *, groups):
    i = pl.program_id(0)
    x = x_ref[...]
    prev = jnp.where(_seq_edge(i, groups, HY_SC_TILE, last=False), 0.0, xp_ref[...])
    nxt = jnp.where(_seq_edge(i, groups, HY_SC_TILE, last=True), 0.0, xn_ref[...])
    row = lax.broadcasted_iota(jnp.int32, x.shape, 0)
    o_ref[...] = (w_ref[0:1, :] * _shift_rows(x, prev, 1, row) + w_ref[1:2, :] * x
                  + w_ref[2:3, :] * _shift_rows(x, nxt, -1, row) + b_ref[...])


def _hy_shortconv(proj, conv_w, conv_b, j, groups):
    t = proj.shape[0]
    width = 3 * D_D
    prev_spec, next_spec = _halo_specs(HY_SC_TILE, width, 0, t)
    return pl.pallas_call(
        functools.partial(_hy_shortconv_body, groups=groups),
        grid=(t // HY_SC_TILE,),
        in_specs=[pl.BlockSpec((HY_SC_TILE, width), lambda i: (i, 0)), prev_spec, next_spec,
                  _layer_spec((3, width))(j), _layer_spec((1, width))(j)],
        out_specs=pl.BlockSpec((HY_SC_TILE, width), lambda i: (i, 0)),
        out_shape=jax.ShapeDtypeStruct((t, width), f32),
        compiler_params=_cparams("parallel"),
        name="hy_shortconv",
    )(proj, proj, proj, conv_w, conv_b)


def _hy_mlp_body(z_ref, w1_ref, b1_ref, w2_ref, b2_ref, fr_ref, w3_ref, dl_ref, o_ref):
    z = z_ref[...]
    h = jnp.sin(fr_ref[0:1, :] * (_dot_hi(z, w1_ref[...]) + b1_ref[...]))
    h = jnp.sin(fr_ref[1:2, :] * (_dot_hi(h, w2_ref[...]) + b2_ref[...]))
    h = _dot_hi(h, w3_ref[...])
    o_ref[...] = h * jnp.exp(-z[:, 0:1] * dl_ref[...])


def _hy_mlp(seq_len, w1, b1, w2, b2, freq, w3):
    t = jnp.linspace(0.0, 1.0, seq_len, dtype=f32)[:, None]
    bands = (HY_POS_EMB - 1) // 2
    freqs = jnp.linspace(1e-4, bands - 1, bands, dtype=f32)[None, :]
    wpos = (2.0 * math.pi / seq_len) * jnp.arange(seq_len, dtype=f32)[:, None]
    z = jnp.concatenate([t, jnp.cos(freqs * wpos), -jnp.sin(freqs * wpos)], axis=-1)
    z = jnp.pad(z, ((0, 0), (0, LANES - HY_POS_EMB)))
    w1 = jnp.pad(w1, ((0, LANES - HY_POS_EMB), (0, 0)))
    max_decay = math.log(1e-2) / 0.3
    min_decay = math.log(1e-2) / 1.5
    deltas = jnp.abs(jnp.linspace(min_decay, max_decay, D_D, dtype=f32))
    deltas = jnp.tile(deltas, 4)[None, :]
    width = 4 * D_D
    hid = HY_FILTER_HIDDEN
    return pl.pallas_call(
        _hy_mlp_body,
        grid=(seq_len // HY_MLP_TILE,),
        in_specs=[pl.BlockSpec((HY_MLP_TILE, LANES), lambda i: (i, 0)),
                  _const_spec((LANES, hid)), _const_spec((1, hid)), _const_spec((hid, hid)),
                  _const_spec((1, hid)), _const_spec((2, hid)), _const_spec((hid, width)),
                  _const_spec((1, width))],
        out_specs=pl.BlockSpec((HY_MLP_TILE, width), lambda i: (i, 0)),
        out_shape=jax.ShapeDtypeStruct((seq_len, width), f32),
        compiler_params=_cparams("parallel"),
        name="hy_mlp",
    )(z, w1, b1[None, :], w2, b2[None, :], freq, w3, deltas)


def _hy_spectrum_body(hf_ref, hb_ref, f1_ref, f2_ref, twr_ref, twi_ref, or_ref, oi_ref,
                      zb, wr, wi, *, dims):
    k1 = dims[3]
    rows = k1 * HY_N2
    hf = hf_ref[...]
    hb = hb_ref[...]
    row = lax.broadcasted_iota(jnp.int32, hb.shape, 0)
    hb0 = jnp.where(row == 0, 0.0, hb)
    scale = 1.0 / (jnp.sum(jnp.abs(hf), axis=0, keepdims=True) + jnp.sum(jnp.abs(hb0), axis=0, keepdims=True))
    _hy_forward(hf_ref, f1_ref, f2_ref, twr_ref, twi_ref, wr, wi, dims)
    or_ref[...] = wr[:rows, :] * scale
    oi_ref[...] = wi[:rows, :] * scale
    zb[...] = hb0
    _hy_forward(zb, f1_ref, f2_ref, twr_ref, twi_ref, wr, wi, dims)
    or_ref[...] = or_ref[...] + wr[:rows, :] * scale
    oi_ref[...] = oi_ref[...] - wi[:rows, :] * scale


def _hy_spectrum(taps, tables, seq_len):
    dims = _hy_dims(seq_len)
    _, _, lb, k1, k1p = dims
    n_ct = D_D // HY_CW
    rows = k1 * HY_N2
    out = jax.ShapeDtypeStruct((2, rows, D_D), f32)
    return pl.pallas_call(
        functools.partial(_hy_spectrum_body, dims=dims),
        grid=(2, n_ct),
        in_specs=[pl.BlockSpec((seq_len, HY_CW), lambda o, c: (0, o * n_ct + c), pipeline_mode=pl.Buffered(1)),
                  pl.BlockSpec((seq_len, HY_CW), lambda o, c: (0, (2 + o) * n_ct + c),
                               pipeline_mode=pl.Buffered(1)),
                  _const_spec((2 * k1p * SUBLANES, lb * SUBLANES)), _const_spec((2 * HY_N2, HY_N2)),
                  _const_spec((rows, HY_CW)), _const_spec((rows, HY_CW))],
        out_specs=[pl.BlockSpec((None, rows, HY_CW), lambda o, c: (o, 0, c))] * 2,
        out_shape=[out, out],
        scratch_shapes=[pltpu.VMEM((seq_len, HY_CW), f32),
                        pltpu.VMEM((k1p * HY_N2, HY_CW), f32), pltpu.VMEM((k1p * HY_N2, HY_CW), f32)],
        compiler_params=_cparams("arbitrary", "arbitrary"),
        name="hy_spectrum",
    )(taps, taps, tables["f1"], tables["f2"], tables["tw_re"], tables["tw_im"])


def _hy_conv_body(z_ref, g_ref, hr_ref, hi_ref, bias_ref, f1_ref, f1inv_ref, f2_ref, twr_ref, twi_ref,
                  o_ref, wr, wi, *, dims):
    _hy_forward(z_ref, f1_ref, f2_ref, twr_ref, twi_ref, wr, wi, dims)
    _hy_inverse(hr_ref, hi_ref, f1inv_ref, f2_ref, twr_ref, twi_ref, wr, wi, o_ref, dims)
    bias = bias_ref[...]
    n_chunks = z_ref.shape[0] // HY_N2

    def gate(c, carry):
        sl = pl.ds(pl.multiple_of(c * HY_N2, HY_N2), HY_N2)
        o_ref[sl, :] = g_ref[sl, :] * (o_ref[sl, :] + bias * z_ref[sl, :])
        return carry

    lax.fori_loop(0, n_chunks, gate, 0)


def _hy_conv(z, z_off, z_col, gate, gate_off, gate_col, h_re, h_im, bias, tables, order, j, n_seq, seq_len):
    dims = _hy_dims(seq_len)
    _, _, lb, k1, k1p = dims
    n_ct = D_D // HY_CW
    rows = k1 * HY_N2
    z0, g0 = z_off // seq_len, gate_off // seq_len
    one = pl.Buffered(1)
    return pl.pallas_call(
        functools.partial(_hy_conv_body, dims=dims),
        grid=(n_ct, n_seq),
        in_specs=[
            pl.BlockSpec((seq_len, HY_CW), lambda c, b: (z0 + b, z_col * n_ct + c), pipeline_mode=one),
            pl.BlockSpec((seq_len, HY_CW), lambda c, b: (g0 + b, gate_col * n_ct + c), pipeline_mode=one),
            pl.BlockSpec((None, rows, HY_CW), lambda c, b: (order, 0, c), pipeline_mode=one),
            pl.BlockSpec((None, rows, HY_CW), lambda c, b: (order, 0, c), pipeline_mode=one),
            pl.BlockSpec((None, None, 1, HY_CW), lambda c, b: (j, order, 0, c)),
            _const_spec((2 * k1p * SUBLANES, lb * SUBLANES)), _const_spec((lb * SUBLANES, 2 * k1p * SUBLANES)), _const_spec((2 * HY_N2, HY_N2)),
            _const_spec((rows, HY_CW)), _const_spec((rows, HY_CW)),
        ],
        out_specs=pl.BlockSpec((seq_len, HY_CW), lambda c, b: (b, c)),
        out_shape=jax.ShapeDtypeStruct((n_seq * seq_len, D_D), f32),
        scratch_shapes=[pltpu.VMEM((k1p * HY_N2, HY_CW), f32), pltpu.VMEM((k1p * HY_N2, HY_CW), f32)],
        compiler_params=_cparams("arbitrary", "arbitrary"),
        name="hy_conv",
    )(z, gate, h_re, h_im, bias, tables["f1"], tables["f1inv"], tables["f2"], tables["tw_re"], tables["tw_im"])


def _hyena_mixer(proj, prm, j, groups):
    pc = _hy_shortconv(proj, prm["conv_w"], prm["conv_b"], j, groups)
    outs = []
    for tok_off, n_seq, seq_len in groups:
        tables = _hy_tables(seq_len)
        taps = _hy_mlp(seq_len, prm["f_w1"][j], prm["f_b1"][j], prm["f_w2"][j], prm["f_b2"][j],
                       prm["f_freq"][j], prm["f_w3"][j])
        h_re, h_im = _hy_spectrum(taps, tables, seq_len)
        z1 = _hy_conv(pc, tok_off, 0, pc, tok_off, 1, h_re, h_im, prm["bias"], tables, 0, j, n_seq, seq_len)
        y = _hy_conv(z1, 0, 0, pc, tok_off, 2, h_re, h_im, prm["bias"], tables, 1, j, n_seq, seq_len)
        outs.append(y)
    return jnp.concatenate(outs, axis=0)


def _outproj_odd_body(x_ref, hf_ref, hb_ref, gb_ref, yd_ref, w_ref, o_ref):
    y_c = (hf_ref[...] + hb_ref[...]) * jax.nn.gelu(gb_ref[...])
    acc = _dot(y_c.astype(bf16), w_ref[:D_C, :])
    acc = acc + _dot(yd_ref[...].astype(bf16), w_ref[D_C:, :])
    o_ref[...] = x_ref[...] + acc


def _outproj_odd(x, h_f, h_b, proj, y_d, w, j):
    t = x.shape[0]
    tok = lambda i: (i, 0)
    return pl.pallas_call(
        _outproj_odd_body,
        grid=(t // TOKEN_TILE,),
        in_specs=[
            pl.BlockSpec((TOKEN_TILE, D_MODEL), tok),
            pl.BlockSpec((TOKEN_TILE, D_C), tok),
            pl.BlockSpec((TOKEN_TILE, D_C), tok),
            pl.BlockSpec((TOKEN_TILE, D_C), lambda i: (i, CD_GB_BLK)),
            pl.BlockSpec((TOKEN_TILE, D_D), tok),
            _layer_spec((D_C + D_D, D_MODEL))(j),
        ],
        out_specs=pl.BlockSpec((TOKEN_TILE, D_MODEL), tok),
        out_shape=jax.ShapeDtypeStruct((t, D_MODEL), f32),
        compiler_params=_cparams("parallel"),
        name="outproj_odd",
    )(x, h_f, h_b, proj, y_d, w)


def _block_diag_heads(w):
    n, h, a, b = w.shape
    return jnp.einsum("nhij,hk->nhikj", w, jnp.eye(h, dtype=w.dtype)).reshape(n, h * a, h * b)


def _pad_rows(w, lo, total):
    return jnp.pad(w, ((0, 0), (lo, total - lo - w.shape[1]), (0, 0)))


def kernel(x_prompt, x_sample, ffn1_norm, ffn1_w_gate, ffn1_w_up, ffn1_w_down, mix_norm, ffn2_norm, ffn2_w_gate, ffn2_w_up, ffn2_w_down, ab_w_in, ab_w_out, s5_lambda_re, s5_lambda_im, s5_log_step, s5_b_re, s5_b_im, s5_c_re, s5_c_im, s5_d, s5_glu_w, s5_glu_b, rw_mu, rw_w0, rw_w_up, rw_a0, rw_a_up, rw_g_up, rw_k_k, rw_k_a, rw_r_k, rw_ln_w, rw_ln_b, cd_w_in, cd_w_out, lru_conv_w, lru_conv_b, lru_lambda, lru_wa, lru_ba, lru_wx, lru_bx, hy_conv_w, hy_conv_b, hy_f_w1, hy_f_b1, hy_f_w2, hy_f_b2, hy_f_freq, hy_f_w3, hy_bias, final_norm):
    n_p, l_p, _ = x_prompt.shape
    n_s, l_s, _ = x_sample.shape
    t_p = n_p * l_p
    groups = ((0, n_p, l_p), (t_p, n_s, l_s))
    x = jnp.concatenate([x_prompt.reshape(t_p, D_MODEL), x_sample.reshape(n_s * l_s, D_MODEL)], axis=0)

    row = lambda a: a[:, None, :]
    cast = lambda a: a.astype(bf16)
    ffn1 = (row(ffn1_norm), cast(ffn1_w_gate), cast(ffn1_w_up), cast(ffn1_w_down))
    ffn2 = (row(ffn2_norm), cast(ffn2_w_gate), cast(ffn2_w_up), cast(ffn2_w_down))
    mix_g = row(mix_norm)

    ab_in = cast(jnp.pad(ab_w_in, ((0, 0), (0, 0), (0, RW_P_WIDTH - ab_w_in.shape[-1]))))
    ab_out = cast(ab_w_out)
    s5 = dict(lam_re=s5_lambda_re, lam_im=s5_lambda_im, log_step=s5_log_step, b_re=s5_b_re, b_im=s5_b_im,
              c_re=s5_c_re, c_im=s5_c_im, d=row(s5_d), glu_w=cast(s5_glu_w), glu_b=row(s5_glu_b))
    lo = RW_DECAY_LORA
    lora_w = jnp.stack([_pad_rows(rw_w_up[:, 0], 0, RW_LORA_W), _pad_rows(rw_w_up[:, 1], lo, RW_LORA_W),
                        _pad_rows(rw_a_up, 2 * lo, RW_LORA_W),
                        _pad_rows(rw_g_up, 2 * lo + RW_A_LORA, RW_LORA_W)], axis=1)
    rw = dict(mu=row(jnp.pad(rw_mu, ((0, 0), (D_A, RW_P_WIDTH - D_A - rw_mu.shape[-1])))),
              lora_w=cast(lora_w), w0=rw_w0, a0=row(rw_a0), k_k=row(rw_k_k), k_a=row(rw_k_a),
              r_k=rw_r_k.reshape(-1, 1, D_B), ln_w=row(rw_ln_w), ln_b=row(rw_ln_b))

    cd_in = cast(jnp.concatenate([cd_w_in[..., 2 * D_C:], cd_w_in[..., :2 * D_C]], axis=-1))
    cd_out = cast(cd_w_out)
    w_gates = jnp.concatenate([_block_diag_heads(lru_wa[:, 0]), _block_diag_heads(lru_wx[:, 0]),
                               _block_diag_heads(lru_wa[:, 1]), _block_diag_heads(lru_wx[:, 1])], axis=-1)
    b_gates = jnp.concatenate([lru_ba[:, 0], lru_bx[:, 0], lru_ba[:, 1], lru_bx[:, 1]], axis=-1)
    lru = dict(conv_w=lru_conv_w, conv_b=row(lru_conv_b), lam=lru_lambda, w_gates=cast(w_gates),
               b_gates=row(b_gates))
    hy = dict(conv_w=hy_conv_w, conv_b=row(hy_conv_b), f_w1=hy_f_w1, f_b1=hy_f_b1, f_w2=hy_f_w2,
              f_b2=hy_f_b2, f_freq=hy_f_freq, f_w3=hy_f_w3, bias=hy_bias[:, :, None, :])

    for layer in range(DEPTH):
        j = layer // 2
        x = _ffn(x, *ffn1, layer)
        if layer % 2 == 0:
            proj = _proj(x, mix_g, ab_in, layer, j)
            y_a = _s5_mixer(proj, s5, j, groups)
            y_b = _rwkv7_mixer(proj, rw, j, groups)
            x = _outproj(x, y_a, y_b, ab_out, j)
        else:
            proj = _proj(x, mix_g, cd_in, layer, j)
            a_f, b_f, a_b, b_b = _lru_gates(proj, lru, j, groups)
            h_f, h_b = _lru_scan(a_f, b_f, a_b, b_b, groups)
            y_d = _hyena_mixer(proj, hy, j, groups)
            x = _outproj_odd(x, h_f, h_b, proj, y_d, cd_out, j)
        x = _ffn(x, *ffn2, layer)
    y = _final_norm(x, final_norm[None, :])
    return (y[:t_p].reshape(n_p, l_p, D_MODEL), y[t_p:].reshape(n_s, l_s, D_MODEL))
```

```python
import functools
import math

import numpy as np
import jax
import jax.numpy as jnp
from jax import lax
from jax.experimental import pallas as pl
from jax.experimental.pallas import tpu as pltpu

f32 = jnp.float32
bf16 = jnp.bfloat16

D_MODEL = 1024
DEPTH = 4
D_FF = 2816
RMS_EPS = 1e-6
D_A = 512
S5_GROUP = 16
S5_GROUPS = 32
S5_STATE = 64
D_B = 512
RW_HEAD = 64
RW_HEADS = 8
RW_DECAY_LORA = 32
RW_A_LORA = 32
RW_GATE_LORA = 64
RW_GN_EPS = 64e-5
D_C = 512
LRU_HEADS = 8
LRU_HEAD_DIM = 64
LRU_CONV = 4
LRU_C = 8.0
D_D = 512
HY_POS_EMB = 33
HY_FILTER_HIDDEN = 64

LANES = 128
SUBLANES = 8
VMEM_LIMIT = 56 * 1024 * 1024
TOKEN_TILE = 512
FF_CHUNK = 1408


def _cparams(*sem):
    return pltpu.CompilerParams(dimension_semantics=tuple(sem), vmem_limit_bytes=VMEM_LIMIT)


def _const_spec(shape):
    nd = len(shape)
    return pl.BlockSpec(shape, lambda *_: (0,) * nd, pipeline_mode=pl.Buffered(1))


def _layer_spec(shape):
    def make(layer):
        nd = len(shape)
        return pl.BlockSpec((None,) + tuple(shape), lambda *_: (layer,) + (0,) * nd,
                            pipeline_mode=pl.Buffered(1))
    return make


def _rms(x, g):
    ms = jnp.mean(x * x, axis=-1, keepdims=True)
    return x * lax.rsqrt(ms + RMS_EPS) * g


def _dot(a, b):
    return jnp.dot(a, b, preferred_element_type=f32)


_HI = lax.Precision.HIGHEST
_NN = (((1,), (0,)), ((), ()))
_NT = (((1,), (1,)), ((), ()))
_TN = (((0,), (0,)), ((), ()))


def _mm(a, b, prec, dims=_NN):
    dg = functools.partial(lax.dot_general, dimension_numbers=dims, preferred_element_type=f32)
    if prec == "hi":
        return dg(a, b, precision=_HI)
    a1, b1 = a.astype(bf16), b.astype(bf16)
    if prec == "b1":
        return dg(a1, b1)
    a2 = (a - a1.astype(f32)).astype(bf16)
    if prec == "x2":
        return dg(a1, b1) + dg(a2, b1)
    b2 = (b - b1.astype(f32)).astype(bf16)
    return dg(a1, b1) + (dg(a1, b2) + dg(a2, b1))


def _dot_hi(a, b):
    return _mm(a, b, "hi")


def _ffn_body(x_ref, g_ref, wg_ref, wu_ref, wd_ref, o_ref):
    x = x_ref[...]
    h = _rms(x, g_ref[...]).astype(bf16)
    acc = None
    for c in range(D_FF // FF_CHUNK):
        sl = slice(c * FF_CHUNK, (c + 1) * FF_CHUNK)
        gate = _dot(h, wg_ref[:, sl])
        up = _dot(h, wu_ref[:, sl])
        act = (gate * jax.nn.sigmoid(gate) * up).astype(bf16)
        part = _dot(act, wd_ref[sl, :])
        acc = part if acc is None else acc + part
    o_ref[...] = x + 0.5 * acc


def _ffn(x, norm, wg, wu, wd, layer):
    t = x.shape[0]
    return pl.pallas_call(
        _ffn_body,
        grid=(t // TOKEN_TILE,),
        in_specs=[
            pl.BlockSpec((TOKEN_TILE, D_MODEL), lambda i: (i, 0)),
            _layer_spec((1, D_MODEL))(layer),
            _layer_spec((D_MODEL, D_FF))(layer),
            _layer_spec((D_MODEL, D_FF))(layer),
            _layer_spec((D_FF, D_MODEL))(layer),
        ],
        out_specs=pl.BlockSpec((TOKEN_TILE, D_MODEL), lambda i: (i, 0)),
        out_shape=jax.ShapeDtypeStruct((t, D_MODEL), f32),
        compiler_params=_cparams("parallel"),
        name="ffn",
    )(x, norm, wg, wu, wd)


def _proj_body(x_ref, g_ref, w_ref, o_ref):
    h = _rms(x_ref[...], g_ref[...]).astype(bf16)
    o_ref[...] = _dot(h, w_ref[...])


def _proj(x, norm, w, layer, j):
    t = x.shape[0]
    p = w.shape[-1]
    return pl.pallas_call(
        _proj_body,
        grid=(t // TOKEN_TILE,),
        in_specs=[
            pl.BlockSpec((TOKEN_TILE, D_MODEL), lambda i: (i, 0)),
            _layer_spec((1, D_MODEL))(layer),
            _layer_spec((D_MODEL, p))(j),
        ],
        out_specs=pl.BlockSpec((TOKEN_TILE, p), lambda i: (i, 0)),
        out_shape=jax.ShapeDtypeStruct((t, p), f32),
        compiler_params=_cparams("parallel"),
        name="proj",
    )(x, norm, w)


def _outproj_body(x_ref, ya_ref, yb_ref, w_ref, o_ref):
    half = w_ref.shape[0] // 2
    acc = _dot(ya_ref[...].astype(bf16), w_ref[:half, :])
    acc = acc + _dot(yb_ref[...].astype(bf16), w_ref[half:, :])
    o_ref[...] = x_ref[...] + acc


def _outproj(x, ya, yb, w, j):
    t = x.shape[0]
    half = ya.shape[-1]
    return pl.pallas_call(
        _outproj_body,
        grid=(t // TOKEN_TILE,),
        in_specs=[
            pl.BlockSpec((TOKEN_TILE, D_MODEL), lambda i: (i, 0)),
            pl.BlockSpec((TOKEN_TILE, half), lambda i: (i, 0)),
            pl.BlockSpec((TOKEN_TILE, half), lambda i: (i, 0)),
            _layer_spec((2 * half, D_MODEL))(j),
        ],
        out_specs=pl.BlockSpec((TOKEN_TILE, D_MODEL), lambda i: (i, 0)),
        out_shape=jax.ShapeDtypeStruct((t, D_MODEL), f32),
        compiler_params=_cparams("parallel"),
        name="outproj",
    )(x, ya, yb, w)


def _final_norm_body(x_ref, g_ref, o_ref):
    o_ref[...] = _rms(x_ref[...], g_ref[...])


def _final_norm(x, g):
    t = x.shape[0]
    return pl.pallas_call(
        _final_norm_body,
        grid=(t // TOKEN_TILE,),
        in_specs=[pl.BlockSpec((TOKEN_TILE, D_MODEL), lambda i: (i, 0)), _const_spec((1, D_MODEL))],
        out_specs=pl.BlockSpec((TOKEN_TILE, D_MODEL), lambda i: (i, 0)),
        out_shape=jax.ShapeDtypeStruct((t, D_MODEL), f32),
        compiler_params=_cparams("parallel"),
        name="final_norm",
    )(x, g)


def _seq_edge(blk, groups, tile, last):
    hit = None
    for off, n_seq, seq_len in groups:
        start, per, n = off // tile, seq_len // tile, n_seq * (seq_len // tile)
        rel = blk - start
        edge = (per - 1) if last else 0
        h = (rel >= 0) & (rel < n) & (lax.rem(jnp.maximum(rel, 0), per) == edge)
        hit = h if hit is None else (hit | h)
    return hit


S5_BLOCK = 8
S5_ROW = S5_BLOCK * S5_GROUP
S5_OCT = 4
S5_OCT_IN = LANES * S5_BLOCK
S5_CW = 8 * S5_STATE
S5_ZW = 4 * S5_CW


def _s5_weight_body(lr_ref, li_ref, st_ref, btr_ref, bti_ref, cr_ref, ci_ref,
                    kt_ref, er_ref, ei_ref, wor_ref, woi_ref, aqr_ref, aqi_ref):
    backward = pl.program_id(0) >= S5_GROUPS
    lam_re = jnp.minimum(lr_ref[...], -1e-4)
    lam_im = li_ref[...]
    step = jnp.exp(st_ref[...])
    zr, zi = lam_re * step, lam_im * step
    mag = jnp.exp(zr)
    a_re, a_im = mag * jnp.cos(zi), mag * jnp.sin(zi)
    den = lam_re * lam_re + lam_im * lam_im
    num_re = a_re - 1.0
    coef_re = (num_re * lam_re + a_im * lam_im) / den
    coef_im = (a_im * lam_re - num_re * lam_im) / den
    bt_re, bt_im = btr_ref[...], bti_ref[...]
    bb_re = coef_re * bt_re - coef_im * bt_im
    bb_im = coef_re * bt_im + coef_im * bt_re

    def power(ell):
        m = jnp.exp(ell * zr)
        return m * jnp.cos(ell * zi), m * jnp.sin(ell * zi)

    tile = lambda a: jnp.concatenate([a] * S5_BLOCK, axis=0)
    lag = (lax.broadcasted_iota(jnp.int32, (S5_ROW, S5_STATE), 0) // S5_GROUP).astype(f32)
    p_re, p_im = power(lag)
    tb_re, tb_im = tile(bb_re), tile(bb_im)
    e_re = p_re * tb_re - p_im * tb_im
    e_im = p_re * tb_im + p_im * tb_re
    er_ref[...] = e_re
    ei_ref[...] = e_im
    c_re, c_im = cr_ref[...], ci_ref[...]
    kt_ref[...] = _mm(e_re, c_re, "hi", _NT) - _mm(e_im, c_im, "hi", _NT)
    lag_out = jnp.where(backward, S5_BLOCK - lag, lag + 1.0)
    q_re, q_im = power(lag_out)
    tc_re, tc_im = tile(c_re), tile(c_im)
    wor_ref[...] = tc_re * q_re - tc_im * q_im
    woi_ref[...] = -(tc_re * q_im + tc_im * q_re)
    jj = (lax.broadcasted_iota(jnp.int32, (SUBLANES, S5_STATE), 0) + 1).astype(f32) * S5_BLOCK
    aqr_ref[...], aqi_ref[...] = power(jj)


def _s5_weights(lam_re, lam_im, log_step, b_re, b_im, c_re, c_im):
    dg = 2 * S5_GROUPS
    vec = lambda a: a.reshape(dg, 1, S5_STATE)
    st = jnp.broadcast_to(log_step.reshape(dg, 1, 1), (dg, 1, S5_STATE))
    bt = lambda b: jnp.swapaxes(b, -1, -2).reshape(dg, S5_GROUP, S5_STATE)
    cc = lambda c: c.reshape(dg, S5_GROUP, S5_STATE)
    spec = lambda r, c: pl.BlockSpec((None, r, c), lambda i: (i, 0, 0))
    out = lambda r, c: jax.ShapeDtypeStruct((dg, r, c), f32)
    return pl.pallas_call(
        _s5_weight_body,
        grid=(dg,),
        in_specs=[spec(1, S5_STATE)] * 3 + [spec(S5_GROUP, S5_STATE)] * 4,
        out_specs=[spec(S5_ROW, S5_GROUP)] + [spec(S5_ROW, S5_STATE)] * 4 + [spec(SUBLANES, S5_STATE)] * 2,
        out_shape=[out(S5_ROW, S5_GROUP)] + [out(S5_ROW, S5_STATE)] * 4 + [out(SUBLANES, S5_STATE)] * 2,
        compiler_params=_cparams("parallel"),
        name="s5_weights",
    )(vec(lam_re), vec(lam_im), st, bt(b_re), bt(b_im), cc(c_re), cc(c_im))


def _s5_assemble(kt, e_re, e_im, wo_re, wo_im, aq_re, aq_im):
    g, t, c, n = S5_GROUPS, S5_BLOCK, S5_GROUP, S5_STATE
    eye = jnp.eye(8, dtype=f32)
    k = kt.reshape(2, g, t, c, c)
    lag = jnp.arange(t)[None, :] - jnp.arange(t)[:, None]
    fwd = jnp.where((lag >= 0)[None, :, :, None, None], k[0][:, jnp.clip(lag, 0, t - 1)], 0.0)
    bwd = jnp.where((lag <= 0)[None, :, :, None, None], k[1][:, jnp.clip(-lag, 0, t - 1)], 0.0)
    toep = (fwd + bwd).reshape(S5_OCT, 8, t, t, c, c)
    toep = jnp.einsum("qgstic,gh->qsgithc", toep, eye).reshape(S5_OCT, S5_OCT_IN, S5_OCT_IN)

    def spread(a):
        a = a.reshape(S5_OCT, 8, t, c, n)
        return jnp.einsum("qgsin,gh->qsgihn", a, eye).reshape(S5_OCT, S5_OCT_IN, S5_CW)

    e = lambda a: a.reshape(2, g, t, c, n)
    inj = jnp.concatenate([spread(e(e_re)[0][:, ::-1]), spread(e(e_im)[0][:, ::-1]),
                           spread(e(e_re)[1]), spread(e(e_im)[1])], axis=-1)
    out_t = jnp.concatenate([spread(e(wo_re)[0]), spread(e(wo_im)[0]),
                             spread(e(wo_re)[1]), spread(e(wo_im)[1])], axis=-1)
    tab = lambda x: x.reshape(2, S5_OCT, 8, SUBLANES, n).transpose(0, 1, 3, 2, 4).reshape(2, S5_OCT, SUBLANES, S5_CW)
    tab_re = jnp.concatenate([tab(aq_re)[0], tab(aq_re)[1]], axis=-1)
    tab_im = jnp.concatenate([tab(aq_im)[0], tab(aq_im)[1]], axis=-1)
    return toep.astype(bf16), inj.astype(bf16), out_t.astype(bf16), tab_re, tab_im


def _s5_conv_body(*refs, groups):
    x_refs = refs[:S5_BLOCK]
    m_ref, inj_ref, out_ref, tr_ref, ti_ref, o_ref, z_buf, h_buf = refs[S5_BLOCK:]
    n_tiles = o_ref.shape[0] // SUBLANES
    tile0 = pl.program_id(1) * n_tiles
    tile_tokens = SUBLANES * S5_BLOCK
    xb = jnp.concatenate([r[...].astype(bf16) for r in x_refs], axis=1)
    z_buf[...] = _dot(xb, inj_ref[...])
    row = lax.broadcasted_iota(jnp.int32, (SUBLANES, S5_CW), 0)
    shape = (SUBLANES, S5_CW)

    def tables(reverse):
        lanes = slice(S5_CW, 2 * S5_CW) if reverse else slice(0, S5_CW)
        full = lambda r: (jnp.broadcast_to(tr_ref[r:r + 1, lanes], shape),
                          jnp.broadcast_to(ti_ref[r:r + 1, lanes], shape))
        levels = []
        for d in (1, 2, 4):
            ar, ai = full(d - 1)
            keep = (row < SUBLANES - d) if reverse else (row >= d)
            levels.append((d, jnp.where(keep, ar, 0.0), jnp.where(keep, ai, 0.0)))
        cr, ci = jnp.zeros(shape, f32), jnp.zeros(shape, f32)
        for r in range(SUBLANES):
            ar, ai = full(r)
            sel = (row == (SUBLANES - 1 - r)) if reverse else (row == r)
            cr, ci = jnp.where(sel, ar, cr), jnp.where(sel, ai, ci)
        return levels, cr, ci

    tabs = (tables(False), tables(True))

    def scan_tile(t, carry, reverse):
        levels, cr, ci = tabs[int(reverse)]
        base = 2 * S5_CW * int(reverse)
        re_l, im_l = slice(base, base + S5_CW), slice(base + S5_CW, base + 2 * S5_CW)
        fresh = _seq_edge(tile0 + t, groups, tile_tokens, last=reverse)
        c_re, c_im = (jnp.where(fresh, 0.0, c) for c in carry)
        rows = pl.ds(pl.multiple_of(t * SUBLANES, SUBLANES), SUBLANES)
        h_re, h_im = z_buf[rows, re_l], z_buf[rows, im_l]
        for d, ar, ai in levels:
            sh = (SUBLANES - d) if reverse else d
            s_re, s_im = pltpu.roll(h_re, sh, 0), pltpu.roll(h_im, sh, 0)
            h_re, h_im = h_re + (ar * s_re - ai * s_im), h_im + (ar * s_im + ai * s_re)
        h_re, h_im = h_re + (cr * c_re - ci * c_im), h_im + (cr * c_im + ci * c_re)
        edge_in = SUBLANES - 1 if reverse else 0
        sh = (SUBLANES - 1) if reverse else 1
        h_buf[rows, re_l] = jnp.where(row == edge_in, c_re, pltpu.roll(h_re, sh, 0))
        h_buf[rows, im_l] = jnp.where(row == edge_in, c_im, pltpu.roll(h_im, sh, 0))
        edge_out = 0 if reverse else SUBLANES - 1
        return (jnp.broadcast_to(h_re[edge_out:edge_out + 1, :], shape),
                jnp.broadcast_to(h_im[edge_out:edge_out + 1, :], shape))

    def tile_pair(i, carry):
        return (scan_tile(i, carry[0], False), scan_tile(n_tiles - 1 - i, carry[1], True))

    zero = (jnp.zeros(shape, f32), jnp.zeros(shape, f32))
    lax.fori_loop(0, n_tiles, tile_pair, (zero, zero))

    y = _dot(xb, m_ref[...])
    y = y + lax.dot_general(h_buf[...].astype(bf16), out_ref[...], _NT, preferred_element_type=f32)
    o_ref[...] = y


def _s5_conv(proj, toep, inj, out_t, tab_re, tab_im, groups):
    t, width = proj.shape
    n_rows = t // S5_BLOCK
    rows = max(seq_len for _, _, seq_len in groups) // S5_BLOCK
    col_blocks = width // LANES
    x_view = proj.reshape(n_rows, S5_BLOCK * width)
    x_specs = [pl.BlockSpec((rows, LANES), lambda q, r, s=s: (r, s * col_blocks + q)) for s in range(S5_BLOCK)]
    per = lambda r, c: pl.BlockSpec((None, r, c), lambda q, r_: (q, 0, 0), pipeline_mode=pl.Buffered(1))
    y = pl.pallas_call(
        functools.partial(_s5_conv_body, groups=groups),
        grid=(S5_OCT, n_rows // rows),
        in_specs=x_specs + [per(S5_OCT_IN, S5_OCT_IN), per(S5_OCT_IN, S5_ZW), per(S5_OCT_IN, S5_ZW),
                            per(SUBLANES, 2 * S5_CW), per(SUBLANES, 2 * S5_CW)],
        out_specs=pl.BlockSpec((None, rows, S5_OCT_IN), lambda q, r: (q, r, 0)),
        out_shape=jax.ShapeDtypeStruct((S5_OCT, n_rows, S5_OCT_IN), f32),
        scratch_shapes=[pltpu.VMEM((rows, S5_ZW), f32), pltpu.VMEM((rows, S5_ZW), f32)],
        compiler_params=_cparams("parallel", "parallel"),
        name="s5_conv",
    )(*([x_view] * S5_BLOCK), toep, inj, out_t, tab_re, tab_im)
    return y.reshape(S5_OCT, t, LANES)


def _s5_post_body(u_ref, y0_ref, y1_ref, y2_ref, y3_ref, d_ref, w_ref, b_ref, o_ref):
    y = jnp.concatenate([r[...] for r in (y0_ref, y1_ref, y2_ref, y3_ref)], axis=1)
    y = y + d_ref[...] * u_ref[...]
    y = jax.nn.gelu(y)
    gate = _dot(y.astype(bf16), w_ref[...]) + b_ref[...]
    o_ref[...] = y * jax.nn.sigmoid(gate)


def _s5_post(proj, y, d, glu_w, glu_b, j):
    t = proj.shape[0]
    tok = lambda i: (i, 0)
    return pl.pallas_call(
        _s5_post_body,
        grid=(t // TOKEN_TILE,),
        in_specs=[
            pl.BlockSpec((TOKEN_TILE, D_A), tok),
        ] + [pl.BlockSpec((None, TOKEN_TILE, LANES), lambda i, q=q: (q, i, 0)) for q in range(S5_OCT)] + [
            _layer_spec((1, D_A))(j),
            _layer_spec((D_A, D_A))(j),
            _layer_spec((1, D_A))(j),
        ],
        out_specs=pl.BlockSpec((TOKEN_TILE, D_A), tok),
        out_shape=jax.ShapeDtypeStruct((t, D_A), f32),
        compiler_params=_cparams("parallel"),
        name="s5_post",
    )(proj, y, y, y, y, d, glu_w, glu_b)


def _s5_mixer(proj, prm, j, groups):
    w = _s5_weights(prm["lam_re"][j], prm["lam_im"][j], prm["log_step"][j], prm["b_re"][j], prm["b_im"][j],
                    prm["c_re"][j], prm["c_im"][j])
    y = _s5_conv(proj, *_s5_assemble(*w), groups)
    return _s5_post(proj, y, prm["d"], prm["glu_w"], prm["glu_b"], j)


RW_TILE = 256
RW_CHUNK = 64
RW_PAIRS = RW_HEADS // 2
RW_CHUNKS_PER_STEP = 2
RW_PREC_A = "b1"
RW_PREC_INV = "b1"
RW_PREC_APPLY = "b1"
RW_PREC_SEQ = "b1"
RW_PREC_SUM = "x2"
RW_P_WIDTH = 2304
RW_LORA_OFF = 2048
RW_LORA_W = 256


def _head_ones():
    a = lax.broadcasted_iota(jnp.int32, (D_B, D_B), 0) // RW_HEAD
    b = lax.broadcasted_iota(jnp.int32, (D_B, D_B), 1) // RW_HEAD
    return jnp.where(a == b, 1.0, 0.0).astype(f32)


def _rw_prep_body(x_ref, xp_ref, xn_ref, mu_ref, lw_ref, w0_ref, a0_ref, kk_ref, ka_ref,
                  r_out, k_out, v_out, kk_out, kka_out, lwf_out, lwb_out, g_out, *, groups):
    i = pl.program_id(0)
    x = x_ref[:, D_A:]
    prev = jnp.where(_seq_edge(i, groups, RW_TILE, last=False), 0.0, xp_ref[:, D_A:])
    nxt = jnp.where(_seq_edge(i, groups, RW_TILE, last=True), 0.0, xn_ref[:, D_A:])
    row = lax.broadcasted_iota(jnp.int32, x.shape, 0)
    shifted = 0.5 * (_shift_rows(x, prev, 1, row) + _shift_rows(x, nxt, -1, row))
    p = x + (shifted - x) * mu_ref[:, D_A:]
    r = p[:, :D_B]
    k = p[:, D_B:2 * D_B]
    v = p[:, 2 * D_B:3 * D_B]
    lora = p[:, RW_LORA_OFF - D_A:RW_LORA_OFF - D_A + RW_LORA_W]
    lora_t = jnp.tanh(lora).astype(bf16)
    a = jax.nn.sigmoid(a0_ref[...] + _dot(lora.astype(bf16), lw_ref[2]))
    g = _dot(jax.nn.sigmoid(lora).astype(bf16), lw_ref[3])
    kx = k * kk_ref[...]
    ss = _mm(kx * kx, _head_ones(), RW_PREC_SUM)
    kk = kx / jnp.maximum(jnp.sqrt(ss), 1e-12)
    r_out[...] = r
    k_out[...] = k * (1.0 + (a - 1.0) * ka_ref[...])
    v_out[...] = v
    kk_out[...] = kk
    kka_out[...] = kk * a
    g_out[...] = g
    for direction, out in enumerate((lwf_out, lwb_out)):
        wl = -jax.nn.softplus(-(w0_ref[direction:direction + 1, :] + _dot(lora_t, lw_ref[direction]))) - 0.5
        out[...] = -jnp.exp(wl)


def _rw_prep(proj, prm, j, groups):
    t = proj.shape[0]
    tok = lambda i: (i, 0)
    prev_spec, next_spec = _halo_specs(RW_TILE, RW_P_WIDTH, 0, t)
    out = jax.ShapeDtypeStruct((t, D_B), f32)
    return pl.pallas_call(
        functools.partial(_rw_prep_body, groups=groups),
        grid=(t // RW_TILE,),
        in_specs=[pl.BlockSpec((RW_TILE, RW_P_WIDTH), tok), prev_spec, next_spec,
                  _layer_spec((1, RW_P_WIDTH))(j), _layer_spec((4, RW_LORA_W, D_B))(j),
                  _layer_spec((2, D_B))(j), _layer_spec((1, D_B))(j), _layer_spec((1, D_B))(j),
                  _layer_spec((1, D_B))(j)],
        out_specs=[pl.BlockSpec((RW_TILE, D_B), tok)] * 8,
        out_shape=[out] * 8,
        compiler_params=_cparams("parallel"),
        name="rw_prep",
    )(proj, proj, proj, prm["mu"], prm["lora_w"], prm["w0"], prm["a0"], prm["k_k"], prm["k_a"])


def _rw_chunk_body(r_ref, k_ref, v_ref, kk_ref, kka_ref, lw_ref, y_out, st, *, reverse, groups, n_steps):
    c = RW_CHUNK
    pid = pl.program_id(0)

    @pl.when(pid == 0)
    def _():
        st[...] = jnp.zeros_like(st)

    ri = lax.broadcasted_iota(jnp.int32, (c, c), 0)
    ci = lax.broadcasted_iota(jnp.int32, (c, c), 1)
    tri = jnp.where((ci >= ri) if reverse else (ci <= ri), 1.0, 0.0).astype(f32)
    edge = 0 if reverse else c - 1
    big_r = lax.broadcasted_iota(jnp.int32, (LANES, LANES), 0)
    big_c = lax.broadcasted_iota(jnp.int32, (LANES, LANES), 1)
    same_head = (big_r // RW_HEAD) == (big_c // RW_HEAD)
    s_idx, j_idx = big_r % RW_HEAD, big_c % RW_HEAD
    strict = same_head & ((j_idx > s_idx) if reverse else (j_idx < s_idx))
    incl = same_head & ((j_idx >= s_idx) if reverse else (j_idx <= s_idx))
    eye = big_r == big_c
    eye_f = jnp.where(eye, 1.0, 0.0).astype(f32)
    head0 = lax.broadcasted_iota(jnp.int32, (c, LANES), 1) < RW_HEAD

    def expand(x):
        return jnp.concatenate([jnp.where(head0, x, 0.0), jnp.where(head0, 0.0, x)], axis=0)

    def collapse(x):
        return x[:c] + x[c:]

    n_sub = r_ref.shape[0] // c
    lanes = [slice(p * LANES, (p + 1) * LANES) for p in range(RW_PAIRS)]
    rows = [slice(ch * c, (ch + 1) * c) for ch in range(n_sub)]
    units = [(ch, p) for ch in range(n_sub) for p in range(RW_PAIRS)]
    pre = []
    for rs in rows:
        lw = lw_ref[rs, :]
        cum = _dot_hi(tri, lw)
        total = cum[edge:edge + 1, :]
        e_neg, tail = jnp.exp(-cum), jnp.exp(total - cum)
        k, kka = k_ref[rs, :], kka_ref[rs, :]
        pre.append(dict(at=-kk_ref[rs, :] * jnp.exp(cum - lw), rt=r_ref[rs, :] * jnp.exp(cum),
                        kh=k * e_neg, bh=kka * e_neg, kp=k * tail, bp=kka * tail, v=v_ref[rs, :],
                        pc=jnp.exp(total)))

    def get(name):
        return [pre[ch][name][:, lanes[p]] for ch, p in units]

    at, rt, kh, bh, kp, bp, v = (get(n) for n in ("at", "rt", "kh", "bh", "kp", "bp", "v"))
    at_exp = [expand(x) for x in at]
    v_exp = [expand(x) for x in v]
    z = [_mm(jnp.concatenate([ae, expand(rr)], axis=0), jnp.concatenate([a, a, b, b], axis=0), RW_PREC_A, _NT)
         for ae, rr, a, b in zip(at_exp, rt, kh, bh)]
    a_ak = [jnp.where(strict, zz[:LANES, :LANES], 0.0) for zz in z]
    a_ab = [jnp.where(strict, zz[:LANES, LANES:], 0.0) for zz in z]
    a_rk = [jnp.where(incl, zz[LANES:, :LANES], 0.0) for zz in z]
    a_rb = [jnp.where(incl, zz[LANES:, LANES:], 0.0) for zz in z]
    tinv = [eye_f + a for a in a_ab]
    pw = a_ab
    for _ in range(int(math.log2(c)) - 1):
        pw = [_mm(q, q, RW_PREC_INV) for q in pw]
        tinv = [_mm(t, eye_f + q, RW_PREC_INV) for t, q in zip(tinv, pw)]
    w = [_mm(a, ve, RW_PREC_APPLY) for a, ve in zip(a_ak, v_exp)]
    tu = [_mm(t, jnp.concatenate([ae, ww], axis=1), RW_PREC_APPLY) for t, ae, ww in zip(tinv, at_exp, w)]
    atp_exp = [x[:, :LANES] for x in tu]
    u0_exp = [x[:, LANES:] for x in tu]
    y0 = [collapse(_mm(ark, ve, RW_PREC_APPLY) + _mm(arb, ue, RW_PREC_APPLY))
          for ark, arb, ve, ue in zip(a_rk, a_rb, v_exp, u0_exp)]
    rp = [rr + collapse(_mm(arb, ae, RW_PREC_APPLY)) for rr, arb, ae in zip(rt, a_rb, atp_exp)]
    atp = [collapse(x) for x in atp_exp]
    u0 = [collapse(x) for x in u0_exp]
    g_full = [_mm(b, a, RW_PREC_APPLY, _TN) for b, a in zip(bp, atp)]
    h_full = [_mm(kd, vv, RW_PREC_APPLY, _TN) + _mm(b, u, RW_PREC_APPLY, _TN)
              for kd, vv, b, u in zip(kp, v, bp, u0)]
    blk = (n_steps - 1 - pid) if reverse else pid
    state = [st[:, sl] for sl in lanes]
    for ch in (range(n_sub - 1, -1, -1) if reverse else range(n_sub)):
        fresh = _seq_edge(blk * n_sub + ch, groups, c, last=reverse)
        for p in range(RW_PAIRS):
            i, sl = ch * RW_PAIRS + p, lanes[p]
            pc = jnp.broadcast_to(pre[ch]["pc"][:, sl], (LANES, LANES))
            g_bd = jnp.where(same_head, g_full[i], 0.0) + jnp.where(eye, pc, 0.0)
            h_bd = jnp.where(same_head, h_full[i], 0.0)
            s0 = jnp.where(fresh, 0.0, state[p])
            y_out[rows[ch], sl] = _mm(rp[i], s0, RW_PREC_SEQ) + y0[i]
            state[p] = _mm(g_bd, s0, RW_PREC_SEQ) + h_bd
    for p in range(RW_PAIRS):
        st[:, lanes[p]] = state[p]


def _rw_chunk(r, k, v, kk, kka, lw, groups, reverse):
    t = r.shape[0]
    step_rows = RW_CHUNK * RW_CHUNKS_PER_STEP
    n_steps = t // step_rows
    order = (lambda i: (n_steps - 1 - i, 0)) if reverse else (lambda i: (i, 0))
    return pl.pallas_call(
        functools.partial(_rw_chunk_body, reverse=reverse, groups=groups, n_steps=n_steps),
        grid=(n_steps,),
        in_specs=[pl.BlockSpec((step_rows, D_B), order)] * 6,
        out_specs=pl.BlockSpec((step_rows, D_B), order),
        out_shape=jax.ShapeDtypeStruct((t, D_B), f32),
        scratch_shapes=[pltpu.VMEM((LANES, D_B), f32)],
        compiler_params=_cparams("arbitrary"),
        name="rw_chunk_rev" if reverse else "rw_chunk_fwd",
    )(r, k, v, kk, kka, lw)


def _rw_post_body(yf_ref, yb_ref, r_ref, k_ref, v_ref, g_ref, rk_ref, lnw_ref, lnb_ref, o_ref):
    ones = _head_ones()
    inv = 1.0 / RW_HEAD
    y = yf_ref[...] + yb_ref[...]
    mean = _mm(y, ones, RW_PREC_SUM) * inv
    yc = y - mean
    var = _mm(yc * yc, ones, RW_PREC_SUM) * inv
    yn = yc * lax.rsqrt(var + RW_GN_EPS) * lnw_ref[...] + lnb_ref[...]
    bonus = _mm(r_ref[...] * k_ref[...] * rk_ref[...], ones, RW_PREC_SUM) * v_ref[...]
    o_ref[...] = (yn + bonus) * g_ref[...]


def _rw_post(y_f, y_b, r, k, v, g, prm, j):
    t = y_f.shape[0]
    tok = pl.BlockSpec((TOKEN_TILE, D_B), lambda i: (i, 0))
    vec = _layer_spec((1, D_B))(j)
    return pl.pallas_call(
        _rw_post_body,
        grid=(t // TOKEN_TILE,),
        in_specs=[tok] * 6 + [vec] * 3,
        out_specs=tok,
        out_shape=jax.ShapeDtypeStruct((t, D_B), f32),
        compiler_params=_cparams("parallel"),
        name="rw_post",
    )(y_f, y_b, r, k, v, g, prm["r_k"], prm["ln_w"], prm["ln_b"])


def _rwkv7_mixer(proj, prm, j, groups):
    r, k, v, kk, kka, lw_f, lw_b, g = _rw_prep(proj, prm, j, groups)
    y_f = _rw_chunk(r, k, v, kk, kka, lw_f, groups, reverse=False)
    y_b = _rw_chunk(r, k, v, kk, kka, lw_b, groups, reverse=True)
    return _rw_post(y_f, y_b, r, k, v, g, prm, j)


def _halo_specs(tile, width, col_blk, n_rows):
    per = tile // SUBLANES
    last8 = n_rows // SUBLANES - 1
    prev = pl.BlockSpec((SUBLANES, width), lambda i: (jnp.maximum(i * per - 1, 0), col_blk))
    nxt = pl.BlockSpec((SUBLANES, width), lambda i: (jnp.minimum((i + 1) * per, last8), col_blk))
    return prev, nxt


def _shift_rows(x, halo, k, row):
    n = x.shape[0]
    if k > 0:
        y = pltpu.roll(x, k, 0)
        for r in range(k):
            y = jnp.where(row == r, halo[SUBLANES - k + r:SUBLANES - k + r + 1, :], y)
        return y
    y = pltpu.roll(x, n - 1, 0)
    return jnp.where(row == n - 1, halo[0:1, :], y)


LRU_TILE = 512
CD_XB_BLK = 3
CD_GB_BLK = 4


def _lru_gate_body(x_ref, xp_ref, xn_ref, cw_ref, cb_ref, lam_ref, w_ref, b_ref,
                   af_ref, bf_ref, ab_ref, bb_ref, *, groups):
    i = pl.program_id(0)
    first = _seq_edge(i, groups, LRU_TILE, last=False)
    last = _seq_edge(i, groups, LRU_TILE, last=True)
    x = x_ref[...]
    prev = jnp.where(first, 0.0, xp_ref[...])
    nxt = jnp.where(last, 0.0, xn_ref[...])
    row = lax.broadcasted_iota(jnp.int32, x.shape, 0)
    xc = (cw_ref[0:1, :] * _shift_rows(x, prev, 2, row) + cw_ref[1:2, :] * _shift_rows(x, prev, 1, row)
          + cw_ref[2:3, :] * x + cw_ref[3:4, :] * _shift_rows(x, nxt, -1, row) + cb_ref[...])
    pre = _dot(xc.astype(bf16), w_ref[...]) + b_ref[...]
    outs = ((af_ref, bf_ref), (ab_ref, bb_ref))
    for direction in range(2):
        base = 2 * direction * D_C
        gate_r = jax.nn.sigmoid(pre[:, base:base + D_C])
        gate_i = jax.nn.sigmoid(pre[:, base + D_C:base + 2 * D_C])
        log_a = -LRU_C * gate_r * jax.nn.softplus(-lam_ref[direction:direction + 1, :])
        t = jnp.tanh(log_a)
        mult = jnp.sqrt(-2.0 * t / (1.0 - t))
        a_ref, b_ref_out = outs[direction]
        a_ref[...] = jnp.exp(log_a)
        b_ref_out[...] = mult * gate_i * xc


def _lru_gates(proj, prm, j, groups):
    t = proj.shape[0]
    tok = lambda i: (i, 0)
    prev_spec, next_spec = _halo_specs(LRU_TILE, D_C, CD_XB_BLK, t)
    out = jax.ShapeDtypeStruct((t, D_C), f32)
    return pl.pallas_call(
        functools.partial(_lru_gate_body, groups=groups),
        grid=(t // LRU_TILE,),
        in_specs=[
            pl.BlockSpec((LRU_TILE, D_C), lambda i: (i, CD_XB_BLK)),
            prev_spec,
            next_spec,
            _layer_spec((LRU_CONV, D_C))(j),
            _layer_spec((1, D_C))(j),
            _layer_spec((2, D_C))(j),
            _layer_spec((D_C, 4 * D_C))(j),
            _layer_spec((1, 4 * D_C))(j),
        ],
        out_specs=[pl.BlockSpec((LRU_TILE, D_C), tok)] * 4,
        out_shape=[out] * 4,
        compiler_params=_cparams("parallel"),
        name="lru_gates",
    )(proj, proj, proj, prm["conv_w"], prm["conv_b"], prm["lam"], prm["w_gates"], prm["b_gates"])


def _lru_scan_body(af_ref, bf_ref, ab_ref, bb_ref, hf_ref, hb_ref, carry_f, carry_b, *, groups, n_blk):
    i = pl.program_id(0)

    @pl.when(_seq_edge(i, groups, LRU_TILE, last=False))
    def _():
        carry_f[...] = jnp.zeros_like(carry_f)

    @pl.when(_seq_edge(n_blk - 1 - i, groups, LRU_TILE, last=True))
    def _():
        carry_b[...] = jnp.zeros_like(carry_b)

    row = lax.broadcasted_iota(jnp.int32, (SUBLANES, D_C), 0)
    n_tiles = LRU_TILE // SUBLANES

    def run(a_ref, b_ref, h_ref, carry, reverse):
        def tile_step(k, c):
            t = (n_tiles - 1 - k) if reverse else k
            r0 = pl.multiple_of(t * SUBLANES, SUBLANES)
            a = a_ref[pl.ds(r0, SUBLANES), :]
            b = b_ref[pl.ds(r0, SUBLANES), :]
            for d in (1, 2, 4):
                sh = (SUBLANES - d) if reverse else d
                keep = (row < SUBLANES - d) if reverse else (row >= d)
                a_s = jnp.where(keep, pltpu.roll(a, sh, 0), 1.0)
                b_s = jnp.where(keep, pltpu.roll(b, sh, 0), 0.0)
                b = b + a * b_s
                a = a * a_s
            h = b + a * c
            h_ref[pl.ds(r0, SUBLANES), :] = h
            edge = 0 if reverse else SUBLANES - 1
            return jnp.broadcast_to(h[edge:edge + 1, :], (SUBLANES, D_C))

        carry[...] = lax.fori_loop(0, n_tiles, tile_step, carry[...])

    run(af_ref, bf_ref, hf_ref, carry_f, False)
    run(ab_ref, bb_ref, hb_ref, carry_b, True)


def _lru_scan(a_f, b_f, a_b, b_b, groups):
    t = a_f.shape[0]
    n_blk = t // LRU_TILE
    fwd = lambda i: (i, 0)
    bwd = lambda i: (n_blk - 1 - i, 0)
    out = jax.ShapeDtypeStruct((t, D_C), f32)
    return pl.pallas_call(
        functools.partial(_lru_scan_body, groups=groups, n_blk=n_blk),
        grid=(n_blk,),
        in_specs=[pl.BlockSpec((LRU_TILE, D_C), fwd), pl.BlockSpec((LRU_TILE, D_C), fwd),
                  pl.BlockSpec((LRU_TILE, D_C), bwd), pl.BlockSpec((LRU_TILE, D_C), bwd)],
        out_specs=[pl.BlockSpec((LRU_TILE, D_C), fwd), pl.BlockSpec((LRU_TILE, D_C), bwd)],
        out_shape=[out, out],
        scratch_shapes=[pltpu.VMEM((SUBLANES, D_C), f32), pltpu.VMEM((SUBLANES, D_C), f32)],
        compiler_params=_cparams("arbitrary"),
        name="lru_scan",
    )(a_f, b_f, a_b, b_b)


HY_N2 = 256
HY_CW = 128
HY_SC_TILE = 512
HY_MLP_TILE = 512
HY_PREC_S1 = "b1"
HY_PREC_S2 = "b1"
HY_S1_UNROLL = 2


def _hy_dims(seq_len):
    n = 2 * seq_len
    n1 = n // HY_N2
    lb = n1 // 2
    k1 = n1 // 2 + 1
    k1p = -(-k1 // SUBLANES) * SUBLANES
    return n, n1, lb, k1, k1p


def _hy_tables(seq_len):
    n, n1, lb, k1, k1p = _hy_dims(seq_len)
    two_pi = 2.0 * math.pi
    kk = np.arange(k1, dtype=np.float64)[:, None]
    nn = np.arange(lb, dtype=np.float64)[None, :]
    ang1 = two_pi * kk * nn / n1
    f1 = np.zeros((2 * k1p, lb), np.float64)
    f1[:k1] = np.cos(ang1)
    f1[k1p:k1p + k1] = -np.sin(ang1)
    weight = np.where((np.arange(k1) == 0) | (np.arange(k1) == n1 // 2), 1.0, 2.0)[None, :] / n
    f1inv = np.zeros((lb, 2 * k1p), np.float64)
    f1inv[:, :k1] = weight * np.cos(ang1.T)
    f1inv[:, k1p:k1p + k1] = -weight * np.sin(ang1.T)
    idx = np.arange(HY_N2, dtype=np.float64)
    ang2 = two_pi * np.outer(idx, idx) / HY_N2
    f2 = np.concatenate([np.cos(ang2), -np.sin(ang2)], axis=0)
    prod = (jnp.arange(k1, dtype=jnp.int32)[:, None] * jnp.arange(HY_N2, dtype=jnp.int32)[None, :]) % n
    ang = prod.astype(f32) * f32(two_pi / n)
    bc = lambda a: jnp.broadcast_to(a.reshape(k1 * HY_N2, 1), (k1 * HY_N2, HY_CW))
    eye8 = np.eye(SUBLANES)
    return dict(f1=jnp.asarray(np.kron(f1, eye8), f32), f1inv=jnp.asarray(np.kron(f1inv, eye8), f32),
                f2=jnp.asarray(f2, f32), tw_re=bc(jnp.cos(ang)), tw_im=bc(-jnp.sin(ang)))


def _hy_slab(k):
    return pl.ds(k * HY_N2 if isinstance(k, int) else pl.multiple_of(k * HY_N2, HY_N2), HY_N2)


def _hy_slab_loop(slab, k1):
    def pair(i, carry):
        slab(2 * i, carry)
        slab(2 * i + 1, carry)
        return carry

    lax.fori_loop(0, k1 // 2, pair, 0)
    if k1 % 2:
        slab(k1 - 1, 0)


def _hy_forward(z_ref, f1_ref, f2_ref, twr_ref, twi_ref, wr, wi, dims):
    _, _, lb, k1, k1p = dims

    def stage1_group(n2):
        tiles = [z_ref[pl.ds(pl.multiple_of(b * HY_N2 + n2, SUBLANES), SUBLANES), :] for b in range(lb)]
        y = _mm(f1_ref[...], jnp.concatenate(tiles, axis=0), HY_PREC_S1)
        for s in range(k1p):
            dst = pl.ds(pl.multiple_of(s * HY_N2 + n2, SUBLANES), SUBLANES)
            wr[dst, :] = y[s * SUBLANES:(s + 1) * SUBLANES]
            wi[dst, :] = y[(k1p + s) * SUBLANES:(k1p + s + 1) * SUBLANES]

    def stage1(g, carry):
        for u in range(HY_S1_UNROLL):
            stage1_group((g * HY_S1_UNROLL + u) * SUBLANES)
        return carry

    lax.fori_loop(0, HY_N2 // (SUBLANES * HY_S1_UNROLL), stage1, 0)

    def stage2(k, carry):
        sl = _hy_slab(k)
        yr, yi, tr, ti = wr[sl, :], wi[sl, :], twr_ref[sl, :], twi_ref[sl, :]
        ar = yr * tr - yi * ti
        ai = yr * ti + yi * tr
        pq = _mm(f2_ref[...], jnp.concatenate([ar, ai], axis=1), HY_PREC_S2)
        p, q = pq[:, :HY_CW], pq[:, HY_CW:]
        wr[sl, :] = p[:HY_N2] - q[HY_N2:]
        wi[sl, :] = p[HY_N2:] + q[:HY_N2]
        return carry

    _hy_slab_loop(stage2, k1)


def _hy_inverse(hr_ref, hi_ref, f1inv_ref, f2_ref, twr_ref, twi_ref, wr, wi, o_ref, dims):
    _, _, lb, k1, k1p = dims

    def stage2(k, carry):
        sl = _hy_slab(k)
        xr, xi, hr, hi = wr[sl, :], wi[sl, :], hr_ref[sl, :], hi_ref[sl, :]
        zr = xr * hr - xi * hi
        zi = xr * hi + xi * hr
        pq = _mm(f2_ref[...], jnp.concatenate([zr, zi], axis=1), HY_PREC_S2)
        p, q = pq[:, :HY_CW], pq[:, HY_CW:]
        vr = p[:HY_N2] + q[HY_N2:]
        vi = q[:HY_N2] - p[HY_N2:]
        tr, ti = twr_ref[sl, :], twi_ref[sl, :]
        wr[sl, :] = vr * tr + vi * ti
        wi[sl, :] = vi * tr - vr * ti
        return carry

    _hy_slab_loop(stage2, k1)

    def stage1_group(n2):
        src = [pl.ds(pl.multiple_of(s * HY_N2 + n2, SUBLANES), SUBLANES) for s in range(k1p)]
        tiles = [wr[d, :] for d in src] + [wi[d, :] for d in src]
        x = _mm(f1inv_ref[...], jnp.concatenate(tiles, axis=0), HY_PREC_S1)
        for b in range(lb):
            dst = pl.ds(pl.multiple_of(b * HY_N2 + n2, SUBLANES), SUBLANES)
            o_ref[dst, :] = x[b * SUBLANES:(b + 1) * SUBLANES]

    def stage1(g, carry):
        for u in range(HY_S1_UNROLL):
            stage1_group((g * HY_S1_UNROLL + u) * SUBLANES)
        return carry

    lax.fori_loop(0, HY_N2 // (SUBLANES * HY_S1_UNROLL), stage1, 0)


def _hy_shortconv_body(x_ref, xp_ref, xn_ref, w_ref, b_ref, o_ref, *, groups):
    i = pl.program_id(0)
    x = x_ref[...]
    prev = jnp.where(_seq_edge(i, groups, HY_SC_TILE, last=False), 0.0, xp_ref[...])
    nxt = jnp.where(_seq_edge(i, groups, HY_SC_TILE, last=True), 0.0, xn_ref[...])
    row = lax.broadcasted_iota(jnp.int32, x.shape, 0)
    o_ref[...] = (w_ref[0:1, :] * _shift_rows(x, prev, 1, row) + w_ref[1:2, :] * x
                  + w_ref[2:3, :] * _shift_rows(x, nxt, -1, row) + b_ref[...])


def _hy_shortconv(proj, conv_w, conv_b, j, groups):
    t = proj.shape[0]
    width = 3 * D_D
    prev_spec, next_spec = _halo_specs(HY_SC_TILE, width, 0, t)
    return pl.pallas_call(
        functools.partial(_hy_shortconv_body, groups=groups),
        grid=(t // HY_SC_TILE,),
        in_specs=[pl.BlockSpec((HY_SC_TILE, width), lambda i: (i, 0)), prev_spec, next_spec,
                  _layer_spec((3, width))(j), _layer_spec((1, width))(j)],
        out_specs=pl.BlockSpec((HY_SC_TILE, width), lambda i: (i, 0)),
        out_shape=jax.ShapeDtypeStruct((t, width), f32),
        compiler_params=_cparams("parallel"),
        name="hy_shortconv",
    )(proj, proj, proj, conv_w, conv_b)


def _hy_mlp_body(z_ref, w1_ref, b1_ref, w2_ref, b2_ref, fr_ref, w3_ref, dl_ref, o_ref):
    z = z_ref[...]
    h = jnp.sin(fr_ref[0:1, :] * (_dot_hi(z, w1_ref[...]) + b1_ref[...]))
    h = jnp.sin(fr_ref[1:2, :] * (_dot_hi(h, w2_ref[...]) + b2_ref[...]))
    h = _dot_hi(h, w3_ref[...])
    o_ref[...] = h * jnp.exp(-z[:, 0:1] * dl_ref[...])


def _hy_mlp(seq_len, w1, b1, w2, b2, freq, w3):
    t = jnp.linspace(0.0, 1.0, seq_len, dtype=f32)[:, None]
    bands = (HY_POS_EMB - 1) // 2
    freqs = jnp.linspace(1e-4, bands - 1, bands, dtype=f32)[None, :]
    wpos = (2.0 * math.pi / seq_len) * jnp.arange(seq_len, dtype=f32)[:, None]
    z = jnp.concatenate([t, jnp.cos(freqs * wpos), -jnp.sin(freqs * wpos)], axis=-1)
    z = jnp.pad(z, ((0, 0), (0, LANES - HY_POS_EMB)))
    w1 = jnp.pad(w1, ((0, LANES - HY_POS_EMB), (0, 0)))
    max_decay = math.log(1e-2) / 0.3
    min_decay = math.log(1e-2) / 1.5
    deltas = jnp.abs(jnp.linspace(min_decay, max_decay, D_D, dtype=f32))
    deltas = jnp.tile(deltas, 4)[None, :]
    width = 4 * D_D
    hid = HY_FILTER_HIDDEN
    return pl.pallas_call(
        _hy_mlp_body,
        grid=(seq_len // HY_MLP_TILE,),
        in_specs=[pl.BlockSpec((HY_MLP_TILE, LANES), lambda i: (i, 0)),
                  _const_spec((LANES, hid)), _const_spec((1, hid)), _const_spec((hid, hid)),
                  _const_spec((1, hid)), _const_spec((2, hid)), _const_spec((hid, width)),
                  _const_spec((1, width))],
        out_specs=pl.BlockSpec((HY_MLP_TILE, width), lambda i: (i, 0)),
        out_shape=jax.ShapeDtypeStruct((seq_len, width), f32),
        compiler_params=_cparams("parallel"),
        name="hy_mlp",
    )(z, w1, b1[None, :], w2, b2[None, :], freq, w3, deltas)


def _hy_spectrum_body(hf_ref, hb_ref, f1_ref, f2_ref, twr_ref, twi_ref, or_ref, oi_ref,
                      zb, wr, wi, *, dims):
    k1 = dims[3]
    rows = k1 * HY_N2
    hf = hf_ref[...]
    hb = hb_ref[...]
    row = lax.broadcasted_iota(jnp.int32, hb.shape, 0)
    hb0 = jnp.where(row == 0, 0.0, hb)
    scale = 1.0 / (jnp.sum(jnp.abs(hf), axis=0, keepdims=True) + jnp.sum(jnp.abs(hb0), axis=0, keepdims=True))
    _hy_forward(hf_ref, f1_ref, f2_ref, twr_ref, twi_ref, wr, wi, dims)
    or_ref[...] = wr[:rows, :] * scale
    oi_ref[...] = wi[:rows, :] * scale
    zb[...] = hb0
    _hy_forward(zb, f1_ref, f2_ref, twr_ref, twi_ref, wr, wi, dims)
    or_ref[...] = or_ref[...] + wr[:rows, :] * scale
    oi_ref[...] = oi_ref[...] - wi[:rows, :] * scale


def _hy_spectrum(taps, tables, seq_len):
    dims = _hy_dims(seq_len)
    _, _, lb, k1, k1p = dims
    n_ct = D_D // HY_CW
    rows = k1 * HY_N2
    out = jax.ShapeDtypeStruct((2, rows, D_D), f32)
    return pl.pallas_call(
        functools.partial(_hy_spectrum_body, dims=dims),
        grid=(2, n_ct),
        in_specs=[pl.BlockSpec((seq_len, HY_CW), lambda o, c: (0, o * n_ct + c), pipeline_mode=pl.Buffered(1)),
                  pl.BlockSpec((seq_len, HY_CW), lambda o, c: (0, (2 + o) * n_ct + c),
                               pipeline_mode=pl.Buffered(1)),
                  _const_spec((2 * k1p * SUBLANES, lb * SUBLANES)), _const_spec((2 * HY_N2, HY_N2)),
                  _const_spec((rows, HY_CW)), _const_spec((rows, HY_CW))],
        out_specs=[pl.BlockSpec((None, rows, HY_CW), lambda o, c: (o, 0, c))] * 2,
        out_shape=[out, out],
        scratch_shapes=[pltpu.VMEM((seq_len, HY_CW), f32),
                        pltpu.VMEM((k1p * HY_N2, HY_CW), f32), pltpu.VMEM((k1p * HY_N2, HY_CW), f32)],
        compiler_params=_cparams("arbitrary", "arbitrary"),
        name="hy_spectrum",
    )(taps, taps, tables["f1"], tables["f2"], tables["tw_re"], tables["tw_im"])


def _hy_conv_body(z_ref, g_ref, hr_ref, hi_ref, bias_ref, f1_ref, f1inv_ref, f2_ref, twr_ref, twi_ref,
                  o_ref, wr, wi, *, dims):
    _hy_forward(z_ref, f1_ref, f2_ref, twr_ref, twi_ref, wr, wi, dims)
    _hy_inverse(hr_ref, hi_ref, f1inv_ref, f2_ref, twr_ref, twi_ref, wr, wi, o_ref, dims)
    bias = bias_ref[...]
    n_chunks = z_ref.shape[0] // HY_N2

    def gate(c, carry):
        sl = pl.ds(pl.multiple_of(c * HY_N2, HY_N2), HY_N2)
        o_ref[sl, :] = g_ref[sl, :] * (o_ref[sl, :] + bias * z_ref[sl, :])
        return carry

    lax.fori_loop(0, n_chunks, gate, 0)


def _hy_conv(z, z_off, z_col, gate, gate_off, gate_col, h_re, h_im, bias, tables, order, j, n_seq, seq_len):
    dims = _hy_dims(seq_len)
    _, _, lb, k1, k1p = dims
    n_ct = D_D // HY_CW
    rows = k1 * HY_N2
    z0, g0 = z_off // seq_len, gate_off // seq_len
    one = pl.Buffered(1)
    return pl.pallas_call(
        functools.partial(_hy_conv_body, dims=dims),
        grid=(n_ct, n_seq),
        in_specs=[
            pl.BlockSpec((seq_len, HY_CW), lambda c, b: (z0 + b, z_col * n_ct + c), pipeline_mode=one),
            pl.BlockSpec((seq_len, HY_CW), lambda c, b: (g0 + b, gate_col * n_ct + c), pipeline_mode=one),
            pl.BlockSpec((None, rows, HY_CW), lambda c, b: (order, 0, c), pipeline_mode=one),
            pl.BlockSpec((None, rows, HY_CW), lambda c, b: (order, 0, c), pipeline_mode=one),
            pl.BlockSpec((None, None, 1, HY_CW), lambda c, b: (j, order, 0, c)),
            _const_spec((2 * k1p * SUBLANES, lb * SUBLANES)), _const_spec((lb * SUBLANES, 2 * k1p * SUBLANES)), _const_spec((2 * HY_N2, HY_N2)),
            _const_spec((rows, HY_CW)), _const_spec((rows, HY_CW)),
        ],
        out_specs=pl.BlockSpec((seq_len, HY_CW), lambda c, b: (b, c)),
        out_shape=jax.ShapeDtypeStruct((n_seq * seq_len, D_D), f32),
        scratch_shapes=[pltpu.VMEM((k1p * HY_N2, HY_CW), f32), pltpu.VMEM((k1p * HY_N2, HY_CW), f32)],
        compiler_params=_cparams("arbitrary", "arbitrary"),
        name="hy_conv",
    )(z, gate, h_re, h_im, bias, tables["f1"], tables["f1inv"], tables["f2"], tables["tw_re"], tables["tw_im"])


def _hyena_mixer(proj, prm, j, groups):
    pc = _hy_shortconv(proj, prm["conv_w"], prm["conv_b"], j, groups)
    outs = []
    for tok_off, n_seq, seq_len in groups:
        tables = _hy_tables(seq_len)
        taps = _hy_mlp(seq_len, prm["f_w1"][j], prm["f_b1"][j], prm["f_w2"][j], prm["f_b2"][j],
                       prm["f_freq"][j], prm["f_w3"][j])
        h_re, h_im = _hy_spectrum(taps, tables, seq_len)
        z1 = _hy_conv(pc, tok_off, 0, pc, tok_off, 1, h_re, h_im, prm["bias"], tables, 0, j, n_seq, seq_len)
        y = _hy_conv(z1, 0, 0, pc, tok_off, 2, h_re, h_im, prm["bias"], tables, 1, j, n_seq, seq_len)
        outs.append(y)
    return jnp.concatenate(outs, axis=0)


def _outproj_odd_body(x_ref, hf_ref, hb_ref, gb_ref, yd_ref, w_ref, o_ref):
    y_c = (hf_ref[...] + hb_ref[...]) * jax.nn.gelu(gb_ref[...])
    acc = _dot(y_c.astype(bf16), w_ref[:D_C, :])
    acc = acc + _dot(yd_ref[...].astype(bf16), w_ref[D_C:, :])
    o_ref[...] = x_ref[...] + acc


def _outproj_odd(x, h_f, h_b, proj, y_d, w, j):
    t = x.shape[0]
    tok = lambda i: (i, 0)
    return pl.pallas_call(
        _outproj_odd_body,
        grid=(t // TOKEN_TILE,),
        in_specs=[
            pl.BlockSpec((TOKEN_TILE, D_MODEL), tok),
            pl.BlockSpec((TOKEN_TILE, D_C), tok),
            pl.BlockSpec((TOKEN_TILE, D_C), tok),
            pl.BlockSpec((TOKEN_TILE, D_C), lambda i: (i, CD_GB_BLK)),
            pl.BlockSpec((TOKEN_TILE, D_D), tok),
            _layer_spec((D_C + D_D, D_MODEL))(j),
        ],
        out_specs=pl.BlockSpec((TOKEN_TILE, D_MODEL), tok),
        out_shape=jax.ShapeDtypeStruct((t, D_MODEL), f32),
        compiler_params=_cparams("parallel"),
        name="outproj_odd",
    )(x, h_f, h_b, proj, y_d, w)


def _block_diag_heads(w):
    n, h, a, b = w.shape
    return jnp.einsum("nhij,hk->nhikj", w, jnp.eye(h, dtype=w.dtype)).reshape(n, h * a, h * b)


def _pad_rows(w, lo, total):
    return jnp.pad(w, ((0, 0), (lo, total - lo - w.shape[1]), (0, 0)))


def kernel(x_prompt, x_sample, ffn1_norm, ffn1_w_gate, ffn1_w_up, ffn1_w_down, mix_norm, ffn2_norm, ffn2_w_gate, ffn2_w_up, ffn2_w_down, ab_w_in, ab_w_out, s5_lambda_re, s5_lambda_im, s5_log_step, s5_b_re, s5_b_im, s5_c_re, s5_c_im, s5_d, s5_glu_w, s5_glu_b, rw_mu, rw_w0, rw_w_up, rw_a0, rw_a_up, rw_g_up, rw_k_k, rw_k_a, rw_r_k, rw_ln_w, rw_ln_b, cd_w_in, cd_w_out, lru_conv_w, lru_conv_b, lru_lambda, lru_wa, lru_ba, lru_wx, lru_bx, hy_conv_w, hy_conv_b, hy_f_w1, hy_f_b1, hy_f_w2, hy_f_b2, hy_f_freq, hy_f_w3, hy_bias, final_norm):
    n_p, l_p, _ = x_prompt.shape
    n_s, l_s, _ = x_sample.shape
    t_p = n_p * l_p
    groups = ((0, n_p, l_p), (t_p, n_s, l_s))
    x = jnp.concatenate([x_prompt.reshape(t_p, D_MODEL), x_sample.reshape(n_s * l_s, D_MODEL)], axis=0)

    row = lambda a: a[:, None, :]
    cast = lambda a: a.astype(bf16)
    ffn1 = (row(ffn1_norm), cast(ffn1_w_gate), cast(ffn1_w_up), cast(ffn1_w_down))
    ffn2 = (row(ffn2_norm), cast(ffn2_w_gate), cast(ffn2_w_up), cast(ffn2_w_down))
    mix_g = row(mix_norm)

    ab_in = cast(jnp.pad(ab_w_in, ((0, 0), (0, 0), (0, RW_P_WIDTH - ab_w_in.shape[-1]))))
    ab_out = cast(ab_w_out)
    s5 = dict(lam_re=s5_lambda_re, lam_im=s5_lambda_im, log_step=s5_log_step, b_re=s5_b_re, b_im=s5_b_im,
              c_re=s5_c_re, c_im=s5_c_im, d=row(s5_d), glu_w=cast(s5_glu_w), glu_b=row(s5_glu_b))
    lo = RW_DECAY_LORA
    lora_w = jnp.stack([_pad_rows(rw_w_up[:, 0], 0, RW_LORA_W), _pad_rows(rw_w_up[:, 1], lo, RW_LORA_W),
                        _pad_rows(rw_a_up, 2 * lo, RW_LORA_W),
                        _pad_rows(rw_g_up, 2 * lo + RW_A_LORA, RW_LORA_W)], axis=1)
    rw = dict(mu=row(jnp.pad(rw_mu, ((0, 0), (D_A, RW_P_WIDTH - D_A - rw_mu.shape[-1])))),
              lora_w=cast(lora_w), w0=rw_w0, a0=row(rw_a0), k_k=row(rw_k_k), k_a=row(rw_k_a),
              r_k=rw_r_k.reshape(-1, 1, D_B), ln_w=row(rw_ln_w), ln_b=row(rw_ln_b))

    cd_in = cast(jnp.concatenate([cd_w_in[..., 2 * D_C:], cd_w_in[..., :2 * D_C]], axis=-1))
    cd_out = cast(cd_w_out)
    w_gates = jnp.concatenate([_block_diag_heads(lru_wa[:, 0]), _block_diag_heads(lru_wx[:, 0]),
                               _block_diag_heads(lru_wa[:, 1]), _block_diag_heads(lru_wx[:, 1])], axis=-1)
    b_gates = jnp.concatenate([lru_ba[:, 0], lru_bx[:, 0], lru_ba[:, 1], lru_bx[:, 1]], axis=-1)
    lru = dict(conv_w=lru_conv_w, conv_b=row(lru_conv_b), lam=lru_lambda, w_gates=cast(w_gates),
               b_gates=row(b_gates))
    hy = dict(conv_w=hy_conv_w, conv_b=row(hy_conv_b), f_w1=hy_f_w1, f_b1=hy_f_b1, f_w2=hy_f_w2,
              f_b2=hy_f_b2, f_freq=hy_f_freq, f_w3=hy_f_w3, bias=hy_bias[:, :, None, :])

    for layer in range(DEPTH):
        j = layer // 2
        x = _ffn(x, *ffn1, layer)
        if layer % 2 == 0:
            proj = _proj(x, mix_g, ab_in, layer, j)
            y_a = _s5_mixer(proj, s5, j, groups)
            y_b = _rwkv7_mixer(proj, rw, j, groups)
            x = _outproj(x, y_a, y_b, ab_out, j)
        else:
            proj = _proj(x, mix_g, cd_in, layer, j)
            a_f, b_f, a_b, b_b = _lru_gates(proj, lru, j, groups)
            h_f, h_b = _lru_scan(a_f, b_f, a_b, b_b, groups)
            y_d = _hyena_mixer(proj, hy, j, groups)
            x = _outproj_odd(x, h_f, h_b, proj, y_d, cd_out, j)
        x = _ffn(x, *ffn2, layer)
    y = _final_norm(x, final_norm[None, :])
    return (y[:t_p].reshape(n_p, l_p, D_MODEL), y[t_p:].reshape(n_s, l_s, D_MODEL))
```

```python
import functools
import math

import numpy as np
import jax
import jax.numpy as jnp
from jax import lax
from jax.experimental import pallas as pl
from jax.experimental.pallas import tpu as pltpu

f32 = jnp.float32
bf16 = jnp.bfloat16

D_MODEL = 1024
DEPTH = 4
D_FF = 2816
RMS_EPS = 1e-6
D_A = 512
S5_GROUP = 16
S5_GROUPS = 32
S5_STATE = 64
D_B = 512
RW_HEAD = 64
RW_HEADS = 8
RW_DECAY_LORA = 32
RW_A_LORA = 32
RW_GATE_LORA = 64
RW_GN_EPS = 64e-5
D_C = 512
LRU_HEADS = 8
LRU_HEAD_DIM = 64
LRU_CONV = 4
LRU_C = 8.0
D_D = 512
HY_POS_EMB = 33
HY_FILTER_HIDDEN = 64

LANES = 128
SUBLANES = 8
VMEM_LIMIT = 56 * 1024 * 1024
TOKEN_TILE = 512
FF_CHUNK = 1408


def _cparams(*sem):
    return pltpu.CompilerParams(dimension_semantics=tuple(sem), vmem_limit_bytes=VMEM_LIMIT)


def _const_spec(shape):
    nd = len(shape)
    return pl.BlockSpec(shape, lambda *_: (0,) * nd, pipeline_mode=pl.Buffered(1))


def _layer_spec(shape):
    def make(layer):
        nd = len(shape)
        return pl.BlockSpec((None,) + tuple(shape), lambda *_: (layer,) + (0,) * nd,
                            pipeline_mode=pl.Buffered(1))
    return make


def _rms(x, g):
    ms = jnp.mean(x * x, axis=-1, keepdims=True)
    return x * lax.rsqrt(ms + RMS_EPS) * g


def _dot(a, b):
    return jnp.dot(a, b, preferred_element_type=f32)


_HI = lax.Precision.HIGHEST
_NN = (((1,), (0,)), ((), ()))
_NT = (((1,), (1,)), ((), ()))
_TN = (((0,), (0,)), ((), ()))


def _mm(a, b, prec, dims=_NN):
    dg = functools.partial(lax.dot_general, dimension_numbers=dims, preferred_element_type=f32)
    if prec == "hi":
        return dg(a, b, precision=_HI)
    a1, b1 = a.astype(bf16), b.astype(bf16)
    if prec == "b1":
        return dg(a1, b1)
    a2 = (a - a1.astype(f32)).astype(bf16)
    if prec == "x2":
        return dg(a1, b1) + dg(a2, b1)
    b2 = (b - b1.astype(f32)).astype(bf16)
    return dg(a1, b1) + (dg(a1, b2) + dg(a2, b1))


def _dot_hi(a, b):
    return _mm(a, b, "hi")


def _ffn_body(x_ref, g_ref, wg_ref, wu_ref, wd_ref, o_ref):
    x = x_ref[...]
    h = _rms(x, g_ref[...]).astype(bf16)
    acc = None
    for c in range(D_FF // FF_CHUNK):
        sl = slice(c * FF_CHUNK, (c + 1) * FF_CHUNK)
        gate = _dot(h, wg_ref[:, sl])
        up = _dot(h, wu_ref[:, sl])
        act = (gate * jax.nn.sigmoid(gate) * up).astype(bf16)
        part = _dot(act, wd_ref[sl, :])
        acc = part if acc is None else acc + part
    o_ref[...] = x + 0.5 * acc


def _ffn(x, norm, wg, wu, wd, layer):
    t = x.shape[0]
    return pl.pallas_call(
        _ffn_body,
        grid=(t // TOKEN_TILE,),
        in_specs=[
            pl.BlockSpec((TOKEN_TILE, D_MODEL), lambda i: (i, 0)),
            _layer_spec((1, D_MODEL))(layer),
            _layer_spec((D_MODEL, D_FF))(layer),
            _layer_spec((D_MODEL, D_FF))(layer),
            _layer_spec((D_FF, D_MODEL))(layer),
        ],
        out_specs=pl.BlockSpec((TOKEN_TILE, D_MODEL), lambda i: (i, 0)),
        out_shape=jax.ShapeDtypeStruct((t, D_MODEL), f32),
        compiler_params=_cparams("parallel"),
        name="ffn",
    )(x, norm, wg, wu, wd)


def _proj_body(x_ref, g_ref, w_ref, o_ref):
    h = _rms(x_ref[...], g_ref[...]).astype(bf16)
    o_ref[...] = _dot(h, w_ref[...])


def _proj(x, norm, w, layer, j):
    t = x.shape[0]
    p = w.shape[-1]
    return pl.pallas_call(
        _proj_body,
        grid=(t // TOKEN_TILE,),
        in_specs=[
            pl.BlockSpec((TOKEN_TILE, D_MODEL), lambda i: (i, 0)),
            _layer_spec((1, D_MODEL))(layer),
            _layer_spec((D_MODEL, p))(j),
        ],
        out_specs=pl.BlockSpec((TOKEN_TILE, p), lambda i: (i, 0)),
        out_shape=jax.ShapeDtypeStruct((t, p), f32),
        compiler_params=_cparams("parallel"),
        name="proj",
    )(x, norm, w)


def _outproj_body(x_ref, ya_ref, yb_ref, w_ref, o_ref):
    half = w_ref.shape[0] // 2
    acc = _dot(ya_ref[...].astype(bf16), w_ref[:half, :])
    acc = acc + _dot(yb_ref[...].astype(bf16), w_ref[half:, :])
    o_ref[...] = x_ref[...] + acc


def _outproj(x, ya, yb, w, j):
    t = x.shape[0]
    half = ya.shape[-1]
    return pl.pallas_call(
        _outproj_body,
        grid=(t // TOKEN_TILE,),
        in_specs=[
            pl.BlockSpec((TOKEN_TILE, D_MODEL), lambda i: (i, 0)),
            pl.BlockSpec((TOKEN_TILE, half), lambda i: (i, 0)),
            pl.BlockSpec((TOKEN_TILE, half), lambda i: (i, 0)),
            _layer_spec((2 * half, D_MODEL))(j),
        ],
        out_specs=pl.BlockSpec((TOKEN_TILE, D_MODEL), lambda i: (i, 0)),
        out_shape=jax.ShapeDtypeStruct((t, D_MODEL), f32),
        compiler_params=_cparams("parallel"),
        name="outproj",
    )(x, ya, yb, w)


def _final_norm_body(x_ref, g_ref, o_ref):
    o_ref[...] = _rms(x_ref[...], g_ref[...])


def _final_norm(x, g):
    t = x.shape[0]
    return pl.pallas_call(
        _final_norm_body,
        grid=(t // TOKEN_TILE,),
        in_specs=[pl.BlockSpec((TOKEN_TILE, D_MODEL), lambda i: (i, 0)), _const_spec((1, D_MODEL))],
        out_specs=pl.BlockSpec((TOKEN_TILE, D_MODEL), lambda i: (i, 0)),
        out_shape=jax.ShapeDtypeStruct((t, D_MODEL), f32),
        compiler_params=_cparams("parallel"),
        name="final_norm",
    )(x, g)


def _seq_edge(blk, groups, tile, last):
    hit = None
    for off, n_seq, seq_len in groups:
        start, per, n = off // tile, seq_len // tile, n_seq * (seq_len // tile)
        rel = blk - start
        edge = (per - 1) if last else 0
        h = (rel >= 0) & (rel < n) & (lax.rem(jnp.maximum(rel, 0), per) == edge)
        hit = h if hit is None else (hit | h)
    return hit


S5_BLOCK = 8
S5_ROW = S5_BLOCK * S5_GROUP
S5_OCT = 4
S5_OCT_IN = LANES * S5_BLOCK
S5_CW = 8 * S5_STATE
S5_ZW = 4 * S5_CW


def _s5_weight_body(lr_ref, li_ref, st_ref, btr_ref, bti_ref, cr_ref, ci_ref,
                    kt_ref, er_ref, ei_ref, wor_ref, woi_ref, aqr_ref, aqi_ref):
    backward = pl.program_id(0) >= S5_GROUPS
    lam_re = jnp.minimum(lr_ref[...], -1e-4)
    lam_im = li_ref[...]
    step = jnp.exp(st_ref[...])
    zr, zi = lam_re * step, lam_im * step
    mag = jnp.exp(zr)
    a_re, a_im = mag * jnp.cos(zi), mag * jnp.sin(zi)
    den = lam_re * lam_re + lam_im * lam_im
    num_re = a_re - 1.0
    coef_re = (num_re * lam_re + a_im * lam_im) / den
    coef_im = (a_im * lam_re - num_re * lam_im) / den
    bt_re, bt_im = btr_ref[...], bti_ref[...]
    bb_re = coef_re * bt_re - coef_im * bt_im
    bb_im = coef_re * bt_im + coef_im * bt_re

    def power(ell):
        m = jnp.exp(ell * zr)
        return m * jnp.cos(ell * zi), m * jnp.sin(ell * zi)

    tile = lambda a: jnp.concatenate([a] * S5_BLOCK, axis=0)
    lag = (lax.broadcasted_iota(jnp.int32, (S5_ROW, S5_STATE), 0) // S5_GROUP).astype(f32)
    p_re, p_im = power(lag)
    tb_re, tb_im = tile(bb_re), tile(bb_im)
    e_re = p_re * tb_re - p_im * tb_im
    e_im = p_re * tb_im + p_im * tb_re
    er_ref[...] = e_re
    ei_ref[...] = e_im
    c_re, c_im = cr_ref[...], ci_ref[...]
    kt_ref[...] = _mm(e_re, c_re, "hi", _NT) - _mm(e_im, c_im, "hi", _NT)
    lag_out = jnp.where(backward, S5_BLOCK - lag, lag + 1.0)
    q_re, q_im = power(lag_out)
    tc_re, tc_im = tile(c_re), tile(c_im)
    wor_ref[...] = tc_re * q_re - tc_im * q_im
    woi_ref[...] = -(tc_re * q_im + tc_im * q_re)
    jj = (lax.broadcasted_iota(jnp.int32, (SUBLANES, S5_STATE), 0) + 1).astype(f32) * S5_BLOCK
    aqr_ref[...], aqi_ref[...] = power(jj)


def _s5_weights(lam_re, lam_im, log_step, b_re, b_im, c_re, c_im):
    dg = 2 * S5_GROUPS
    vec = lambda a: a.reshape(dg, 1, S5_STATE)
    st = jnp.broadcast_to(log_step.reshape(dg, 1, 1), (dg, 1, S5_STATE))
    bt = lambda b: jnp.swapaxes(b, -1, -2).reshape(dg, S5_GROUP, S5_STATE)
    cc = lambda c: c.reshape(dg, S5_GROUP, S5_STATE)
    spec = lambda r, c: pl.BlockSpec((None, r, c), lambda i: (i, 0, 0))
    out = lambda r, c: jax.ShapeDtypeStruct((dg, r, c), f32)
    return pl.pallas_call(
        _s5_weight_body,
        grid=(dg,),
        in_specs=[spec(1, S5_STATE)] * 3 + [spec(S5_GROUP, S5_STATE)] * 4,
        out_specs=[spec(S5_ROW, S5_GROUP)] + [spec(S5_ROW, S5_STATE)] * 4 + [spec(SUBLANES, S5_STATE)] * 2,
        out_shape=[out(S5_ROW, S5_GROUP)] + [out(S5_ROW, S5_STATE)] * 4 + [out(SUBLANES, S5_STATE)] * 2,
        compiler_params=_cparams("parallel"),
        name="s5_weights",
    )(vec(lam_re), vec(lam_im), st, bt(b_re), bt(b_im), cc(c_re), cc(c_im))


def _s5_assemble(kt, e_re, e_im, wo_re, wo_im, aq_re, aq_im):
    g, t, c, n = S5_GROUPS, S5_BLOCK, S5_GROUP, S5_STATE
    eye = jnp.eye(8, dtype=f32)
    k = kt.reshape(2, g, t, c, c)
    lag = jnp.arange(t)[None, :] - jnp.arange(t)[:, None]
    fwd = jnp.where((lag >= 0)[None, :, :, None, None], k[0][:, jnp.clip(lag, 0, t - 1)], 0.0)
    bwd = jnp.where((lag <= 0)[None, :, :, None, None], k[1][:, jnp.clip(-lag, 0, t - 1)], 0.0)
    toep = (fwd + bwd).reshape(S5_OCT, 8, t, t, c, c)
    toep = jnp.einsum("qgstic,gh->qsgithc", toep, eye).reshape(S5_OCT, S5_OCT_IN, S5_OCT_IN)

    def spread(a):
        a = a.reshape(S5_OCT, 8, t, c, n)
        return jnp.einsum("qgsin,gh->qsgihn", a, eye).reshape(S5_OCT, S5_OCT_IN, S5_CW)

    e = lambda a: a.reshape(2, g, t, c, n)
    inj = jnp.concatenate([spread(e(e_re)[0][:, ::-1]), spread(e(e_im)[0][:, ::-1]),
                           spread(e(e_re)[1]), spread(e(e_im)[1])], axis=-1)
    out_t = jnp.concatenate([spread(e(wo_re)[0]), spread(e(wo_im)[0]),
                             spread(e(wo_re)[1]), spread(e(wo_im)[1])], axis=-1)
    tab = lambda x: x.reshape(2, S5_OCT, 8, SUBLANES, n).transpose(0, 1, 3, 2, 4).reshape(2, S5_OCT, SUBLANES, S5_CW)
    tab_re = jnp.concatenate([tab(aq_re)[0], tab(aq_re)[1]], axis=-1)
    tab_im = jnp.concatenate([tab(aq_im)[0], tab(aq_im)[1]], axis=-1)
    return toep.astype(bf16), inj.astype(bf16), out_t.astype(bf16), tab_re, tab_im


def _s5_conv_body(x_ref, m_ref, inj_ref, out_ref, tr_ref, ti_ref, o_ref, z_buf, h_buf, *, groups):
    n_rows = x_ref.shape[0] // S5_BLOCK
    n_tiles = n_rows // SUBLANES
    tile0 = pl.program_id(1) * n_tiles
    tile_tokens = SUBLANES * S5_BLOCK
    xb = jnp.concatenate([x_ref[pl.ds(s, n_rows, stride=S5_BLOCK), :].astype(bf16) for s in range(S5_BLOCK)],
                         axis=1)
    z_buf[...] = _dot(xb, inj_ref[...])
    row = lax.broadcasted_iota(jnp.int32, (SUBLANES, S5_CW), 0)
    shape = (SUBLANES, S5_CW)

    def tables(reverse):
        lanes = slice(S5_CW, 2 * S5_CW) if reverse else slice(0, S5_CW)
        full = lambda r: (jnp.broadcast_to(tr_ref[r:r + 1, lanes], shape),
                          jnp.broadcast_to(ti_ref[r:r + 1, lanes], shape))
        levels = []
        for d in (1, 2, 4):
            ar, ai = full(d - 1)
            keep = (row < SUBLANES - d) if reverse else (row >= d)
            levels.append((d, jnp.where(keep, ar, 0.0), jnp.where(keep, ai, 0.0)))
        cr, ci = jnp.zeros(shape, f32), jnp.zeros(shape, f32)
        for r in range(SUBLANES):
            ar, ai = full(r)
            sel = (row == (SUBLANES - 1 - r)) if reverse else (row == r)
            cr, ci = jnp.where(sel, ar, cr), jnp.where(sel, ai, ci)
        return levels, cr, ci

    tabs = (tables(False), tables(True))

    def scan_tile(t, carry, reverse):
        levels, cr, ci = tabs[int(reverse)]
        base = 2 * S5_CW * int(reverse)
        re_l, im_l = slice(base, base + S5_CW), slice(base + S5_CW, base + 2 * S5_CW)
        fresh = _seq_edge(tile0 + t, groups, tile_tokens, last=reverse)
        c_re, c_im = (jnp.where(fresh, 0.0, c) for c in carry)
        rows = pl.ds(pl.multiple_of(t * SUBLANES, SUBLANES), SUBLANES)
        h_re, h_im = z_buf[rows, re_l], z_buf[rows, im_l]
        for d, ar, ai in levels:
            sh = (SUBLANES - d) if reverse else d
            s_re, s_im = pltpu.roll(h_re, sh, 0), pltpu.roll(h_im, sh, 0)
            h_re, h_im = h_re + (ar * s_re - ai * s_im), h_im + (ar * s_im + ai * s_re)
        h_re, h_im = h_re + (cr * c_re - ci * c_im), h_im + (cr * c_im + ci * c_re)
        edge_in = SUBLANES - 1 if reverse else 0
        sh = (SUBLANES - 1) if reverse else 1
        h_buf[rows, re_l] = jnp.where(row == edge_in, c_re, pltpu.roll(h_re, sh, 0))
        h_buf[rows, im_l] = jnp.where(row == edge_in, c_im, pltpu.roll(h_im, sh, 0))
        edge_out = 0 if reverse else SUBLANES - 1
        return (jnp.broadcast_to(h_re[edge_out:edge_out + 1, :], shape),
                jnp.broadcast_to(h_im[edge_out:edge_out + 1, :], shape))

    def tile_pair(i, carry):
        return (scan_tile(i, carry[0], False), scan_tile(n_tiles - 1 - i, carry[1], True))

    zero = (jnp.zeros(shape, f32), jnp.zeros(shape, f32))
    lax.fori_loop(0, n_tiles, tile_pair, (zero, zero))

    y = _dot(xb, m_ref[...])
    y = y + lax.dot_general(h_buf[...].astype(bf16), out_ref[...], _NT, preferred_element_type=f32)
    for s in range(S5_BLOCK):
        o_ref[pl.ds(s, n_rows, stride=S5_BLOCK), :] = y[:, s * LANES:(s + 1) * LANES]


def _s5_conv(proj, toep, inj, out_t, tab_re, tab_im, groups):
    t = proj.shape[0]
    tokens = max(seq_len for _, _, seq_len in groups)
    rows = tokens // S5_BLOCK
    per = lambda r, c: pl.BlockSpec((None, r, c), lambda q, r_: (q, 0, 0), pipeline_mode=pl.Buffered(1))
    return pl.pallas_call(
        functools.partial(_s5_conv_body, groups=groups),
        grid=(S5_OCT, t // tokens),
        in_specs=[pl.BlockSpec((tokens, LANES), lambda q, r: (r, q)),
                  per(S5_OCT_IN, S5_OCT_IN), per(S5_OCT_IN, S5_ZW), per(S5_OCT_IN, S5_ZW),
                  per(SUBLANES, 2 * S5_CW), per(SUBLANES, 2 * S5_CW)],
        out_specs=pl.BlockSpec((tokens, LANES), lambda q, r: (r, q)),
        out_shape=jax.ShapeDtypeStruct((t, D_A), f32),
        scratch_shapes=[pltpu.VMEM((rows, S5_ZW), f32), pltpu.VMEM((rows, S5_ZW), f32)],
        compiler_params=_cparams("parallel", "parallel"),
        name="s5_conv",
    )(proj, toep, inj, out_t, tab_re, tab_im)


def _s5_post_body(u_ref, y_ref, d_ref, w_ref, b_ref, o_ref):
    y = y_ref[...] + d_ref[...] * u_ref[...]
    y = jax.nn.gelu(y)
    gate = _dot(y.astype(bf16), w_ref[...]) + b_ref[...]
    o_ref[...] = y * jax.nn.sigmoid(gate)


def _s5_post(proj, y, d, glu_w, glu_b, j):
    t = proj.shape[0]
    tok = lambda i: (i, 0)
    return pl.pallas_call(
        _s5_post_body,
        grid=(t // TOKEN_TILE,),
        in_specs=[
            pl.BlockSpec((TOKEN_TILE, D_A), tok),
            pl.BlockSpec((TOKEN_TILE, D_A), tok),
            _layer_spec((1, D_A))(j),
            _layer_spec((D_A, D_A))(j),
            _layer_spec((1, D_A))(j),
        ],
        out_specs=pl.BlockSpec((TOKEN_TILE, D_A), tok),
        out_shape=jax.ShapeDtypeStruct((t, D_A), f32),
        compiler_params=_cparams("parallel"),
        name="s5_post",
    )(proj, y, d, glu_w, glu_b)


def _s5_mixer(proj, prm, j, groups):
    w = _s5_weights(prm["lam_re"][j], prm["lam_im"][j], prm["log_step"][j], prm["b_re"][j], prm["b_im"][j],
                    prm["c_re"][j], prm["c_im"][j])
    y = _s5_conv(proj, *_s5_assemble(*w), groups)
    return _s5_post(proj, y, prm["d"], prm["glu_w"], prm["glu_b"], j)


RW_TILE = 256
RW_CHUNK = 64
RW_PAIRS = RW_HEADS // 2
RW_CHUNKS_PER_STEP = 2
RW_PREC_A = "b1"
RW_PREC_INV = "b1"
RW_PREC_APPLY = "b1"
RW_PREC_SEQ = "b1"
RW_PREC_SUM = "x2"
RW_P_WIDTH = 2304
RW_LORA_OFF = 2048
RW_LORA_W = 256


def _head_ones():
    a = lax.broadcasted_iota(jnp.int32, (D_B, D_B), 0) // RW_HEAD
    b = lax.broadcasted_iota(jnp.int32, (D_B, D_B), 1) // RW_HEAD
    return jnp.where(a == b, 1.0, 0.0).astype(f32)


def _rw_prep_body(x_ref, xp_ref, xn_ref, mu_ref, lw_ref, w0_ref, a0_ref, kk_ref, ka_ref,
                  r_out, k_out, v_out, kk_out, kka_out, lwf_out, lwb_out, g_out, *, groups):
    i = pl.program_id(0)
    x = x_ref[:, D_A:]
    prev = jnp.where(_seq_edge(i, groups, RW_TILE, last=False), 0.0, xp_ref[:, D_A:])
    nxt = jnp.where(_seq_edge(i, groups, RW_TILE, last=True), 0.0, xn_ref[:, D_A:])
    row = lax.broadcasted_iota(jnp.int32, x.shape, 0)
    shifted = 0.5 * (_shift_rows(x, prev, 1, row) + _shift_rows(x, nxt, -1, row))
    p = x + (shifted - x) * mu_ref[:, D_A:]
    r = p[:, :D_B]
    k = p[:, D_B:2 * D_B]
    v = p[:, 2 * D_B:3 * D_B]
    lora = p[:, RW_LORA_OFF - D_A:RW_LORA_OFF - D_A + RW_LORA_W]
    lora_t = jnp.tanh(lora).astype(bf16)
    a = jax.nn.sigmoid(a0_ref[...] + _dot(lora.astype(bf16), lw_ref[2]))
    g = _dot(jax.nn.sigmoid(lora).astype(bf16), lw_ref[3])
    kx = k * kk_ref[...]
    ss = _mm(kx * kx, _head_ones(), RW_PREC_SUM)
    kk = kx / jnp.maximum(jnp.sqrt(ss), 1e-12)
    r_out[...] = r
    k_out[...] = k * (1.0 + (a - 1.0) * ka_ref[...])
    v_out[...] = v
    kk_out[...] = kk
    kka_out[...] = kk * a
    g_out[...] = g
    for direction, out in enumerate((lwf_out, lwb_out)):
        wl = -jax.nn.softplus(-(w0_ref[direction:direction + 1, :] + _dot(lora_t, lw_ref[direction]))) - 0.5
        out[...] = -jnp.exp(wl)


def _rw_prep(proj, prm, j, groups):
    t = proj.shape[0]
    tok = lambda i: (i, 0)
    prev_spec, next_spec = _halo_specs(RW_TILE, RW_P_WIDTH, 0, t)
    out = jax.ShapeDtypeStruct((t, D_B), f32)
    return pl.pallas_call(
        functools.partial(_rw_prep_body, groups=groups),
        grid=(t // RW_TILE,),
        in_specs=[pl.BlockSpec((RW_TILE, RW_P_WIDTH), tok), prev_spec, next_spec,
                  _layer_spec((1, RW_P_WIDTH))(j), _layer_spec((4, RW_LORA_W, D_B))(j),
                  _layer_spec((2, D_B))(j), _layer_spec((1, D_B))(j), _layer_spec((1, D_B))(j),
                  _layer_spec((1, D_B))(j)],
        out_specs=[pl.BlockSpec((RW_TILE, D_B), tok)] * 8,
        out_shape=[out] * 8,
        compiler_params=_cparams("parallel"),
        name="rw_prep",
    )(proj, proj, proj, prm["mu"], prm["lora_w"], prm["w0"], prm["a0"], prm["k_k"], prm["k_a"])


def _rw_chunk_body(r_ref, k_ref, v_ref, kk_ref, kka_ref, lw_ref, y_out, st, *, reverse, groups, n_steps):
    c = RW_CHUNK
    pid = pl.program_id(0)

    @pl.when(pid == 0)
    def _():
        st[...] = jnp.zeros_like(st)

    ri = lax.broadcasted_iota(jnp.int32, (c, c), 0)
    ci = lax.broadcasted_iota(jnp.int32, (c, c), 1)
    tri = jnp.where((ci >= ri) if reverse else (ci <= ri), 1.0, 0.0).astype(f32)
    edge = 0 if reverse else c - 1
    big_r = lax.broadcasted_iota(jnp.int32, (LANES, LANES), 0)
    big_c = lax.broadcasted_iota(jnp.int32, (LANES, LANES), 1)
    same_head = (big_r // RW_HEAD) == (big_c // RW_HEAD)
    s_idx, j_idx = big_r % RW_HEAD, big_c % RW_HEAD
    strict = same_head & ((j_idx > s_idx) if reverse else (j_idx < s_idx))
    incl = same_head & ((j_idx >= s_idx) if reverse else (j_idx <= s_idx))
    eye = big_r == big_c
    eye_f = jnp.where(eye, 1.0, 0.0).astype(f32)
    head0 = lax.broadcasted_iota(jnp.int32, (c, LANES), 1) < RW_HEAD

    def expand(x):
        return jnp.concatenate([jnp.where(head0, x, 0.0), jnp.where(head0, 0.0, x)], axis=0)

    def collapse(x):
        return x[:c] + x[c:]

    n_sub = r_ref.shape[0] // c
    lanes = [slice(p * LANES, (p + 1) * LANES) for p in range(RW_PAIRS)]
    rows = [slice(ch * c, (ch + 1) * c) for ch in range(n_sub)]
    units = [(ch, p) for ch in range(n_sub) for p in range(RW_PAIRS)]
    pre = []
    for rs in rows:
        lw = lw_ref[rs, :]
        cum = _dot_hi(tri, lw)
        total = cum[edge:edge + 1, :]
        e_neg, tail = jnp.exp(-cum), jnp.exp(total - cum)
        k, kka = k_ref[rs, :], kka_ref[rs, :]
        pre.append(dict(at=-kk_ref[rs, :] * jnp.exp(cum - lw), rt=r_ref[rs, :] * jnp.exp(cum),
                        kh=k * e_neg, bh=kka * e_neg, kp=k * tail, bp=kka * tail, v=v_ref[rs, :],
                        pc=jnp.exp(total)))

    def get(name):
        return [pre[ch][name][:, lanes[p]] for ch, p in units]

    at, rt, kh, bh, kp, bp, v = (get(n) for n in ("at", "rt", "kh", "bh", "kp", "bp", "v"))
    at_exp = [expand(x) for x in at]
    v_exp = [expand(x) for x in v]
    z = [_mm(jnp.concatenate([ae, expand(rr)], axis=0), jnp.concatenate([a, a, b, b], axis=0), RW_PREC_A, _NT)
         for ae, rr, a, b in zip(at_exp, rt, kh, bh)]
    a_ak = [jnp.where(strict, zz[:LANES, :LANES], 0.0) for zz in z]
    a_ab = [jnp.where(strict, zz[:LANES, LANES:], 0.0) for zz in z]
    a_rk = [jnp.where(incl, zz[LANES:, :LANES], 0.0) for zz in z]
    a_rb = [jnp.where(incl, zz[LANES:, LANES:], 0.0) for zz in z]
    tinv = [eye_f + a for a in a_ab]
    pw = a_ab
    for _ in range(int(math.log2(c)) - 1):
        pw = [_mm(q, q, RW_PREC_INV) for q in pw]
        tinv = [_mm(t, eye_f + q, RW_PREC_INV) for t, q in zip(tinv, pw)]
    w = [_mm(a, ve, RW_PREC_APPLY) for a, ve in zip(a_ak, v_exp)]
    tu = [_mm(t, jnp.concatenate([ae, ww], axis=1), RW_PREC_APPLY) for t, ae, ww in zip(tinv, at_exp, w)]
    atp_exp = [x[:, :LANES] for x in tu]
    u0_exp = [x[:, LANES:] for x in tu]
    y0 = [collapse(_mm(ark, ve, RW_PREC_APPLY) + _mm(arb, ue, RW_PREC_APPLY))
          for ark, arb, ve, ue in zip(a_rk, a_rb, v_exp, u0_exp)]
    rp = [rr + collapse(_mm(arb, ae, RW_PREC_APPLY)) for rr, arb, ae in zip(rt, a_rb, atp_exp)]
    atp = [collapse(x) for x in atp_exp]
    u0 = [collapse(x) for x in u0_exp]
    g_full = [_mm(b, a, RW_PREC_APPLY, _TN) for b, a in zip(bp, atp)]
    h_full = [_mm(kd, vv, RW_PREC_APPLY, _TN) + _mm(b, u, RW_PREC_APPLY, _TN)
              for kd, vv, b, u in zip(kp, v, bp, u0)]
    blk = (n_steps - 1 - pid) if reverse else pid
    state = [st[:, sl] for sl in lanes]
    for ch in (range(n_sub - 1, -1, -1) if reverse else range(n_sub)):
        fresh = _seq_edge(blk * n_sub + ch, groups, c, last=reverse)
        for p in range(RW_PAIRS):
            i, sl = ch * RW_PAIRS + p, lanes[p]
            pc = jnp.broadcast_to(pre[ch]["pc"][:, sl], (LANES, LANES))
            g_bd = jnp.where(same_head, g_full[i], 0.0) + jnp.where(eye, pc, 0.0)
            h_bd = jnp.where(same_head, h_full[i], 0.0)
            s0 = jnp.where(fresh, 0.0, state[p])
            y_out[rows[ch], sl] = _mm(rp[i], s0, RW_PREC_SEQ) + y0[i]
            state[p] = _mm(g_bd, s0, RW_PREC_SEQ) + h_bd
    for p in range(RW_PAIRS):
        st[:, lanes[p]] = state[p]


def _rw_chunk(r, k, v, kk, kka, lw, groups, reverse):
    t = r.shape[0]
    step_rows = RW_CHUNK * RW_CHUNKS_PER_STEP
    n_steps = t // step_rows
    order = (lambda i: (n_steps - 1 - i, 0)) if reverse else (lambda i: (i, 0))
    return pl.pallas_call(
        functools.partial(_rw_chunk_body, reverse=reverse, groups=groups, n_steps=n_steps),
        grid=(n_steps,),
        in_specs=[pl.BlockSpec((step_rows, D_B), order)] * 6,
        out_specs=pl.BlockSpec((step_rows, D_B), order),
        out_shape=jax.ShapeDtypeStruct((t, D_B), f32),
        scratch_shapes=[pltpu.VMEM((LANES, D_B), f32)],
        compiler_params=_cparams("arbitrary"),
        name="rw_chunk_rev" if reverse else "rw_chunk_fwd",
    )(r, k, v, kk, kka, lw)


def _rw_post_body(yf_ref, yb_ref, r_ref, k_ref, v_ref, g_ref, rk_ref, lnw_ref, lnb_ref, o_ref):
    ones = _head_ones()
    inv = 1.0 / RW_HEAD
    y = yf_ref[...] + yb_ref[...]
    mean = _mm(y, ones, RW_PREC_SUM) * inv
    yc = y - mean
    var = _mm(yc * yc, ones, RW_PREC_SUM) * inv
    yn = yc * lax.rsqrt(var + RW_GN_EPS) * lnw_ref[...] + lnb_ref[...]
    bonus = _mm(r_ref[...] * k_ref[...] * rk_ref[...], ones, RW_PREC_SUM) * v_ref[...]
    o_ref[...] = (yn + bonus) * g_ref[...]


def _rw_post(y_f, y_b, r, k, v, g, prm, j):
    t = y_f.shape[0]
    tok = pl.BlockSpec((TOKEN_TILE, D_B), lambda i: (i, 0))
    vec = _layer_spec((1, D_B))(j)
    return pl.pallas_call(
        _rw_post_body,
        grid=(t // TOKEN_TILE,),
        in_specs=[tok] * 6 + [vec] * 3,
        out_specs=tok,
        out_shape=jax.ShapeDtypeStruct((t, D_B), f32),
        compiler_params=_cparams("parallel"),
        name="rw_post",
    )(y_f, y_b, r, k, v, g, prm["r_k"], prm["ln_w"], prm["ln_b"])


def _rwkv7_mixer(proj, prm, j, groups):
    r, k, v, kk, kka, lw_f, lw_b, g = _rw_prep(proj, prm, j, groups)
    y_f = _rw_chunk(r, k, v, kk, kka, lw_f, groups, reverse=False)
    y_b = _rw_chunk(r, k, v, kk, kka, lw_b, groups, reverse=True)
    return _rw_post(y_f, y_b, r, k, v, g, prm, j)


def _halo_specs(tile, width, col_blk, n_rows):
    per = tile // SUBLANES
    last8 = n_rows // SUBLANES - 1
    prev = pl.BlockSpec((SUBLANES, width), lambda i: (jnp.maximum(i * per - 1, 0), col_blk))
    nxt = pl.BlockSpec((SUBLANES, width), lambda i: (jnp.minimum((i + 1) * per, last8), col_blk))
    return prev, nxt


def _shift_rows(x, halo, k, row):
    n = x.shape[0]
    if k > 0:
        y = pltpu.roll(x, k, 0)
        for r in range(k):
            y = jnp.where(row == r, halo[SUBLANES - k + r:SUBLANES - k + r + 1, :], y)
        return y
    y = pltpu.roll(x, n - 1, 0)
    return jnp.where(row == n - 1, halo[0:1, :], y)


LRU_TILE = 512
CD_XB_BLK = 3
CD_GB_BLK = 4


def _lru_gate_body(x_ref, xp_ref, xn_ref, cw_ref, cb_ref, lam_ref, w_ref, b_ref,
                   af_ref, bf_ref, ab_ref, bb_ref, *, groups):
    i = pl.program_id(0)
    first = _seq_edge(i, groups, LRU_TILE, last=False)
    last = _seq_edge(i, groups, LRU_TILE, last=True)
    x = x_ref[...]
    prev = jnp.where(first, 0.0, xp_ref[...])
    nxt = jnp.where(last, 0.0, xn_ref[...])
    row = lax.broadcasted_iota(jnp.int32, x.shape, 0)
    xc = (cw_ref[0:1, :] * _shift_rows(x, prev, 2, row) + cw_ref[1:2, :] * _shift_rows(x, prev, 1, row)
          + cw_ref[2:3, :] * x + cw_ref[3:4, :] * _shift_rows(x, nxt, -1, row) + cb_ref[...])
    pre = _dot(xc.astype(bf16), w_ref[...]) + b_ref[...]
    outs = ((af_ref, bf_ref), (ab_ref, bb_ref))
    for direction in range(2):
        base = 2 * direction * D_C
        gate_r = jax.nn.sigmoid(pre[:, base:base + D_C])
        gate_i = jax.nn.sigmoid(pre[:, base + D_C:base + 2 * D_C])
        log_a = -LRU_C * gate_r * jax.nn.softplus(-lam_ref[direction:direction + 1, :])
        t = jnp.tanh(log_a)
        mult = jnp.sqrt(-2.0 * t / (1.0 - t))
        a_ref, b_ref_out = outs[direction]
        a_ref[...] = jnp.exp(log_a)
        b_ref_out[...] = mult * gate_i * xc


def _lru_gates(proj, prm, j, groups):
    t = proj.shape[0]
    tok = lambda i: (i, 0)
    prev_spec, next_spec = _halo_specs(LRU_TILE, D_C, CD_XB_BLK, t)
    out = jax.ShapeDtypeStruct((t, D_C), f32)
    return pl.pallas_call(
        functools.partial(_lru_gate_body, groups=groups),
        grid=(t // LRU_TILE,),
        in_specs=[
            pl.BlockSpec((LRU_TILE, D_C), lambda i: (i, CD_XB_BLK)),
            prev_spec,
            next_spec,
            _layer_spec((LRU_CONV, D_C))(j),
            _layer_spec((1, D_C))(j),
            _layer_spec((2, D_C))(j),
            _layer_spec((D_C, 4 * D_C))(j),
            _layer_spec((1, 4 * D_C))(j),
        ],
        out_specs=[pl.BlockSpec((LRU_TILE, D_C), tok)] * 4,
        out_shape=[out] * 4,
        compiler_params=_cparams("parallel"),
        name="lru_gates",
    )(proj, proj, proj, prm["conv_w"], prm["conv_b"], prm["lam"], prm["w_gates"], prm["b_gates"])


def _lru_scan_body(af_ref, bf_ref, ab_ref, bb_ref, hf_ref, hb_ref, carry_f, carry_b, *, groups, n_blk):
    i = pl.program_id(0)

    @pl.when(_seq_edge(i, groups, LRU_TILE, last=False))
    def _():
        carry_f[...] = jnp.zeros_like(carry_f)

    @pl.when(_seq_edge(n_blk - 1 - i, groups, LRU_TILE, last=True))
    def _():
        carry_b[...] = jnp.zeros_like(carry_b)

    row = lax.broadcasted_iota(jnp.int32, (SUBLANES, D_C), 0)
    n_tiles = LRU_TILE // SUBLANES

    def run(a_ref, b_ref, h_ref, carry, reverse):
        def tile_step(k, c):
            t = (n_tiles - 1 - k) if reverse else k
            r0 = pl.multiple_of(t * SUBLANES, SUBLANES)
            a = a_ref[pl.ds(r0, SUBLANES), :]
            b = b_ref[pl.ds(r0, SUBLANES), :]
            for d in (1, 2, 4):
                sh = (SUBLANES - d) if reverse else d
                keep = (row < SUBLANES - d) if reverse else (row >= d)
                a_s = jnp.where(keep, pltpu.roll(a, sh, 0), 1.0)
                b_s = jnp.where(keep, pltpu.roll(b, sh, 0), 0.0)
                b = b + a * b_s
                a = a * a_s
            h = b + a * c
            h_ref[pl.ds(r0, SUBLANES), :] = h
            edge = 0 if reverse else SUBLANES - 1
            return jnp.broadcast_to(h[edge:edge + 1, :], (SUBLANES, D_C))

        carry[...] = lax.fori_loop(0, n_tiles, tile_step, carry[...])

    run(af_ref, bf_ref, hf_ref, carry_f, False)
    run(ab_ref, bb_ref, hb_ref, carry_b, True)


def _lru_scan(a_f, b_f, a_b, b_b, groups):
    t = a_f.shape[0]
    n_blk = t // LRU_TILE
    fwd = lambda i: (i, 0)
    bwd = lambda i: (n_blk - 1 - i, 0)
    out = jax.ShapeDtypeStruct((t, D_C), f32)
    return pl.pallas_call(
        functools.partial(_lru_scan_body, groups=groups, n_blk=n_blk),
        grid=(n_blk,),
        in_specs=[pl.BlockSpec((LRU_TILE, D_C), fwd), pl.BlockSpec((LRU_TILE, D_C), fwd),
                  pl.BlockSpec((LRU_TILE, D_C), bwd), pl.BlockSpec((LRU_TILE, D_C), bwd)],
        out_specs=[pl.BlockSpec((LRU_TILE, D_C), fwd), pl.BlockSpec((LRU_TILE, D_C), bwd)],
        out_shape=[out, out],
        scratch_shapes=[pltpu.VMEM((SUBLANES, D_C), f32), pltpu.VMEM((SUBLANES, D_C), f32)],
        compiler_params=_cparams("arbitrary"),
        name="lru_scan",
    )(a_f, b_f, a_b, b_b)


HY_N2 = 256
HY_CW = 128
HY_SC_TILE = 512
HY_MLP_TILE = 512
HY_PREC_S1 = "b1"
HY_PREC_S2 = "b1"
HY_S1_UNROLL = 2


def _hy_dims(seq_len):
    n = 2 * seq_len
    n1 = n // HY_N2
    lb = n1 // 2
    k1 = n1 // 2 + 1
    k1p = -(-k1 // SUBLANES) * SUBLANES
    return n, n1, lb, k1, k1p


def _hy_tables(seq_len):
    n, n1, lb, k1, k1p = _hy_dims(seq_len)
    two_pi = 2.0 * math.pi
    kk = np.arange(k1, dtype=np.float64)[:, None]
    nn = np.arange(lb, dtype=np.float64)[None, :]
    ang1 = two_pi * kk * nn / n1
    f1 = np.zeros((2 * k1p, lb), np.float64)
    f1[:k1] = np.cos(ang1)
    f1[k1p:k1p + k1] = -np.sin(ang1)
    weight = np.where((np.arange(k1) == 0) | (np.arange(k1) == n1 // 2), 1.0, 2.0)[None, :] / n
    f1inv = np.zeros((lb, 2 * k1p), np.float64)
    f1inv[:, :k1] = weight * np.cos(ang1.T)
    f1inv[:, k1p:k1p + k1] = -weight * np.sin(ang1.T)
    idx = np.arange(HY_N2, dtype=np.float64)
    ang2 = two_pi * np.outer(idx, idx) / HY_N2
    f2 = np.concatenate([np.cos(ang2), -np.sin(ang2)], axis=0)
    prod = (jnp.arange(k1, dtype=jnp.int32)[:, None] * jnp.arange(HY_N2, dtype=jnp.int32)[None, :]) % n
    ang = prod.astype(f32) * f32(two_pi / n)
    bc = lambda a: jnp.broadcast_to(a.reshape(k1 * HY_N2, 1), (k1 * HY_N2, HY_CW))
    eye8 = np.eye(SUBLANES)
    return dict(f1=jnp.asarray(np.kron(f1, eye8), f32), f1inv=jnp.asarray(np.kron(f1inv, eye8), f32),
                f2=jnp.asarray(f2, f32), tw_re=bc(jnp.cos(ang)), tw_im=bc(-jnp.sin(ang)))


def _hy_slab(k):
    return pl.ds(k * HY_N2 if isinstance(k, int) else pl.multiple_of(k * HY_N2, HY_N2), HY_N2)


def _hy_slab_loop(slab, k1):
    def pair(i, carry):
        slab(2 * i, carry)
        slab(2 * i + 1, carry)
        return carry

    lax.fori_loop(0, k1 // 2, pair, 0)
    if k1 % 2:
        slab(k1 - 1, 0)


def _hy_forward(z_ref, f1_ref, f2_ref, twr_ref, twi_ref, wr, wi, dims):
    _, _, lb, k1, k1p = dims

    def stage1_group(n2):
        tiles = [z_ref[pl.ds(pl.multiple_of(b * HY_N2 + n2, SUBLANES), SUBLANES), :] for b in range(lb)]
        y = _mm(f1_ref[...], jnp.concatenate(tiles, axis=0), HY_PREC_S1)
        for s in range(k1p):
            dst = pl.ds(pl.multiple_of(s * HY_N2 + n2, SUBLANES), SUBLANES)
            wr[dst, :] = y[s * SUBLANES:(s + 1) * SUBLANES]
            wi[dst, :] = y[(k1p + s) * SUBLANES:(k1p + s + 1) * SUBLANES]

    def stage1(g, carry):
        for u in range(HY_S1_UNROLL):
            stage1_group((g * HY_S1_UNROLL + u) * SUBLANES)
        return carry

    lax.fori_loop(0, HY_N2 // (SUBLANES * HY_S1_UNROLL), stage1, 0)

    def stage2(k, carry):
        sl = _hy_slab(k)
        yr, yi, tr, ti = wr[sl, :], wi[sl, :], twr_ref[sl, :], twi_ref[sl, :]
        ar = yr * tr - yi * ti
        ai = yr * ti + yi * tr
        pq = _mm(f2_ref[...], jnp.concatenate([ar, ai], axis=1), HY_PREC_S2)
        p, q = pq[:, :HY_CW], pq[:, HY_CW:]
        wr[sl, :] = p[:HY_N2] - q[HY_N2:]
        wi[sl, :] = p[HY_N2:] + q[:HY_N2]
        return carry

    _hy_slab_loop(stage2, k1)


def _hy_inverse(hr_ref, hi_ref, f1inv_ref, f2_ref, twr_ref, twi_ref, wr, wi, o_ref, dims):
    _, _, lb, k1, k1p = dims

    def stage2(k, carry):
        sl = _hy_slab(k)
        xr, xi, hr, hi = wr[sl, :], wi[sl, :], hr_ref[sl, :], hi_ref[sl, :]
        zr = xr * hr - xi * hi
        zi = xr * hi + xi * hr
        pq = _mm(f2_ref[...], jnp.concatenate([zr, zi], axis=1), HY_PREC_S2)
        p, q = pq[:, :HY_CW], pq[:, HY_CW:]
        vr = p[:HY_N2] + q[HY_N2:]
        vi = q[:HY_N2] - p[HY_N2:]
        tr, ti = twr_ref[sl, :], twi_ref[sl, :]
        wr[sl, :] = vr * tr + vi * ti
        wi[sl, :] = vi * tr - vr * ti
        return carry

    _hy_slab_loop(stage2, k1)

    def stage1_group(n2):
        src = [pl.ds(pl.multiple_of(s * HY_N2 + n2, SUBLANES), SUBLANES) for s in range(k1p)]
        tiles = [wr[d, :] for d in src] + [wi[d, :] for d in src]
        x = _mm(f1inv_ref[...], jnp.concatenate(tiles, axis=0), HY_PREC_S1)
        for b in range(lb):
            dst = pl.ds(pl.multiple_of(b * HY_N2 + n2, SUBLANES), SUBLANES)
            o_ref[dst, :] = x[b * SUBLANES:(b + 1) * SUBLANES]

    def stage1(g, carry):
        for u in range(HY_S1_UNROLL):
            stage1_group((g * HY_S1_UNROLL + u) * SUBLANES)
        return carry

    lax.fori_loop(0, HY_N2 // (SUBLANES * HY_S1_UNROLL), stage1, 0)


def _hy_shortconv_body(x_ref, xp_ref, xn_ref, w_ref, b_ref, o_ref, *, groups):
    i = pl.program_id(0)
    x = x_ref[...]
    prev = jnp.where(_seq_edge(i, groups, HY_SC_TILE, last=False), 0.0, xp_ref[...])
    nxt = jnp.where(_seq_edge(i, groups, HY_SC_TILE, last=True), 0.0, xn_ref[...])
    row = lax.broadcasted_iota(jnp.int32, x.shape, 0)
    o_ref[...] = (w_ref[0:1, :] * _shift_rows(x, prev, 1, row) + w_ref[1:2, :] * x
                  + w_ref[2:3, :] * _shift_rows(x, nxt, -1, row) + b_ref[...])


def _hy_shortconv(proj, conv_w, conv_b, j, groups):
    t = proj.shape[0]
    width = 3 * D_D
    prev_spec, next_spec = _halo_specs(HY_SC_TILE, width, 0, t)
    return pl.pallas_call(
        functools.partial(_hy_shortconv_body, groups=groups),
        grid=(t // HY_SC_TILE,),
        in_specs=[pl.BlockSpec((HY_SC_TILE, width), lambda i: (i, 0)), prev_spec, next_spec,
                  _layer_spec((3, width))(j), _layer_spec((1, width))(j)],
        out_specs=pl.BlockSpec((HY_SC_TILE, width), lambda i: (i, 0)),
        out_shape=jax.ShapeDtypeStruct((t, width), f32),
        compiler_params=_cparams("parallel"),
        name="hy_shortconv",
    )(proj, proj, proj, conv_w, conv_b)


def _hy_mlp_body(z_ref, w1_ref, b1_ref, w2_ref, b2_ref, fr_ref, w3_ref, dl_ref, o_ref):
    z = z_ref[...]
    h = jnp.sin(fr_ref[0:1, :] * (_dot_hi(z, w1_ref[...]) + b1_ref[...]))
    h = jnp.sin(fr_ref[1:2, :] * (_dot_hi(h, w2_ref[...]) + b2_ref[...]))
    h = _dot_hi(h, w3_ref[...])
    o_ref[...] = h * jnp.exp(-z[:, 0:1] * dl_ref[...])


def _hy_mlp(seq_len, w1, b1, w2, b2, freq, w3):
    t = jnp.linspace(0.0, 1.0, seq_len, dtype=f32)[:, None]
    bands = (HY_POS_EMB - 1) // 2
    freqs = jnp.linspace(1e-4, bands - 1, bands, dtype=f32)[None, :]
    wpos = (2.0 * math.pi / seq_len) * jnp.arange(seq_len, dtype=f32)[:, None]
    z = jnp.concatenate([t, jnp.cos(freqs * wpos), -jnp.sin(freqs * wpos)], axis=-1)
    z = jnp.pad(z, ((0, 0), (0, LANES - HY_POS_EMB)))
    w1 = jnp.pad(w1, ((0, LANES - HY_POS_EMB), (0, 0)))
    max_decay = math.log(1e-2) / 0.3
    min_decay = math.log(1e-2) / 1.5
    deltas = jnp.abs(jnp.linspace(min_decay, max_decay, D_D, dtype=f32))
    deltas = jnp.tile(deltas, 4)[None, :]
    width = 4 * D_D
    hid = HY_FILTER_HIDDEN
    return pl.pallas_call(
        _hy_mlp_body,
        grid=(seq_len // HY_MLP_TILE,),
        in_specs=[pl.BlockSpec((HY_MLP_TILE, LANES), lambda i: (i, 0)),
                  _const_spec((LANES, hid)), _const_spec((1, hid)), _const_spec((hid, hid)),
                  _const_spec((1, hid)), _const_spec((2, hid)), _const_spec((hid, width)),
                  _const_spec((1, width))],
        out_specs=pl.BlockSpec((HY_MLP_TILE, width), lambda i: (i, 0)),
        out_shape=jax.ShapeDtypeStruct((seq_len, width), f32),
        compiler_params=_cparams("parallel"),
        name="hy_mlp",
    )(z, w1, b1[None, :], w2, b2[None, :], freq, w3, deltas)


def _hy_spectrum_body(hf_ref, hb_ref, f1_ref, f2_ref, twr_ref, twi_ref, or_ref, oi_ref,
                      zb, wr, wi, *, dims):
    k1 = dims[3]
    rows = k1 * HY_N2
    hf = hf_ref[...]
    hb = hb_ref[...]
    row = lax.broadcasted_iota(jnp.int32, hb.shape, 0)
    hb0 = jnp.where(row == 0, 0.0, hb)
    scale = 1.0 / (jnp.sum(jnp.abs(hf), axis=0, keepdims=True) + jnp.sum(jnp.abs(hb0), axis=0, keepdims=True))
    _hy_forward(hf_ref, f1_ref, f2_ref, twr_ref, twi_ref, wr, wi, dims)
    or_ref[...] = wr[:rows, :] * scale
    oi_ref[...] = wi[:rows, :] * scale
    zb[...] = hb0
    _hy_forward(zb, f1_ref, f2_ref, twr_ref, twi_ref, wr, wi, dims)
    or_ref[...] = or_ref[...] + wr[:rows, :] * scale
    oi_ref[...] = oi_ref[...] - wi[:rows, :] * scale


def _hy_spectrum(taps, tables, seq_len):
    dims = _hy_dims(seq_len)
    _, _, lb, k1, k1p = dims
    n_ct = D_D // HY_CW
    rows = k1 * HY_N2
    out = jax.ShapeDtypeStruct((2, rows, D_D), f32)
    return pl.pallas_call(
        functools.partial(_hy_spectrum_body, dims=dims),
        grid=(2, n_ct),
        in_specs=[pl.BlockSpec((seq_len, HY_CW), lambda o, c: (0, o * n_ct + c), pipeline_mode=pl.Buffered(1)),
                  pl.BlockSpec((seq_len, HY_CW), lambda o, c: (0, (2 + o) * n_ct + c),
                               pipeline_mode=pl.Buffered(1)),
                  _const_spec((2 * k1p * SUBLANES, lb * SUBLANES)), _const_spec((2 * HY_N2, HY_N2)),
                  _const_spec((rows, HY_CW)), _const_spec((rows, HY_CW))],
        out_specs=[pl.BlockSpec((None, rows, HY_CW), lambda o, c: (o, 0, c))] * 2,
        out_shape=[out, out],
        scratch_shapes=[pltpu.VMEM((seq_len, HY_CW), f32),
                        pltpu.VMEM((k1p * HY_N2, HY_CW), f32), pltpu.VMEM((k1p * HY_N2, HY_CW), f32)],
        compiler_params=_cparams("arbitrary", "arbitrary"),
        name="hy_spectrum",
    )(taps, taps, tables["f1"], tables["f2"], tables["tw_re"], tables["tw_im"])


def _hy_conv_body(z_ref, g_ref, hr_ref, hi_ref, bias_ref, f1_ref, f1inv_ref, f2_ref, twr_ref, twi_ref,
                  o_ref, wr, wi, *, dims):
    _hy_forward(z_ref, f1_ref, f2_ref, twr_ref, twi_ref, wr, wi, dims)
    _hy_inverse(hr_ref, hi_ref, f1inv_ref, f2_ref, twr_ref, twi_ref, wr, wi, o_ref, dims)
    bias = bias_ref[...]
    n_chunks = z_ref.shape[0] // HY_N2

    def gate(c, carry):
        sl = pl.ds(pl.multiple_of(c * HY_N2, HY_N2), HY_N2)
        o_ref[sl, :] = g_ref[sl, :] * (o_ref[sl, :] + bias * z_ref[sl, :])
        return carry

    lax.fori_loop(0, n_chunks, gate, 0)


def _hy_conv(z, z_off, z_col, gate, gate_off, gate_col, h_re, h_im, bias, tables, order, j, n_seq, seq_len):
    dims = _hy_dims(seq_len)
    _, _, lb, k1, k1p = dims
    n_ct = D_D // HY_CW
    rows = k1 * HY_N2
    z0, g0 = z_off // seq_len, gate_off // seq_len
    one = pl.Buffered(1)
    return pl.pallas_call(
        functools.partial(_hy_conv_body, dims=dims),
        grid=(n_ct, n_seq),
        in_specs=[
            pl.BlockSpec((seq_len, HY_CW), lambda c, b: (z0 + b, z_col * n_ct + c), pipeline_mode=one),
            pl.BlockSpec((seq_len, HY_CW), lambda c, b: (g0 + b, gate_col * n_ct + c), pipeline_mode=one),
            pl.BlockSpec((None, rows, HY_CW), lambda c, b: (order, 0, c), pipeline_mode=one),
            pl.BlockSpec((None, rows, HY_CW), lambda c, b: (order, 0, c), pipeline_mode=one),
            pl.BlockSpec((None, None, 1, HY_CW), lambda c, b: (j, order, 0, c)),
            _const_spec((2 * k1p * SUBLANES, lb * SUBLANES)), _const_spec((lb * SUBLANES, 2 * k1p * SUBLANES)), _const_spec((2 * HY_N2, HY_N2)),
            _const_spec((rows, HY_CW)), _const_spec((rows, HY_CW)),
        ],
        out_specs=pl.BlockSpec((seq_len, HY_CW), lambda c, b: (b, c)),
        out_shape=jax.ShapeDtypeStruct((n_seq * seq_len, D_D), f32),
        scratch_shapes=[pltpu.VMEM((k1p * HY_N2, HY_CW), f32), pltpu.VMEM((k1p * HY_N2, HY_CW), f32)],
        compiler_params=_cparams("arbitrary", "arbitrary"),
        name="hy_conv",
    )(z, gate, h_re, h_im, bias, tables["f1"], tables["f1inv"], tables["f2"], tables["tw_re"], tables["tw_im"])


def _hyena_mixer(proj, prm, j, groups):
    pc = _hy_shortconv(proj, prm["conv_w"], prm["conv_b"], j, groups)
    outs = []
    for tok_off, n_seq, seq_len in groups:
        tables = _hy_tables(seq_len)
        taps = _hy_mlp(seq_len, prm["f_w1"][j], prm["f_b1"][j], prm["f_w2"][j], prm["f_b2"][j],
                       prm["f_freq"][j], prm["f_w3"][j])
        h_re, h_im = _hy_spectrum(taps, tables, seq_len)
        z1 = _hy_conv(pc, tok_off, 0, pc, tok_off, 1, h_re, h_im, prm["bias"], tables, 0, j, n_seq, seq_len)
        y = _hy_conv(z1, 0, 0, pc, tok_off, 2, h_re, h_im, prm["bias"], tables, 1, j, n_seq, seq_len)
        outs.append(y)
    return jnp.concatenate(outs, axis=0)


def _outproj_odd_body(x_ref, hf_ref, hb_ref, gb_ref, yd_ref, w_ref, o_ref):
    y_c = (hf_ref[...] + hb_ref[...]) * jax.nn.gelu(gb_ref[...])
    acc = _dot(y_c.astype(bf16), w_ref[:D_C, :])
    acc = acc + _dot(yd_ref[...].astype(bf16), w_ref[D_C:, :])
    o_ref[...] = x_ref[...] + acc


def _outproj_odd(x, h_f, h_b, proj, y_d, w, j):
    t = x.shape[0]
    tok = lambda i: (i, 0)
    return pl.pallas_call(
        _outproj_odd_body,
        grid=(t // TOKEN_TILE,),
        in_specs=[
            pl.BlockSpec((TOKEN_TILE, D_MODEL), tok),
            pl.BlockSpec((TOKEN_TILE, D_C), tok),
            pl.BlockSpec((TOKEN_TILE, D_C), tok),
            pl.BlockSpec((TOKEN_TILE, D_C), lambda i: (i, CD_GB_BLK)),
            pl.BlockSpec((TOKEN_TILE, D_D), tok),
            _layer_spec((D_C + D_D, D_MODEL))(j),
        ],
        out_specs=pl.BlockSpec((TOKEN_TILE, D_MODEL), tok),
        out_shape=jax.ShapeDtypeStruct((t, D_MODEL), f32),
        compiler_params=_cparams("parallel"),
        name="outproj_odd",
    )(x, h_f, h_b, proj, y_d, w)


def _block_diag_heads(w):
    n, h, a, b = w.shape
    return jnp.einsum("nhij,hk->nhikj", w, jnp.eye(h, dtype=w.dtype)).reshape(n, h * a, h * b)


def _pad_rows(w, lo, total):
    return jnp.pad(w, ((0, 0), (lo, total - lo - w.shape[1]), (0, 0)))


def kernel(x_prompt, x_sample, ffn1_norm, ffn1_w_gate, ffn1_w_up, ffn1_w_down, mix_norm, ffn2_norm, ffn2_w_gate, ffn2_w_up, ffn2_w_down, ab_w_in, ab_w_out, s5_lambda_re, s5_lambda_im, s5_log_step, s5_b_re, s5_b_im, s5_c_re, s5_c_im, s5_d, s5_glu_w, s5_glu_b, rw_mu, rw_w0, rw_w_up, rw_a0, rw_a_up, rw_g_up, rw_k_k, rw_k_a, rw_r_k, rw_ln_w, rw_ln_b, cd_w_in, cd_w_out, lru_conv_w, lru_conv_b, lru_lambda, lru_wa, lru_ba, lru_wx, lru_bx, hy_conv_w, hy_conv_b, hy_f_w1, hy_f_b1, hy_f_w2, hy_f_b2, hy_f_freq, hy_f_w3, hy_bias, final_norm):
    n_p, l_p, _ = x_prompt.shape
    n_s, l_s, _ = x_sample.shape
    t_p = n_p * l_p
    groups = ((0, n_p, l_p), (t_p, n_s, l_s))
    x = jnp.concatenate([x_prompt.reshape(t_p, D_MODEL), x_sample.reshape(n_s * l_s, D_MODEL)], axis=0)

    row = lambda a: a[:, None, :]
    cast = lambda a: a.astype(bf16)
    ffn1 = (row(ffn1_norm), cast(ffn1_w_gate), cast(ffn1_w_up), cast(ffn1_w_down))
    ffn2 = (row(ffn2_norm), cast(ffn2_w_gate), cast(ffn2_w_up), cast(ffn2_w_down))
    mix_g = row(mix_norm)

    ab_in = cast(jnp.pad(ab_w_in, ((0, 0), (0, 0), (0, RW_P_WIDTH - ab_w_in.shape[-1]))))
    ab_out = cast(ab_w_out)
    s5 = dict(lam_re=s5_lambda_re, lam_im=s5_lambda_im, log_step=s5_log_step, b_re=s5_b_re, b_im=s5_b_im,
              c_re=s5_c_re, c_im=s5_c_im, d=row(s5_d), glu_w=cast(s5_glu_w), glu_b=row(s5_glu_b))
    lo = RW_DECAY_LORA
    lora_w = jnp.stack([_pad_rows(rw_w_up[:, 0], 0, RW_LORA_W), _pad_rows(rw_w_up[:, 1], lo, RW_LORA_W),
                        _pad_rows(rw_a_up, 2 * lo, RW_LORA_W),
                        _pad_rows(rw_g_up, 2 * lo + RW_A_LORA, RW_LORA_W)], axis=1)
    rw = dict(mu=row(jnp.pad(rw_mu, ((0, 0), (D_A, RW_P_WIDTH - D_A - rw_mu.shape[-1])))),
              lora_w=cast(lora_w), w0=rw_w0, a0=row(rw_a0), k_k=row(rw_k_k), k_a=row(rw_k_a),
              r_k=rw_r_k.reshape(-1, 1, D_B), ln_w=row(rw_ln_w), ln_b=row(rw_ln_b))

    cd_in = cast(jnp.concatenate([cd_w_in[..., 2 * D_C:], cd_w_in[..., :2 * D_C]], axis=-1))
    cd_out = cast(cd_w_out)
    w_gates = jnp.concatenate([_block_diag_heads(lru_wa[:, 0]), _block_diag_heads(lru_wx[:, 0]),
                               _block_diag_heads(lru_wa[:, 1]), _block_diag_heads(lru_wx[:, 1])], axis=-1)
    b_gates = jnp.concatenate([lru_ba[:, 0], lru_bx[:, 0], lru_ba[:, 1], lru_bx[:, 1]], axis=-1)
    lru = dict(conv_w=lru_conv_w, conv_b=row(lru_conv_b), lam=lru_lambda, w_gates=cast(w_gates),
               b_gates=row(b_gates))
    hy = dict(conv_w=hy_conv_w, conv_b=row(hy_conv_b), f_w1=hy_f_w1, f_b1=hy_f_b1, f_w2=hy_f_w2,
              f_b2=hy_f_b2, f_freq=hy_f_freq, f_w3=hy_f_w3, bias=hy_bias[:, :, None, :])

    for layer in range(DEPTH):
        j = layer // 2
        x = _ffn(x, *ffn1, layer)
        if layer % 2 == 0:
            proj = _proj(x, mix_g, ab_in, layer, j)
            y_a = _s5_mixer(proj, s5, j, groups)
            y_b = _rwkv7_mixer(proj, rw, j, groups)
            x = _outproj(x, y_a, y_b, ab_out, j)
        else:
            proj = _proj(x, mix_g, cd_in, layer, j)
            a_f, b_f, a_b, b_b = _lru_gates(proj, lru, j, groups)
            h_f, h_b = _lru_scan(a_f, b_f, a_b, b_b, groups)
            y_d = _hyena_mixer(proj, hy, j, groups)
            x = _outproj_odd(x, h_f, h_b, proj, y_d, cd_out, j)
        x = _ffn(x, *ffn2, layer)
    y = _final_norm(x, final_norm[None, :])
    return (y[:t_p].reshape(n_p, l_p, D_MODEL), y[t_p:].reshape(n_s, l_s, D_MODEL))
```

```python
import functools
import math

import numpy as np
import jax
import jax.numpy as jnp
from jax import lax
from jax.experimental import pallas as pl
from jax.experimental.pallas import tpu as pltpu

f32 = jnp.float32
bf16 = jnp.bfloat16

D_MODEL = 1024
DEPTH = 4
D_FF = 2816
RMS_EPS = 1e-6
D_A = 512
S5_GROUP = 16
S5_GROUPS = 32
S5_STATE = 64
D_B = 512
RW_HEAD = 64
RW_HEADS = 8
RW_DECAY_LORA = 32
RW_A_LORA = 32
RW_GATE_LORA = 64
RW_GN_EPS = 64e-5
D_C = 512
LRU_HEADS = 8
LRU_HEAD_DIM = 64
LRU_CONV = 4
LRU_C = 8.0
D_D = 512
HY_POS_EMB = 33
HY_FILTER_HIDDEN = 64

LANES = 128
SUBLANES = 8
VMEM_LIMIT = 56 * 1024 * 1024
TOKEN_TILE = 512
FFN_TILE = 1024
FF_CHUNK = 1408


def _cparams(*sem):
    return pltpu.CompilerParams(dimension_semantics=tuple(sem), vmem_limit_bytes=VMEM_LIMIT)


def _const_spec(shape):
    nd = len(shape)
    return pl.BlockSpec(shape, lambda *_: (0,) * nd, pipeline_mode=pl.Buffered(1))


def _layer_spec(shape):
    def make(layer):
        nd = len(shape)
        return pl.BlockSpec((None,) + tuple(shape), lambda *_: (layer,) + (0,) * nd,
                            pipeline_mode=pl.Buffered(1))
    return make


def _rms(x, g):
    ms = jnp.mean(x * x, axis=-1, keepdims=True)
    return x * lax.rsqrt(ms + RMS_EPS) * g


def _dot(a, b):
    return jnp.dot(a, b, preferred_element_type=f32)


_HI = lax.Precision.HIGHEST
_NN = (((1,), (0,)), ((), ()))
_NT = (((1,), (1,)), ((), ()))
_TN = (((0,), (0,)), ((), ()))


def _mm(a, b, prec, dims=_NN):
    dg = functools.partial(lax.dot_general, dimension_numbers=dims, preferred_element_type=f32)
    if prec == "hi":
        return dg(a, b, precision=_HI)
    a1, b1 = a.astype(bf16), b.astype(bf16)
    if prec == "b1":
        return dg(a1, b1)
    a2 = (a - a1.astype(f32)).astype(bf16)
    if prec == "x2":
        return dg(a1, b1) + dg(a2, b1)
    b2 = (b - b1.astype(f32)).astype(bf16)
    return dg(a1, b1) + (dg(a1, b2) + dg(a2, b1))


def _dot_hi(a, b):
    return _mm(a, b, "hi")


def _ffn_body(x_ref, g_ref, wg_ref, wu_ref, wd_ref, o_ref):
    x = x_ref[...]
    h = _rms(x, g_ref[...]).astype(bf16)
    acc = None
    for c in range(D_FF // FF_CHUNK):
        sl = slice(c * FF_CHUNK, (c + 1) * FF_CHUNK)
        gate = _dot(h, wg_ref[:, sl])
        up = _dot(h, wu_ref[:, sl])
        act = (gate * jax.nn.sigmoid(gate) * up).astype(bf16)
        part = _dot(act, wd_ref[sl, :])
        acc = part if acc is None else acc + part
    o_ref[...] = x + 0.5 * acc


def _ffn(x, norm, wg, wu, wd, layer):
    t = x.shape[0]
    return pl.pallas_call(
        _ffn_body,
        grid=(t // FFN_TILE,),
        in_specs=[
            pl.BlockSpec((FFN_TILE, D_MODEL), lambda i: (i, 0)),
            _layer_spec((1, D_MODEL))(layer),
            _layer_spec((D_MODEL, D_FF))(layer),
            _layer_spec((D_MODEL, D_FF))(layer),
            _layer_spec((D_FF, D_MODEL))(layer),
        ],
        out_specs=pl.BlockSpec((FFN_TILE, D_MODEL), lambda i: (i, 0)),
        out_shape=jax.ShapeDtypeStruct((t, D_MODEL), f32),
        compiler_params=_cparams("parallel"),
        name="ffn",
    )(x, norm, wg, wu, wd)


def _proj_body(x_ref, g_ref, w_ref, o_ref):
    h = _rms(x_ref[...], g_ref[...]).astype(bf16)
    o_ref[...] = _dot(h, w_ref[...])


def _proj(x, norm, w, layer, j):
    t = x.shape[0]
    p = w.shape[-1]
    return pl.pallas_call(
        _proj_body,
        grid=(t // TOKEN_TILE,),
        in_specs=[
            pl.BlockSpec((TOKEN_TILE, D_MODEL), lambda i: (i, 0)),
            _layer_spec((1, D_MODEL))(layer),
            _layer_spec((D_MODEL, p))(j),
        ],
        out_specs=pl.BlockSpec((TOKEN_TILE, p), lambda i: (i, 0)),
        out_shape=jax.ShapeDtypeStruct((t, p), f32),
        compiler_params=_cparams("parallel"),
        name="proj",
    )(x, norm, w)


def _outproj_body(x_ref, ya_ref, yb_ref, w_ref, o_ref):
    half = w_ref.shape[0] // 2
    acc = _dot(ya_ref[...].astype(bf16), w_ref[:half, :])
    acc = acc + _dot(yb_ref[...].astype(bf16), w_ref[half:, :])
    o_ref[...] = x_ref[...] + acc


def _outproj(x, ya, yb, w, j):
    t = x.shape[0]
    half = ya.shape[-1]
    return pl.pallas_call(
        _outproj_body,
        grid=(t // TOKEN_TILE,),
        in_specs=[
            pl.BlockSpec((TOKEN_TILE, D_MODEL), lambda i: (i, 0)),
            pl.BlockSpec((TOKEN_TILE, half), lambda i: (i, 0)),
            pl.BlockSpec((TOKEN_TILE, half), lambda i: (i, 0)),
            _layer_spec((2 * half, D_MODEL))(j),
        ],
        out_specs=pl.BlockSpec((TOKEN_TILE, D_MODEL), lambda i: (i, 0)),
        out_shape=jax.ShapeDtypeStruct((t, D_MODEL), f32),
        compiler_params=_cparams("parallel"),
        name="outproj",
    )(x, ya, yb, w)


def _final_norm_body(x_ref, g_ref, o_ref):
    o_ref[...] = _rms(x_ref[...], g_ref[...])


def _final_norm(x, g):
    t = x.shape[0]
    return pl.pallas_call(
        _final_norm_body,
        grid=(t // TOKEN_TILE,),
        in_specs=[pl.BlockSpec((TOKEN_TILE, D_MODEL), lambda i: (i, 0)), _const_spec((1, D_MODEL))],
        out_specs=pl.BlockSpec((TOKEN_TILE, D_MODEL), lambda i: (i, 0)),
        out_shape=jax.ShapeDtypeStruct((t, D_MODEL), f32),
        compiler_params=_cparams("parallel"),
        name="final_norm",
    )(x, g)


def _seq_edge(blk, groups, tile, last):
    hit = None
    for off, n_seq, seq_len in groups:
        start, per, n = off // tile, seq_len // tile, n_seq * (seq_len // tile)
        rel = blk - start
        edge = (per - 1) if last else 0
        h = (rel >= 0) & (rel < n) & (lax.rem(jnp.maximum(rel, 0), per) == edge)
        hit = h if hit is None else (hit | h)
    return hit


S5_BLOCK = 8
S5_ROW = S5_BLOCK * S5_GROUP
S5_OCT = 4
S5_OCT_IN = LANES * S5_BLOCK
S5_CW = 8 * S5_STATE
S5_ZW = 4 * S5_CW


def _s5_weight_body(lr_ref, li_ref, st_ref, btr_ref, bti_ref, cr_ref, ci_ref,
                    kt_ref, er_ref, ei_ref, wor_ref, woi_ref, aqr_ref, aqi_ref):
    backward = pl.program_id(0) >= S5_GROUPS
    lam_re = jnp.minimum(lr_ref[...], -1e-4)
    lam_im = li_ref[...]
    step = jnp.exp(st_ref[...])
    zr, zi = lam_re * step, lam_im * step
    mag = jnp.exp(zr)
    a_re, a_im = mag * jnp.cos(zi), mag * jnp.sin(zi)
    den = lam_re * lam_re + lam_im * lam_im
    num_re = a_re - 1.0
    coef_re = (num_re * lam_re + a_im * lam_im) / den
    coef_im = (a_im * lam_re - num_re * lam_im) / den
    bt_re, bt_im = btr_ref[...], bti_ref[...]
    bb_re = coef_re * bt_re - coef_im * bt_im
    bb_im = coef_re * bt_im + coef_im * bt_re

    def power(ell):
        m = jnp.exp(ell * zr)
        return m * jnp.cos(ell * zi), m * jnp.sin(ell * zi)

    tile = lambda a: jnp.concatenate([a] * S5_BLOCK, axis=0)
    lag = (lax.broadcasted_iota(jnp.int32, (S5_ROW, S5_STATE), 0) // S5_GROUP).astype(f32)
    p_re, p_im = power(lag)
    tb_re, tb_im = tile(bb_re), tile(bb_im)
    e_re = p_re * tb_re - p_im * tb_im
    e_im = p_re * tb_im + p_im * tb_re
    er_ref[...] = e_re
    ei_ref[...] = e_im
    c_re, c_im = cr_ref[...], ci_ref[...]
    kt_ref[...] = _mm(e_re, c_re, "hi", _NT) - _mm(e_im, c_im, "hi", _NT)
    lag_out = jnp.where(backward, S5_BLOCK - lag, lag + 1.0)
    q_re, q_im = power(lag_out)
    tc_re, tc_im = tile(c_re), tile(c_im)
    wor_ref[...] = tc_re * q_re - tc_im * q_im
    woi_ref[...] = -(tc_re * q_im + tc_im * q_re)
    jj = (lax.broadcasted_iota(jnp.int32, (SUBLANES, S5_STATE), 0) + 1).astype(f32) * S5_BLOCK
    aqr_ref[...], aqi_ref[...] = power(jj)


def _s5_weights(lam_re, lam_im, log_step, b_re, b_im, c_re, c_im):
    dg = 2 * S5_GROUPS
    vec = lambda a: a.reshape(dg, 1, S5_STATE)
    st = jnp.broadcast_to(log_step.reshape(dg, 1, 1), (dg, 1, S5_STATE))
    bt = lambda b: jnp.swapaxes(b, -1, -2).reshape(dg, S5_GROUP, S5_STATE)
    cc = lambda c: c.reshape(dg, S5_GROUP, S5_STATE)
    spec = lambda r, c: pl.BlockSpec((None, r, c), lambda i: (i, 0, 0))
    out = lambda r, c: jax.ShapeDtypeStruct((dg, r, c), f32)
    return pl.pallas_call(
        _s5_weight_body,
        grid=(dg,),
        in_specs=[spec(1, S5_STATE)] * 3 + [spec(S5_GROUP, S5_STATE)] * 4,
        out_specs=[spec(S5_ROW, S5_GROUP)] + [spec(S5_ROW, S5_STATE)] * 4 + [spec(SUBLANES, S5_STATE)] * 2,
        out_shape=[out(S5_ROW, S5_GROUP)] + [out(S5_ROW, S5_STATE)] * 4 + [out(SUBLANES, S5_STATE)] * 2,
        compiler_params=_cparams("parallel"),
        name="s5_weights",
    )(vec(lam_re), vec(lam_im), st, bt(b_re), bt(b_im), cc(c_re), cc(c_im))


def _s5_assemble(kt, e_re, e_im, wo_re, wo_im, aq_re, aq_im):
    g, t, c, n = S5_GROUPS, S5_BLOCK, S5_GROUP, S5_STATE
    eye = jnp.eye(8, dtype=f32)
    k = kt.reshape(2, g, t, c, c)
    lag = jnp.arange(t)[None, :] - jnp.arange(t)[:, None]
    fwd = jnp.where((lag >= 0)[None, :, :, None, None], k[0][:, jnp.clip(lag, 0, t - 1)], 0.0)
    bwd = jnp.where((lag <= 0)[None, :, :, None, None], k[1][:, jnp.clip(-lag, 0, t - 1)], 0.0)
    toep = (fwd + bwd).reshape(S5_OCT, 8, t, t, c, c)
    toep = jnp.einsum("qgstic,gh->qsgithc", toep, eye).reshape(S5_OCT, S5_OCT_IN, S5_OCT_IN)

    def spread(a):
        a = a.reshape(S5_OCT, 8, t, c, n)
        return jnp.einsum("qgsin,gh->qsgihn", a, eye).reshape(S5_OCT, S5_OCT_IN, S5_CW)

    e = lambda a: a.reshape(2, g, t, c, n)
    inj = jnp.concatenate([spread(e(e_re)[0][:, ::-1]), spread(e(e_im)[0][:, ::-1]),
                           spread(e(e_re)[1]), spread(e(e_im)[1])], axis=-1)
    out_t = jnp.concatenate([spread(e(wo_re)[0]), spread(e(wo_im)[0]),
                             spread(e(wo_re)[1]), spread(e(wo_im)[1])], axis=-1)
    tab = lambda x: x.reshape(2, S5_OCT, 8, SUBLANES, n).transpose(0, 1, 3, 2, 4).reshape(2, S5_OCT, SUBLANES, S5_CW)
    tab_re = jnp.concatenate([tab(aq_re)[0], tab(aq_re)[1]], axis=-1)
    tab_im = jnp.concatenate([tab(aq_im)[0], tab(aq_im)[1]], axis=-1)
    return toep.astype(bf16), inj.astype(bf16), out_t.astype(bf16), tab_re, tab_im


def _s5_conv_body(x_ref, m_ref, inj_ref, out_ref, tr_ref, ti_ref, o_ref, z_buf, h_buf, *, groups):
    n_rows = x_ref.shape[0] // S5_BLOCK
    n_tiles = n_rows // SUBLANES
    tile0 = pl.program_id(1) * n_tiles
    tile_tokens = SUBLANES * S5_BLOCK
    xb = jnp.concatenate([x_ref[pl.ds(s, n_rows, stride=S5_BLOCK), :].astype(bf16) for s in range(S5_BLOCK)],
                         axis=1)
    z_buf[...] = _dot(xb, inj_ref[...])
    row = lax.broadcasted_iota(jnp.int32, (SUBLANES, S5_CW), 0)
    shape = (SUBLANES, S5_CW)

    def tables(reverse):
        lanes = slice(S5_CW, 2 * S5_CW) if reverse else slice(0, S5_CW)
        full = lambda r: (jnp.broadcast_to(tr_ref[r:r + 1, lanes], shape),
                          jnp.broadcast_to(ti_ref[r:r + 1, lanes], shape))
        levels = []
        for d in (1, 2, 4):
            ar, ai = full(d - 1)
            keep = (row < SUBLANES - d) if reverse else (row >= d)
            levels.append((d, jnp.where(keep, ar, 0.0), jnp.where(keep, ai, 0.0)))
        cr, ci = jnp.zeros(shape, f32), jnp.zeros(shape, f32)
        for r in range(SUBLANES):
            ar, ai = full(r)
            sel = (row == (SUBLANES - 1 - r)) if reverse else (row == r)
            cr, ci = jnp.where(sel, ar, cr), jnp.where(sel, ai, ci)
        return levels, cr, ci

    tabs = (tables(False), tables(True))

    def scan_tile(t, carry, reverse):
        levels, cr, ci = tabs[int(reverse)]
        base = 2 * S5_CW * int(reverse)
        re_l, im_l = slice(base, base + S5_CW), slice(base + S5_CW, base + 2 * S5_CW)
        fresh = _seq_edge(tile0 + t, groups, tile_tokens, last=reverse)
        c_re, c_im = (jnp.where(fresh, 0.0, c) for c in carry)
        rows = pl.ds(pl.multiple_of(t * SUBLANES, SUBLANES), SUBLANES)
        h_re, h_im = z_buf[rows, re_l], z_buf[rows, im_l]
        for d, ar, ai in levels:
            sh = (SUBLANES - d) if reverse else d
            s_re, s_im = pltpu.roll(h_re, sh, 0), pltpu.roll(h_im, sh, 0)
            h_re, h_im = h_re + (ar * s_re - ai * s_im), h_im + (ar * s_im + ai * s_re)
        h_re, h_im = h_re + (cr * c_re - ci * c_im), h_im + (cr * c_im + ci * c_re)
        edge_in = SUBLANES - 1 if reverse else 0
        sh = (SUBLANES - 1) if reverse else 1
        h_buf[rows, re_l] = jnp.where(row == edge_in, c_re, pltpu.roll(h_re, sh, 0))
        h_buf[rows, im_l] = jnp.where(row == edge_in, c_im, pltpu.roll(h_im, sh, 0))
        edge_out = 0 if reverse else SUBLANES - 1
        return (jnp.broadcast_to(h_re[edge_out:edge_out + 1, :], shape),
                jnp.broadcast_to(h_im[edge_out:edge_out + 1, :], shape))

    def tile_pair(i, carry):
        return (scan_tile(i, carry[0], False), scan_tile(n_tiles - 1 - i, carry[1], True))

    zero = (jnp.zeros(shape, f32), jnp.zeros(shape, f32))
    lax.fori_loop(0, n_tiles, tile_pair, (zero, zero))

    y = _dot(xb, m_ref[...])
    y = y + lax.dot_general(h_buf[...].astype(bf16), out_ref[...], _NT, preferred_element_type=f32)
    for s in range(S5_BLOCK):
        o_ref[pl.ds(s, n_rows, stride=S5_BLOCK), :] = y[:, s * LANES:(s + 1) * LANES]


def _s5_conv(proj, toep, inj, out_t, tab_re, tab_im, groups):
    t = proj.shape[0]
    tokens = max(seq_len for _, _, seq_len in groups)
    rows = tokens // S5_BLOCK
    per = lambda r, c: pl.BlockSpec((None, r, c), lambda q, r_: (q, 0, 0), pipeline_mode=pl.Buffered(1))
    return pl.pallas_call(
        functools.partial(_s5_conv_body, groups=groups),
        grid=(S5_OCT, t // tokens),
        in_specs=[pl.BlockSpec((tokens, LANES), lambda q, r: (r, q)),
                  per(S5_OCT_IN, S5_OCT_IN), per(S5_OCT_IN, S5_ZW), per(S5_OCT_IN, S5_ZW),
                  per(SUBLANES, 2 * S5_CW), per(SUBLANES, 2 * S5_CW)],
        out_specs=pl.BlockSpec((tokens, LANES), lambda q, r: (r, q)),
        out_shape=jax.ShapeDtypeStruct((t, D_A), f32),
        scratch_shapes=[pltpu.VMEM((rows, S5_ZW), f32), pltpu.VMEM((rows, S5_ZW), f32)],
        compiler_params=_cparams("parallel", "parallel"),
        name="s5_conv",
    )(proj, toep, inj, out_t, tab_re, tab_im)


def _s5_post_body(u_ref, y_ref, d_ref, w_ref, b_ref, o_ref):
    y = y_ref[...] + d_ref[...] * u_ref[...]
    y = jax.nn.gelu(y)
    gate = _dot(y.astype(bf16), w_ref[...]) + b_ref[...]
    o_ref[...] = y * jax.nn.sigmoid(gate)


def _s5_post(proj, y, d, glu_w, glu_b, j):
    t = proj.shape[0]
    tok = lambda i: (i, 0)
    return pl.pallas_call(
        _s5_post_body,
        grid=(t // TOKEN_TILE,),
        in_specs=[
            pl.BlockSpec((TOKEN_TILE, D_A), tok),
            pl.BlockSpec((TOKEN_TILE, D_A), tok),
            _layer_spec((1, D_A))(j),
            _layer_spec((D_A, D_A))(j),
            _layer_spec((1, D_A))(j),
        ],
        out_specs=pl.BlockSpec((TOKEN_TILE, D_A), tok),
        out_shape=jax.ShapeDtypeStruct((t, D_A), f32),
        compiler_params=_cparams("parallel"),
        name="s5_post",
    )(proj, y, d, glu_w, glu_b)


def _s5_mixer(proj, prm, j, groups):
    w = _s5_weights(prm["lam_re"][j], prm["lam_im"][j], prm["log_step"][j], prm["b_re"][j], prm["b_im"][j],
                    prm["c_re"][j], prm["c_im"][j])
    y = _s5_conv(proj, *_s5_assemble(*w), groups)
    return _s5_post(proj, y, prm["d"], prm["glu_w"], prm["glu_b"], j)


RW_TILE = 256
RW_CHUNK = 64
RW_PAIRS = RW_HEADS // 2
RW_CHUNKS_PER_STEP = 2
RW_PREC_A = "b1"
RW_PREC_INV = "b1"
RW_PREC_APPLY = "b1"
RW_PREC_SEQ = "b1"
RW_PREC_SUM = "x2"
RW_P_WIDTH = 2304
RW_LORA_OFF = 2048
RW_LORA_W = 256


def _head_ones():
    a = lax.broadcasted_iota(jnp.int32, (D_B, D_B), 0) // RW_HEAD
    b = lax.broadcasted_iota(jnp.int32, (D_B, D_B), 1) // RW_HEAD
    return jnp.where(a == b, 1.0, 0.0).astype(f32)


def _rw_prep_body(x_ref, xp_ref, xn_ref, mu_ref, lw_ref, w0_ref, a0_ref, kk_ref, ka_ref,
                  r_out, k_out, v_out, kk_out, kka_out, lwf_out, lwb_out, g_out, *, groups):
    i = pl.program_id(0)
    x = x_ref[:, D_A:]
    prev = jnp.where(_seq_edge(i, groups, RW_TILE, last=False), 0.0, xp_ref[:, D_A:])
    nxt = jnp.where(_seq_edge(i, groups, RW_TILE, last=True), 0.0, xn_ref[:, D_A:])
    row = lax.broadcasted_iota(jnp.int32, x.shape, 0)
    shifted = 0.5 * (_shift_rows(x, prev, 1, row) + _shift_rows(x, nxt, -1, row))
    p = x + (shifted - x) * mu_ref[:, D_A:]
    r = p[:, :D_B]
    k = p[:, D_B:2 * D_B]
    v = p[:, 2 * D_B:3 * D_B]
    lora = p[:, RW_LORA_OFF - D_A:RW_LORA_OFF - D_A + RW_LORA_W]
    lora_t = jnp.tanh(lora).astype(bf16)
    a = jax.nn.sigmoid(a0_ref[...] + _dot(lora.astype(bf16), lw_ref[2]))
    g = _dot(jax.nn.sigmoid(lora).astype(bf16), lw_ref[3])
    kx = k * kk_ref[...]
    ss = _mm(kx * kx, _head_ones(), RW_PREC_SUM)
    kk = kx / jnp.maximum(jnp.sqrt(ss), 1e-12)
    r_out[...] = r
    k_out[...] = k * (1.0 + (a - 1.0) * ka_ref[...])
    v_out[...] = v
    kk_out[...] = kk
    kka_out[...] = kk * a
    g_out[...] = g
    for direction, out in enumerate((lwf_out, lwb_out)):
        wl = -jax.nn.softplus(-(w0_ref[direction:direction + 1, :] + _dot(lora_t, lw_ref[direction]))) - 0.5
        out[...] = -jnp.exp(wl)


def _rw_prep(proj, prm, j, groups):
    t = proj.shape[0]
    tok = lambda i: (i, 0)
    prev_spec, next_spec = _halo_specs(RW_TILE, RW_P_WIDTH, 0, t)
    out = jax.ShapeDtypeStruct((t, D_B), f32)
    return pl.pallas_call(
        functools.partial(_rw_prep_body, groups=groups),
        grid=(t // RW_TILE,),
        in_specs=[pl.BlockSpec((RW_TILE, RW_P_WIDTH), tok), prev_spec, next_spec,
                  _layer_spec((1, RW_P_WIDTH))(j), _layer_spec((4, RW_LORA_W, D_B))(j),
                  _layer_spec((2, D_B))(j), _layer_spec((1, D_B))(j), _layer_spec((1, D_B))(j),
                  _layer_spec((1, D_B))(j)],
        out_specs=[pl.BlockSpec((RW_TILE, D_B), tok)] * 8,
        out_shape=[out] * 8,
        compiler_params=_cparams("parallel"),
        name="rw_prep",
    )(proj, proj, proj, prm["mu"], prm["lora_w"], prm["w0"], prm["a0"], prm["k_k"], prm["k_a"])


def _rw_chunk_body(r_ref, k_ref, v_ref, kk_ref, kka_ref, lw_ref, y_out, st, *, reverse, groups, n_steps):
    c = RW_CHUNK
    pid = pl.program_id(0)

    @pl.when(pid == 0)
    def _():
        st[...] = jnp.zeros_like(st)

    ri = lax.broadcasted_iota(jnp.int32, (c, c), 0)
    ci = lax.broadcasted_iota(jnp.int32, (c, c), 1)
    tri = jnp.where((ci >= ri) if reverse else (ci <= ri), 1.0, 0.0).astype(f32)
    edge = 0 if reverse else c - 1
    big_r = lax.broadcasted_iota(jnp.int32, (LANES, LANES), 0)
    big_c = lax.broadcasted_iota(jnp.int32, (LANES, LANES), 1)
    same_head = (big_r // RW_HEAD) == (big_c // RW_HEAD)
    s_idx, j_idx = big_r % RW_HEAD, big_c % RW_HEAD
    strict = same_head & ((j_idx > s_idx) if reverse else (j_idx < s_idx))
    incl = same_head & ((j_idx >= s_idx) if reverse else (j_idx <= s_idx))
    eye = big_r == big_c
    eye_f = jnp.where(eye, 1.0, 0.0).astype(f32)
    head0 = lax.broadcasted_iota(jnp.int32, (c, LANES), 1) < RW_HEAD

    def expand(x):
        return jnp.concatenate([jnp.where(head0, x, 0.0), jnp.where(head0, 0.0, x)], axis=0)

    def collapse(x):
        return x[:c] + x[c:]

    n_sub = r_ref.shape[0] // c
    lanes = [slice(p * LANES, (p + 1) * LANES) for p in range(RW_PAIRS)]
    rows = [slice(ch * c, (ch + 1) * c) for ch in range(n_sub)]
    units = [(ch, p) for ch in range(n_sub) for p in range(RW_PAIRS)]
    pre = []
    for rs in rows:
        lw = lw_ref[rs, :]
        cum = _dot_hi(tri, lw)
        total = cum[edge:edge + 1, :]
        e_neg, tail = jnp.exp(-cum), jnp.exp(total - cum)
        k, kka = k_ref[rs, :], kka_ref[rs, :]
        pre.append(dict(at=-kk_ref[rs, :] * jnp.exp(cum - lw), rt=r_ref[rs, :] * jnp.exp(cum),
                        kh=k * e_neg, bh=kka * e_neg, kp=k * tail, bp=kka * tail, v=v_ref[rs, :],
                        pc=jnp.exp(total)))

    def get(name):
        return [pre[ch][name][:, lanes[p]] for ch, p in units]

    at, rt, kh, bh, kp, bp, v = (get(n) for n in ("at", "rt", "kh", "bh", "kp", "bp", "v"))
    at_exp = [expand(x) for x in at]
    v_exp = [expand(x) for x in v]
    z = [_mm(jnp.concatenate([ae, expand(rr)], axis=0), jnp.concatenate([a, a, b, b], axis=0), RW_PREC_A, _NT)
         for ae, rr, a, b in zip(at_exp, rt, kh, bh)]
    a_ak = [jnp.where(strict, zz[:LANES, :LANES], 0.0) for zz in z]
    a_ab = [jnp.where(strict, zz[:LANES, LANES:], 0.0) for zz in z]
    a_rk = [jnp.where(incl, zz[LANES:, :LANES], 0.0) for zz in z]
    a_rb = [jnp.where(incl, zz[LANES:, LANES:], 0.0) for zz in z]
    tinv = [eye_f + a for a in a_ab]
    pw = a_ab
    for _ in range(int(math.log2(c)) - 1):
        pw = [_mm(q, q, RW_PREC_INV) for q in pw]
        tinv = [_mm(t, eye_f + q, RW_PREC_INV) for t, q in zip(tinv, pw)]
    w = [_mm(a, ve, RW_PREC_APPLY) for a, ve in zip(a_ak, v_exp)]
    tu = [_mm(t, jnp.concatenate([ae, ww], axis=1), RW_PREC_APPLY) for t, ae, ww in zip(tinv, at_exp, w)]
    atp_exp = [x[:, :LANES] for x in tu]
    u0_exp = [x[:, LANES:] for x in tu]
    y0 = [collapse(_mm(ark, ve, RW_PREC_APPLY) + _mm(arb, ue, RW_PREC_APPLY))
          for ark, arb, ve, ue in zip(a_rk, a_rb, v_exp, u0_exp)]
    rp = [rr + collapse(_mm(arb, ae, RW_PREC_APPLY)) for rr, arb, ae in zip(rt, a_rb, atp_exp)]
    atp = [collapse(x) for x in atp_exp]
    u0 = [collapse(x) for x in u0_exp]
    g_full = [_mm(b, a, RW_PREC_APPLY, _TN) for b, a in zip(bp, atp)]
    h_full = [_mm(kd, vv, RW_PREC_APPLY, _TN) + _mm(b, u, RW_PREC_APPLY, _TN)
              for kd, vv, b, u in zip(kp, v, bp, u0)]
    blk = (n_steps - 1 - pid) if reverse else pid
    state = [st[:, sl] for sl in lanes]
    for ch in (range(n_sub - 1, -1, -1) if reverse else range(n_sub)):
        fresh = _seq_edge(blk * n_sub + ch, groups, c, last=reverse)
        for p in range(RW_PAIRS):
            i, sl = ch * RW_PAIRS + p, lanes[p]
            pc = jnp.broadcast_to(pre[ch]["pc"][:, sl], (LANES, LANES))
            g_bd = jnp.where(same_head, g_full[i], 0.0) + jnp.where(eye, pc, 0.0)
            h_bd = jnp.where(same_head, h_full[i], 0.0)
            s0 = jnp.where(fresh, 0.0, state[p])
            y_out[rows[ch], sl] = _mm(rp[i], s0, RW_PREC_SEQ) + y0[i]
            state[p] = _mm(g_bd, s0, RW_PREC_SEQ) + h_bd
    for p in range(RW_PAIRS):
        st[:, lanes[p]] = state[p]


def _rw_chunk(r, k, v, kk, kka, lw, groups, reverse):
    t = r.shape[0]
    step_rows = RW_CHUNK * RW_CHUNKS_PER_STEP
    n_steps = t // step_rows
    order = (lambda i: (n_steps - 1 - i, 0)) if reverse else (lambda i: (i, 0))
    return pl.pallas_call(
        functools.partial(_rw_chunk_body, reverse=reverse, groups=groups, n_steps=n_steps),
        grid=(n_steps,),
        in_specs=[pl.BlockSpec((step_rows, D_B), order)] * 6,
        out_specs=pl.BlockSpec((step_rows, D_B), order),
        out_shape=jax.ShapeDtypeStruct((t, D_B), f32),
        scratch_shapes=[pltpu.VMEM((LANES, D_B), f32)],
        compiler_params=_cparams("arbitrary"),
        name="rw_chunk_rev" if reverse else "rw_chunk_fwd",
    )(r, k, v, kk, kka, lw)


def _rw_post_body(yf_ref, yb_ref, r_ref, k_ref, v_ref, g_ref, rk_ref, lnw_ref, lnb_ref, o_ref):
    ones = _head_ones()
    inv = 1.0 / RW_HEAD
    y = yf_ref[...] + yb_ref[...]
    mean = _mm(y, ones, RW_PREC_SUM) * inv
    yc = y - mean
    var = _mm(yc * yc, ones, RW_PREC_SUM) * inv
    yn = yc * lax.rsqrt(var + RW_GN_EPS) * lnw_ref[...] + lnb_ref[...]
    bonus = _mm(r_ref[...] * k_ref[...] * rk_ref[...], ones, RW_PREC_SUM) * v_ref[...]
    o_ref[...] = (yn + bonus) * g_ref[...]


def _rw_post(y_f, y_b, r, k, v, g, prm, j):
    t = y_f.shape[0]
    tok = pl.BlockSpec((TOKEN_TILE, D_B), lambda i: (i, 0))
    vec = _layer_spec((1, D_B))(j)
    return pl.pallas_call(
        _rw_post_body,
        grid=(t // TOKEN_TILE,),
        in_specs=[tok] * 6 + [vec] * 3,
        out_specs=tok,
        out_shape=jax.ShapeDtypeStruct((t, D_B), f32),
        compiler_params=_cparams("parallel"),
        name="rw_post",
    )(y_f, y_b, r, k, v, g, prm["r_k"], prm["ln_w"], prm["ln_b"])


def _rwkv7_mixer(proj, prm, j, groups):
    r, k, v, kk, kka, lw_f, lw_b, g = _rw_prep(proj, prm, j, groups)
    y_f = _rw_chunk(r, k, v, kk, kka, lw_f, groups, reverse=False)
    y_b = _rw_chunk(r, k, v, kk, kka, lw_b, groups, reverse=True)
    return _rw_post(y_f, y_b, r, k, v, g, prm, j)


def _halo_specs(tile, width, col_blk, n_rows):
    per = tile // SUBLANES
    last8 = n_rows // SUBLANES - 1
    prev = pl.BlockSpec((SUBLANES, width), lambda i: (jnp.maximum(i * per - 1, 0), col_blk))
    nxt = pl.BlockSpec((SUBLANES, width), lambda i: (jnp.minimum((i + 1) * per, last8), col_blk))
    return prev, nxt


def _shift_rows(x, halo, k, row):
    n = x.shape[0]
    if k > 0:
        y = pltpu.roll(x, k, 0)
        for r in range(k):
            y = jnp.where(row == r, halo[SUBLANES - k + r:SUBLANES - k + r + 1, :], y)
        return y
    y = pltpu.roll(x, n - 1, 0)
    return jnp.where(row == n - 1, halo[0:1, :], y)


LRU_TILE = 512
CD_XB_BLK = 3
CD_GB_BLK = 4


def _lru_gate_body(x_ref, xp_ref, xn_ref, cw_ref, cb_ref, lam_ref, w_ref, b_ref,
                   af_ref, bf_ref, ab_ref, bb_ref, *, groups):
    i = pl.program_id(0)
    first = _seq_edge(i, groups, LRU_TILE, last=False)
    last = _seq_edge(i, groups, LRU_TILE, last=True)
    x = x_ref[...]
    prev = jnp.where(first, 0.0, xp_ref[...])
    nxt = jnp.where(last, 0.0, xn_ref[...])
    row = lax.broadcasted_iota(jnp.int32, x.shape, 0)
    xc = (cw_ref[0:1, :] * _shift_rows(x, prev, 2, row) + cw_ref[1:2, :] * _shift_rows(x, prev, 1, row)
          + cw_ref[2:3, :] * x + cw_ref[3:4, :] * _shift_rows(x, nxt, -1, row) + cb_ref[...])
    pre = _dot(xc.astype(bf16), w_ref[...]) + b_ref[...]
    outs = ((af_ref, bf_ref), (ab_ref, bb_ref))
    for direction in range(2):
        base = 2 * direction * D_C
        gate_r = jax.nn.sigmoid(pre[:, base:base + D_C])
        gate_i = jax.nn.sigmoid(pre[:, base + D_C:base + 2 * D_C])
        log_a = -LRU_C * gate_r * jax.nn.softplus(-lam_ref[direction:direction + 1, :])
        t = jnp.tanh(log_a)
        mult = jnp.sqrt(-2.0 * t / (1.0 - t))
        a_ref, b_ref_out = outs[direction]
        a_ref[...] = jnp.exp(log_a)
        b_ref_out[...] = mult * gate_i * xc


def _lru_gates(proj, prm, j, groups):
    t = proj.shape[0]
    tok = lambda i: (i, 0)
    prev_spec, next_spec = _halo_specs(LRU_TILE, D_C, CD_XB_BLK, t)
    out = jax.ShapeDtypeStruct((t, D_C), f32)
    return pl.pallas_call(
        functools.partial(_lru_gate_body, groups=groups),
        grid=(t // LRU_TILE,),
        in_specs=[
            pl.BlockSpec((LRU_TILE, D_C), lambda i: (i, CD_XB_BLK)),
            prev_spec,
            next_spec,
            _layer_spec((LRU_CONV, D_C))(j),
            _layer_spec((1, D_C))(j),
            _layer_spec((2, D_C))(j),
            _layer_spec((D_C, 4 * D_C))(j),
            _layer_spec((1, 4 * D_C))(j),
        ],
        out_specs=[pl.BlockSpec((LRU_TILE, D_C), tok)] * 4,
        out_shape=[out] * 4,
        compiler_params=_cparams("parallel"),
        name="lru_gates",
    )(proj, proj, proj, prm["conv_w"], prm["conv_b"], prm["lam"], prm["w_gates"], prm["b_gates"])


def _lru_scan_body(af_ref, bf_ref, ab_ref, bb_ref, hf_ref, hb_ref, carry_f, carry_b, *, groups, n_blk):
    i = pl.program_id(0)

    @pl.when(_seq_edge(i, groups, LRU_TILE, last=False))
    def _():
        carry_f[...] = jnp.zeros_like(carry_f)

    @pl.when(_seq_edge(n_blk - 1 - i, groups, LRU_TILE, last=True))
    def _():
        carry_b[...] = jnp.zeros_like(carry_b)

    row = lax.broadcasted_iota(jnp.int32, (SUBLANES, D_C), 0)
    n_tiles = LRU_TILE // SUBLANES

    def run(a_ref, b_ref, h_ref, carry, reverse):
        def tile_step(k, c):
            t = (n_tiles - 1 - k) if reverse else k
            r0 = pl.multiple_of(t * SUBLANES, SUBLANES)
            a = a_ref[pl.ds(r0, SUBLANES), :]
            b = b_ref[pl.ds(r0, SUBLANES), :]
            for d in (1, 2, 4):
                sh = (SUBLANES - d) if reverse else d
                keep = (row < SUBLANES - d) if reverse else (row >= d)
                a_s = jnp.where(keep, pltpu.roll(a, sh, 0), 1.0)
                b_s = jnp.where(keep, pltpu.roll(b, sh, 0), 0.0)
                b = b + a * b_s
                a = a * a_s
            h = b + a * c
            h_ref[pl.ds(r0, SUBLANES), :] = h
            edge = 0 if reverse else SUBLANES - 1
            return jnp.broadcast_to(h[edge:edge + 1, :], (SUBLANES, D_C))

        carry[...] = lax.fori_loop(0, n_tiles, tile_step, carry[...])

    run(af_ref, bf_ref, hf_ref, carry_f, False)
    run(ab_ref, bb_ref, hb_ref, carry_b, True)


def _lru_scan(a_f, b_f, a_b, b_b, groups):
    t = a_f.shape[0]
    n_blk = t // LRU_TILE
    fwd = lambda i: (i, 0)
    bwd = lambda i: (n_blk - 1 - i, 0)
    out = jax.ShapeDtypeStruct((t, D_C), f32)
    return pl.pallas_call(
        functools.partial(_lru_scan_body, groups=groups, n_blk=n_blk),
        grid=(n_blk,),
        in_specs=[pl.BlockSpec((LRU_TILE, D_C), fwd), pl.BlockSpec((LRU_TILE, D_C), fwd),
                  pl.BlockSpec((LRU_TILE, D_C), bwd), pl.BlockSpec((LRU_TILE, D_C), bwd)],
        out_specs=[pl.BlockSpec((LRU_TILE, D_C), fwd), pl.BlockSpec((LRU_TILE, D_C), bwd)],
        out_shape=[out, out],
        scratch_shapes=[pltpu.VMEM((SUBLANES, D_C), f32), pltpu.VMEM((SUBLANES, D_C), f32)],
        compiler_params=_cparams("arbitrary"),
        name="lru_scan",
    )(a_f, b_f, a_b, b_b)


HY_N2 = 256
HY_CW = 128
HY_SC_TILE = 512
HY_MLP_TILE = 512
HY_PREC_S1 = "b1"
HY_PREC_S2 = "b1"
HY_S1_UNROLL = 4


def _hy_dims(seq_len):
    n = 2 * seq_len
    n1 = n // HY_N2
    lb = n1 // 2
    k1 = n1 // 2 + 1
    k1p = -(-k1 // SUBLANES) * SUBLANES
    return n, n1, lb, k1, k1p


def _hy_tables(seq_len):
    n, n1, lb, k1, k1p = _hy_dims(seq_len)
    two_pi = 2.0 * math.pi
    kk = np.arange(k1, dtype=np.float64)[:, None]
    nn = np.arange(lb, dtype=np.float64)[None, :]
    ang1 = two_pi * kk * nn / n1
    f1 = np.zeros((2 * k1p, lb), np.float64)
    f1[:k1] = np.cos(ang1)
    f1[k1p:k1p + k1] = -np.sin(ang1)
    weight = np.where((np.arange(k1) == 0) | (np.arange(k1) == n1 // 2), 1.0, 2.0)[None, :] / n
    f1inv = np.zeros((lb, 2 * k1p), np.float64)
    f1inv[:, :k1] = weight * np.cos(ang1.T)
    f1inv[:, k1p:k1p + k1] = -weight * np.sin(ang1.T)
    idx = np.arange(HY_N2, dtype=np.float64)
    ang2 = two_pi * np.outer(idx, idx) / HY_N2
    f2 = np.concatenate([np.cos(ang2), -np.sin(ang2)], axis=0)
    prod = (jnp.arange(k1, dtype=jnp.int32)[:, None] * jnp.arange(HY_N2, dtype=jnp.int32)[None, :]) % n
    ang = prod.astype(f32) * f32(two_pi / n)
    bc = lambda a: jnp.broadcast_to(a.reshape(k1 * HY_N2, 1), (k1 * HY_N2, HY_CW))
    eye8 = np.eye(SUBLANES)
    return dict(f1=jnp.asarray(np.kron(f1, eye8), f32), f1inv=jnp.asarray(np.kron(f1inv, eye8), f32),
                f2=jnp.asarray(f2, f32), tw_re=bc(jnp.cos(ang)), tw_im=bc(-jnp.sin(ang)))


def _hy_slab(k):
    return pl.ds(k * HY_N2 if isinstance(k, int) else pl.multiple_of(k * HY_N2, HY_N2), HY_N2)


def _hy_slab_loop(slab, k1):
    def pair(i, carry):
        slab(2 * i, carry)
        slab(2 * i + 1, carry)
        return carry

    lax.fori_loop(0, k1 // 2, pair, 0)
    if k1 % 2:
        slab(k1 - 1, 0)


def _hy_forward(z_ref, f1_ref, f2_ref, twr_ref, twi_ref, wr, wi, dims):
    _, _, lb, k1, k1p = dims

    def stage1_group(n2):
        tiles = [z_ref[pl.ds(pl.multiple_of(b * HY_N2 + n2, SUBLANES), SUBLANES), :] for b in range(lb)]
        y = _mm(f1_ref[...], jnp.concatenate(tiles, axis=0), HY_PREC_S1)
        for s in range(k1p):
            dst = pl.ds(pl.multiple_of(s * HY_N2 + n2, SUBLANES), SUBLANES)
            wr[dst, :] = y[s * SUBLANES:(s + 1) * SUBLANES]
            wi[dst, :] = y[(k1p + s) * SUBLANES:(k1p + s + 1) * SUBLANES]

    def stage1(g, carry):
        for u in range(HY_S1_UNROLL):
            stage1_group((g * HY_S1_UNROLL + u) * SUBLANES)
        return carry

    lax.fori_loop(0, HY_N2 // (SUBLANES * HY_S1_UNROLL), stage1, 0)

    def stage2(k, carry):
        sl = _hy_slab(k)
        yr, yi, tr, ti = wr[sl, :], wi[sl, :], twr_ref[sl, :], twi_ref[sl, :]
        ar = yr * tr - yi * ti
        ai = yr * ti + yi * tr
        pq = _mm(f2_ref[...], jnp.concatenate([ar, ai], axis=1), HY_PREC_S2)
        p, q = pq[:, :HY_CW], pq[:, HY_CW:]
        wr[sl, :] = p[:HY_N2] - q[HY_N2:]
        wi[sl, :] = p[HY_N2:] + q[:HY_N2]
        return carry

    _hy_slab_loop(stage2, k1)


def _hy_inverse(hr_ref, hi_ref, f1inv_ref, f2_ref, twr_ref, twi_ref, wr, wi, o_ref, dims):
    _, _, lb, k1, k1p = dims

    def stage2(k, carry):
        sl = _hy_slab(k)
        xr, xi, hr, hi = wr[sl, :], wi[sl, :], hr_ref[sl, :], hi_ref[sl, :]
        zr = xr * hr - xi * hi
        zi = xr * hi + xi * hr
        pq = _mm(f2_ref[...], jnp.concatenate([zr, zi], axis=1), HY_PREC_S2)
        p, q = pq[:, :HY_CW], pq[:, HY_CW:]
        vr = p[:HY_N2] + q[HY_N2:]
        vi = q[:HY_N2] - p[HY_N2:]
        tr, ti = twr_ref[sl, :], twi_ref[sl, :]
        wr[sl, :] = vr * tr + vi * ti
        wi[sl, :] = vi * tr - vr * ti
        return carry

    _hy_slab_loop(stage2, k1)

    def stage1_group(n2):
        src = [pl.ds(pl.multiple_of(s * HY_N2 + n2, SUBLANES), SUBLANES) for s in range(k1p)]
        tiles = [wr[d, :] for d in src] + [wi[d, :] for d in src]
        x = _mm(f1inv_ref[...], jnp.concatenate(tiles, axis=0), HY_PREC_S1)
        for b in range(lb):
            dst = pl.ds(pl.multiple_of(b * HY_N2 + n2, SUBLANES), SUBLANES)
            o_ref[dst, :] = x[b * SUBLANES:(b + 1) * SUBLANES]

    def stage1(g, carry):
        for u in range(HY_S1_UNROLL):
            stage1_group((g * HY_S1_UNROLL + u) * SUBLANES)
        return carry

    lax.fori_loop(0, HY_N2 // (SUBLANES * HY_S1_UNROLL), stage1, 0)


def _hy_shortconv_body(x_ref, xp_ref, xn_ref, w_ref, b_ref, o_ref, *, groups):
    i = pl.program_id(0)
    x = x_ref[...]
    prev = jnp.where(_seq_edge(i, groups, HY_SC_TILE, last=False), 0.0, xp_ref[...])
    nxt = jnp.where(_seq_edge(i, groups, HY_SC_TILE, last=True), 0.0, xn_ref[...])
    row = lax.broadcasted_iota(jnp.int32, x.shape, 0)
    o_ref[...] = (w_ref[0:1, :] * _shift_rows(x, prev, 1, row) + w_ref[1:2, :] * x
                  + w_ref[2:3, :] * _shift_rows(x, nxt, -1, row) + b_ref[...])


def _hy_shortconv(proj, conv_w, conv_b, j, groups):
    t = proj.shape[0]
    width = 3 * D_D
    prev_spec, next_spec = _halo_specs(HY_SC_TILE, width, 0, t)
    return pl.pallas_call(
        functools.partial(_hy_shortconv_body, groups=groups),
        grid=(t // HY_SC_TILE,),
        in_specs=[pl.BlockSpec((HY_SC_TILE, width), lambda i: (i, 0)), prev_spec, next_spec,
                  _layer_spec((3, width))(j), _layer_spec((1, width))(j)],
        out_specs=pl.BlockSpec((HY_SC_TILE, width), lambda i: (i, 0)),
        out_shape=jax.ShapeDtypeStruct((t, width), f32),
        compiler_params=_cparams("parallel"),
        name="hy_shortconv",
    )(proj, proj, proj, conv_w, conv_b)


def _hy_mlp_body(z_ref, w1_ref, b1_ref, w2_ref, b2_ref, fr_ref, w3_ref, dl_ref, o_ref):
    z = z_ref[...]
    h = jnp.sin(fr_ref[0:1, :] * (_dot_hi(z, w1_ref[...]) + b1_ref[...]))
    h = jnp.sin(fr_ref[1:2, :] * (_dot_hi(h, w2_ref[...]) + b2_ref[...]))
    h = _dot_hi(h, w3_ref[...])
    o_ref[...] = h * jnp.exp(-z[:, 0:1] * dl_ref[...])


def _hy_mlp(seq_len, w1, b1, w2, b2, freq, w3):
    t = jnp.linspace(0.0, 1.0, seq_len, dtype=f32)[:, None]
    bands = (HY_POS_EMB - 1) // 2
    freqs = jnp.linspace(1e-4, bands - 1, bands, dtype=f32)[None, :]
    wpos = (2.0 * math.pi / seq_len) * jnp.arange(seq_len, dtype=f32)[:, None]
    z = jnp.concatenate([t, jnp.cos(freqs * wpos), -jnp.sin(freqs * wpos)], axis=-1)
    z = jnp.pad(z, ((0, 0), (0, LANES - HY_POS_EMB)))
    w1 = jnp.pad(w1, ((0, LANES - HY_POS_EMB), (0, 0)))
    max_decay = math.log(1e-2) / 0.3
    min_decay = math.log(1e-2) / 1.5
    deltas = jnp.abs(jnp.linspace(min_decay, max_decay, D_D, dtype=f32))
    deltas = jnp.tile(deltas, 4)[None, :]
    width = 4 * D_D
    hid = HY_FILTER_HIDDEN
    return pl.pallas_call(
        _hy_mlp_body,
        grid=(seq_len // HY_MLP_TILE,),
        in_specs=[pl.BlockSpec((HY_MLP_TILE, LANES), lambda i: (i, 0)),
                  _const_spec((LANES, hid)), _const_spec((1, hid)), _const_spec((hid, hid)),
                  _const_spec((1, hid)), _const_spec((2, hid)), _const_spec((hid, width)),
                  _const_spec((1, width))],
        out_specs=pl.BlockSpec((HY_MLP_TILE, width), lambda i: (i, 0)),
        out_shape=jax.ShapeDtypeStruct((seq_len, width), f32),
        compiler_params=_cparams("parallel"),
        name="hy_mlp",
    )(z, w1, b1[None, :], w2, b2[None, :], freq, w3, deltas)


def _hy_spectrum_body(hf_ref, hb_ref, f1_ref, f2_ref, twr_ref, twi_ref, or_ref, oi_ref,
                      zb, wr, wi, *, dims):
    k1 = dims[3]
    rows = k1 * HY_N2
    hf = hf_ref[...]
    hb = hb_ref[...]
    row = lax.broadcasted_iota(jnp.int32, hb.shape, 0)
    hb0 = jnp.where(row == 0, 0.0, hb)
    scale = 1.0 / (jnp.sum(jnp.abs(hf), axis=0, keepdims=True) + jnp.sum(jnp.abs(hb0), axis=0, keepdims=True))
    _hy_forward(hf_ref, f1_ref, f2_ref, twr_ref, twi_ref, wr, wi, dims)
    or_ref[...] = wr[:rows, :] * scale
    oi_ref[...] = wi[:rows, :] * scale
    zb[...] = hb0
    _hy_forward(zb, f1_ref, f2_ref, twr_ref, twi_ref, wr, wi, dims)
    or_ref[...] = or_ref[...] + wr[:rows, :] * scale
    oi_ref[...] = oi_ref[...] - wi[:rows, :] * scale


def _hy_spectrum(taps, tables, seq_len):
    dims = _hy_dims(seq_len)
    _, _, lb, k1, k1p = dims
    n_ct = D_D // HY_CW
    rows = k1 * HY_N2
    out = jax.ShapeDtypeStruct((2, rows, D_D), f32)
    return pl.pallas_call(
        functools.partial(_hy_spectrum_body, dims=dims),
        grid=(2, n_ct),
        in_specs=[pl.BlockSpec((seq_len, HY_CW), lambda o, c: (0, o * n_ct + c), pipeline_mode=pl.Buffered(1)),
                  pl.BlockSpec((seq_len, HY_CW), lambda o, c: (0, (2 + o) * n_ct + c),
                               pipeline_mode=pl.Buffered(1)),
                  _const_spec((2 * k1p * SUBLANES, lb * SUBLANES)), _const_spec((2 * HY_N2, HY_N2)),
                  _const_spec((rows, HY_CW)), _const_spec((rows, HY_CW))],
        out_specs=[pl.BlockSpec((None, rows, HY_CW), lambda o, c: (o, 0, c))] * 2,
        out_shape=[out, out],
        scratch_shapes=[pltpu.VMEM((seq_len, HY_CW), f32),
                        pltpu.VMEM((k1p * HY_N2, HY_CW), f32), pltpu.VMEM((k1p * HY_N2, HY_CW), f32)],
        compiler_params=_cparams("arbitrary", "arbitrary"),
        name="hy_spectrum",
    )(taps, taps, tables["f1"], tables["f2"], tables["tw_re"], tables["tw_im"])


def _hy_conv_body(z_ref, g_ref, hr_ref, hi_ref, bias_ref, f1_ref, f1inv_ref, f2_ref, twr_ref, twi_ref,
                  o_ref, wr, wi, *, dims):
    _hy_forward(z_ref, f1_ref, f2_ref, twr_ref, twi_ref, wr, wi, dims)
    _hy_inverse(hr_ref, hi_ref, f1inv_ref, f2_ref, twr_ref, twi_ref, wr, wi, o_ref, dims)
    bias = bias_ref[...]
    n_chunks = z_ref.shape[0] // HY_N2

    def gate(c, carry):
        sl = pl.ds(pl.multiple_of(c * HY_N2, HY_N2), HY_N2)
        o_ref[sl, :] = g_ref[sl, :] * (o_ref[sl, :] + bias * z_ref[sl, :])
        return carry

    lax.fori_loop(0, n_chunks, gate, 0)


def _hy_conv(z, z_off, z_col, gate, gate_off, gate_col, h_re, h_im, bias, tables, order, j, n_seq, seq_len):
    dims = _hy_dims(seq_len)
    _, _, lb, k1, k1p = dims
    n_ct = D_D // HY_CW
    rows = k1 * HY_N2
    z0, g0 = z_off // seq_len, gate_off // seq_len
    one = pl.Buffered(1)
    return pl.pallas_call(
        functools.partial(_hy_conv_body, dims=dims),
        grid=(n_ct, n_seq),
        in_specs=[
            pl.BlockSpec((seq_len, HY_CW), lambda c, b: (z0 + b, z_col * n_ct + c), pipeline_mode=one),
            pl.BlockSpec((seq_len, HY_CW), lambda c, b: (g0 + b, gate_col * n_ct + c), pipeline_mode=one),
            pl.BlockSpec((None, rows, HY_CW), lambda c, b: (order, 0, c), pipeline_mode=one),
            pl.BlockSpec((None, rows, HY_CW), lambda c, b: (order, 0, c), pipeline_mode=one),
            pl.BlockSpec((None, None, 1, HY_CW), lambda c, b: (j, order, 0, c)),
            _const_spec((2 * k1p * SUBLANES, lb * SUBLANES)), _const_spec((lb * SUBLANES, 2 * k1p * SUBLANES)), _const_spec((2 * HY_N2, HY_N2)),
            _const_spec((rows, HY_CW)), _const_spec((rows, HY_CW)),
        ],
        out_specs=pl.BlockSpec((seq_len, HY_CW), lambda c, b: (b, c)),
        out_shape=jax.ShapeDtypeStruct((n_seq * seq_len, D_D), f32),
        scratch_shapes=[pltpu.VMEM((k1p * HY_N2, HY_CW), f32), pltpu.VMEM((k1p * HY_N2, HY_CW), f32)],
        compiler_params=_cparams("arbitrary", "arbitrary"),
        name="hy_conv",
    )(z, gate, h_re, h_im, bias, tables["f1"], tables["f1inv"], tables["f2"], tables["tw_re"], tables["tw_im"])


def _hyena_mixer(proj, prm, j, groups):
    pc = _hy_shortconv(proj, prm["conv_w"], prm["conv_b"], j, groups)
    outs = []
    for tok_off, n_seq, seq_len in groups:
        tables = _hy_tables(seq_len)
        taps = _hy_mlp(seq_len, prm["f_w1"][j], prm["f_b1"][j], prm["f_w2"][j], prm["f_b2"][j],
                       prm["f_freq"][j], prm["f_w3"][j])
        h_re, h_im = _hy_spectrum(taps, tables, seq_len)
        z1 = _hy_conv(pc, tok_off, 0, pc, tok_off, 1, h_re, h_im, prm["bias"], tables, 0, j, n_seq, seq_len)
        y = _hy_conv(z1, 0, 0, pc, tok_off, 2, h_re, h_im, prm["bias"], tables, 1, j, n_seq, seq_len)
        outs.append(y)
    return jnp.concatenate(outs, axis=0)


def _outproj_odd_body(x_ref, hf_ref, hb_ref, gb_ref, yd_ref, w_ref, o_ref):
    y_c = (hf_ref[...] + hb_ref[...]) * jax.nn.gelu(gb_ref[...])
    acc = _dot(y_c.astype(bf16), w_ref[:D_C, :])
    acc = acc + _dot(yd_ref[...].astype(bf16), w_ref[D_C:, :])
    o_ref[...] = x_ref[...] + acc


def _outproj_odd(x, h_f, h_b, proj, y_d, w, j):
    t = x.shape[0]
    tok = lambda i: (i, 0)
    return pl.pallas_call(
        _outproj_odd_body,
        grid=(t // TOKEN_TILE,),
        in_specs=[
            pl.BlockSpec((TOKEN_TILE, D_MODEL), tok),
            pl.BlockSpec((TOKEN_TILE, D_C), tok),
            pl.BlockSpec((TOKEN_TILE, D_C), tok),
            pl.BlockSpec((TOKEN_TILE, D_C), lambda i: (i, CD_GB_BLK)),
            pl.BlockSpec((TOKEN_TILE, D_D), tok),
            _layer_spec((D_C + D_D, D_MODEL))(j),
        ],
        out_specs=pl.BlockSpec((TOKEN_TILE, D_MODEL), tok),
        out_shape=jax.ShapeDtypeStruct((t, D_MODEL), f32),
        compiler_params=_cparams("parallel"),
        name="outproj_odd",
    )(x, h_f, h_b, proj, y_d, w)


def _block_diag_heads(w):
    n, h, a, b = w.shape
    return jnp.einsum("nhij,hk->nhikj", w, jnp.eye(h, dtype=w.dtype)).reshape(n, h * a, h * b)


def _pad_rows(w, lo, total):
    return jnp.pad(w, ((0, 0), (lo, total - lo - w.shape[1]), (0, 0)))


def kernel(x_prompt, x_sample, ffn1_norm, ffn1_w_gate, ffn1_w_up, ffn1_w_down, mix_norm, ffn2_norm, ffn2_w_gate, ffn2_w_up, ffn2_w_down, ab_w_in, ab_w_out, s5_lambda_re, s5_lambda_im, s5_log_step, s5_b_re, s5_b_im, s5_c_re, s5_c_im, s5_d, s5_glu_w, s5_glu_b, rw_mu, rw_w0, rw_w_up, rw_a0, rw_a_up, rw_g_up, rw_k_k, rw_k_a, rw_r_k, rw_ln_w, rw_ln_b, cd_w_in, cd_w_out, lru_conv_w, lru_conv_b, lru_lambda, lru_wa, lru_ba, lru_wx, lru_bx, hy_conv_w, hy_conv_b, hy_f_w1, hy_f_b1, hy_f_w2, hy_f_b2, hy_f_freq, hy_f_w3, hy_bias, final_norm):
    n_p, l_p, _ = x_prompt.shape
    n_s, l_s, _ = x_sample.shape
    t_p = n_p * l_p
    groups = ((0, n_p, l_p), (t_p, n_s, l_s))
    x = jnp.concatenate([x_prompt.reshape(t_p, D_MODEL), x_sample.reshape(n_s * l_s, D_MODEL)], axis=0)

    row = lambda a: a[:, None, :]
    cast = lambda a: a.astype(bf16)
    ffn1 = (row(ffn1_norm), cast(ffn1_w_gate), cast(ffn1_w_up), cast(ffn1_w_down))
    ffn2 = (row(ffn2_norm), cast(ffn2_w_gate), cast(ffn2_w_up), cast(ffn2_w_down))
    mix_g = row(mix_norm)

    ab_in = cast(jnp.pad(ab_w_in, ((0, 0), (0, 0), (0, RW_P_WIDTH - ab_w_in.shape[-1]))))
    ab_out = cast(ab_w_out)
    s5 = dict(lam_re=s5_lambda_re, lam_im=s5_lambda_im, log_step=s5_log_step, b_re=s5_b_re, b_im=s5_b_im,
              c_re=s5_c_re, c_im=s5_c_im, d=row(s5_d), glu_w=cast(s5_glu_w), glu_b=row(s5_glu_b))
    lo = RW_DECAY_LORA
    lora_w = jnp.stack([_pad_rows(rw_w_up[:, 0], 0, RW_LORA_W), _pad_rows(rw_w_up[:, 1], lo, RW_LORA_W),
                        _pad_rows(rw_a_up, 2 * lo, RW_LORA_W),
                        _pad_rows(rw_g_up, 2 * lo + RW_A_LORA, RW_LORA_W)], axis=1)
    rw = dict(mu=row(jnp.pad(rw_mu, ((0, 0), (D_A, RW_P_WIDTH - D_A - rw_mu.shape[-1])))),
              lora_w=cast(lora_w), w0=rw_w0, a0=row(rw_a0), k_k=row(rw_k_k), k_a=row(rw_k_a),
              r_k=rw_r_k.reshape(-1, 1, D_B), ln_w=row(rw_ln_w), ln_b=row(rw_ln_b))

    cd_in = cast(jnp.concatenate([cd_w_in[..., 2 * D_C:], cd_w_in[..., :2 * D_C]], axis=-1))
    cd_out = cast(cd_w_out)
    w_gates = jnp.concatenate([_block_diag_heads(lru_wa[:, 0]), _block_diag_heads(lru_wx[:, 0]),
                               _block_diag_heads(lru_wa[:, 1]), _block_diag_heads(lru_wx[:, 1])], axis=-1)
    b_gates = jnp.concatenate([lru_ba[:, 0], lru_bx[:, 0], lru_ba[:, 1], lru_bx[:, 1]], axis=-1)
    lru = dict(conv_w=lru_conv_w, conv_b=row(lru_conv_b), lam=lru_lambda, w_gates=cast(w_gates),
               b_gates=row(b_gates))
    hy = dict(conv_w=hy_conv_w, conv_b=row(hy_conv_b), f_w1=hy_f_w1, f_b1=hy_f_b1, f_w2=hy_f_w2,
              f_b2=hy_f_b2, f_freq=hy_f_freq, f_w3=hy_f_w3, bias=hy_bias[:, :, None, :])

    for layer in range(DEPTH):
        j = layer // 2
        x = _ffn(x, *ffn1, layer)
        if layer % 2 == 0:
            proj = _proj(x, mix_g, ab_in, layer, j)
            y_a = _s5_mixer(proj, s5, j, groups)
            y_b = _rwkv7_mixer(proj, rw, j, groups)
            x = _outproj(x, y_a, y_b, ab_out, j)
        else:
            proj = _proj(x, mix_g, cd_in, layer, j)
            a_f, b_f, a_b, b_b = _lru_gates(proj, lru, j, groups)
            h_f, h_b = _lru_scan(a_f, b_f, a_b, b_b, groups)
            y_d = _hyena_mixer(proj, hy, j, groups)
            x = _outproj_odd(x, h_f, h_b, proj, y_d, cd_out, j)
        x = _ffn(x, *ffn2, layer)
    y = _final_norm(x, final_norm[None, :])
    return (y[:t_p].reshape(n_p, l_p, D_MODEL), y[t_p:].reshape(n_s, l_s, D_MODEL))
```
